```python
import jax, jax.numpy as jnp
from jax import lax
import numpy as np

D_MODEL = 1024
BATCH = 8
SEQ = 4096
DEPTH = 4

N_A_LAYERS = DEPTH // 2
N_B_LAYERS = DEPTH - N_A_LAYERS
HEAD_DIM = 64
RWKV_HEADS = D_MODEL // HEAD_DIM
D_DECAY_LORA = max(32, int(round(1.8 * D_MODEL ** 0.5 / 32)) * 32)
D_AAA_LORA = max(32, int(round(1.8 * D_MODEL ** 0.5 / 32)) * 32)
D_MV_LORA = max(32, int(round(1.3 * D_MODEL ** 0.5 / 32)) * 32)
D_GATE_LORA = max(32, int(round(0.6 * D_MODEL ** 0.8 / 32)) * 32)
GN_EPS = 64e-5
N_Q_HEADS = D_MODEL // HEAD_DIM
N_KV_HEADS = 2
GROUP = N_Q_HEADS // N_KV_HEADS
WINDOW = 128
BLOCK = WINDOW
FFN_DIM = 4 * D_MODEL
RMS_EPS = 1e-6

kernel_name = 'yoco_rwkv7_swa_sink_hybrid'


def rms_norm(x, g):
    xf = x.astype(jnp.float32)
    y = xf * lax.rsqrt(jnp.mean(xf * xf, axis=-1, keepdims=True) + RMS_EPS)
    return (y * g.astype(jnp.float32)).astype(x.dtype)


def squared_relu_mlp(h, w1, w2):
    return jnp.square(jax.nn.relu(h @ w1)) @ w2


def token_shift(x):
    return jnp.pad(x, ((0, 0), (1, 0), (0, 0)))[:, :-1]


def wkv7(r, w, k, v, a, b):
    B, T, H, N = r.shape
    s0 = jnp.zeros((B, H, N, N), jnp.float32)

    def step(S, inp):
        r_t, w_t, k_t, v_t, a_t, b_t = inp
        sa = jnp.einsum('bhvk,bhk->bhv', S, a_t)
        S = S * w_t[:, :, None, :] + sa[..., None] * b_t[:, :, None, :] + v_t[..., None] * k_t[:, :, None, :]
        y = jnp.einsum('bhvk,bhk->bhv', S, r_t)
        return S, y

    xs = tuple(jnp.moveaxis(t, 1, 0) for t in (r, w, k, v, a, b))
    _, ys = lax.scan(step, s0, xs)
    return jnp.moveaxis(ys, 0, 1)


def rwkv7_time_mix(h, v_first, vres, mu, w_rkv, w0, w1, w2, a0, a1, a2, g1, g2, k_k, k_a, r_k, gn_g, gn_b, wo):
    f32 = jnp.float32
    B, T, C = h.shape
    H, N = RWKV_HEADS, HEAD_DIM
    xx = token_shift(h) - h
    xs = h[None] + xx[None] * mu[:, None, None, :]
    r, k, v = jnp.einsum('nbtc,ncd->nbtd', xs[:3], w_rkv)
    xv, xw, xa, xg = xs[2], xs[3], xs[4], xs[5]
    w_log = -jax.nn.softplus(-(w0 + jnp.tanh(xw @ w1) @ w2).astype(f32)) - 0.5
    decay = jnp.exp(-jnp.exp(w_log))
    if vres is None:
        v_first = v
    else:
        v0, v1, v2 = vres
        v = v + (v_first - v) * jax.nn.sigmoid(v0 + (xv @ v1) @ v2)
    a = jax.nn.sigmoid(a0 + (xa @ a1) @ a2)
    g = jax.nn.sigmoid(xg @ g1) @ g2

    def split(t):
        return t.reshape(B, T, H, N).astype(f32)

    kk = split(k * k_k)
    kk = kk / jnp.maximum(jnp.sqrt(jnp.sum(kk * kk, axis=-1, keepdims=True)), 1e-12)
    k_mod = k * (1.0 + (a - 1.0) * k_a)
    rh, kh, vh, ah, wh = split(r), split(k_mod), split(v), split(a), split(decay)
    y = wkv7(rh, wh, kh, vh, -kk, kk * ah)
    mean = jnp.mean(y, axis=-1, keepdims=True)
    var = jnp.mean(jnp.square(y - mean), axis=-1, keepdims=True)
    y = (y - mean) * lax.rsqrt(var + GN_EPS) * gn_g.reshape(H, N).astype(f32) + gn_b.reshape(H, N).astype(f32)
    y = y + jnp.sum(rh * kh * r_k.astype(f32), axis=-1, keepdims=True) * vh
    out = (y.reshape(B, T, C).astype(h.dtype) * g) @ wo
    return out, v_first


def shared_kv(x, kv_norm, w_kv, k_gain):
    B, T, _ = x.shape
    kv = (rms_norm(x, kv_norm) @ w_kv).reshape(B, T, 2, N_KV_HEADS, HEAD_DIM)
    k = rms_norm(kv[:, :, 0], k_gain)
    v = kv[:, :, 1]
    return k, v


def swa_sink_attention(q, k, v, sinks):
    B, T, _, D = q.shape
    nb = T // BLOCK
    qb = q.reshape(B, nb, BLOCK, N_KV_HEADS, GROUP, D)
    kb = k.reshape(B, nb, BLOCK, N_KV_HEADS, D)
    vb = v.reshape(B, nb, BLOCK, N_KV_HEADS, D)
    pad = ((0, 0), (1, 0), (0, 0), (0, 0), (0, 0))
    kcat = jnp.concatenate([jnp.pad(kb, pad)[:, :-1], kb], axis=2)
    vcat = jnp.concatenate([jnp.pad(vb, pad)[:, :-1], vb], axis=2)
    s = jnp.einsum('bnqhgd,bnshd->bnhgqs', qb, kcat, preferred_element_type=jnp.float32)
    qi = jnp.arange(BLOCK)[:, None]
    kj = jnp.arange(2 * BLOCK)[None, :]
    dist = qi + BLOCK - kj
    blk = jnp.arange(nb)[:, None, None]
    valid = (dist >= 0)[None] & (dist < WINDOW)[None] & ((blk > 0) | (kj >= BLOCK)[None])
    slopes = jnp.exp2(-8.0 * jnp.arange(1, N_Q_HEADS + 1, dtype=jnp.float32) / N_Q_HEADS)
    alibi = slopes.reshape(N_KV_HEADS, GROUP)[:, :, None, None] * dist.astype(jnp.float32)
    s = jnp.where(valid[None, :, None, None], s - alibi, -jnp.inf)
    sink = sinks.astype(jnp.float32).reshape(N_KV_HEADS, GROUP)[None, None, :, :, None, None]
    m = jnp.maximum(jnp.max(s, axis=-1, keepdims=True), sink)
    p = jnp.exp(s - m)
    p = p / (jnp.sum(p, axis=-1, keepdims=True) + jnp.exp(sink - m))
    o = jnp.einsum('bnhgqs,bnshd->bnqhgd', p.astype(v.dtype), vcat)
    return o.reshape(B, T, N_Q_HEADS * D)


def swa_layer(h, k_sh, v_sh, wq, q_gain, sinks, wo):
    B, T, _ = h.shape
    q = (h @ wq).reshape(B, T, N_Q_HEADS, HEAD_DIM)
    q = rms_norm(q, q_gain) * (HEAD_DIM ** -0.5)
    return swa_sink_attention(q, k_sh, v_sh, sinks) @ wo


def setup_inputs(seed: int = 0) -> dict:
    key = jax.random.key(seed)
    ks = iter(jax.random.split(key, 40))
    f32 = jnp.float32
    C, F, NA, NB = D_MODEL, FFN_DIM, N_A_LAYERS, N_B_LAYERS

    def nrm(shape, scale):
        return jax.random.normal(next(ks), shape, f32) * scale

    def gain(shape):
        return 1.0 + nrm(shape, 0.02)

    return {
        'x': nrm((BATCH, SEQ, C), 1.0),
        'ln_mix': gain((DEPTH, C)),
        'ln_mlp': gain((DEPTH, C)),
        'mlp_w1': nrm((DEPTH, C, F), C ** -0.5),
        'mlp_w2': nrm((DEPTH, F, C), F ** -0.5),
        'a_mu': jax.random.uniform(next(ks), (NA, 6, C), f32),
        'a_w_rkv': nrm((NA, 3, C, C), C ** -0.5),
        'a_w0': jax.random.uniform(next(ks), (NA, C), f32, -6.0, -0.5),
        'a_w1': nrm((NA, C, D_DECAY_LORA), C ** -0.5),
        'a_w2': nrm((NA, D_DECAY_LORA, C), 0.1 * D_DECAY_LORA ** -0.5),
        'a_a0': nrm((NA, C), 0.5),
        'a_a1': nrm((NA, C, D_AAA_LORA), C ** -0.5),
        'a_a2': nrm((NA, D_AAA_LORA, C), 0.1 * D_AAA_LORA ** -0.5),
        'a_g1': nrm((NA, C, D_GATE_LORA), C ** -0.5),
        'a_g2': nrm((NA, D_GATE_LORA, C), D_GATE_LORA ** -0.5),
        'a_k_k': 0.85 + nrm((NA, C), 0.05),
        'a_k_a': 1.0 + nrm((NA, C), 0.05),
        'a_r_k': nrm((NA, RWKV_HEADS, HEAD_DIM), 0.1),
        'a_gn_g': gain((NA, C)),
        'a_gn_b': nrm((NA, C), 0.02),
        'a_wo': nrm((NA, C, C), C ** -0.5),
        'a_v0': 1.0 + nrm((NA - 1, C), 0.1),
        'a_v1': nrm((NA - 1, C, D_MV_LORA), C ** -0.5),
        'a_v2': nrm((NA - 1, D_MV_LORA, C), 0.1 * D_MV_LORA ** -0.5),
        'kv_norm': gain((C,)),
        'w_kv': nrm((C, 2 * N_KV_HEADS * HEAD_DIM), C ** -0.5),
        'k_gain': gain((HEAD_DIM,)),
        'b_wq': nrm((NB, C, N_Q_HEADS * HEAD_DIM), C ** -0.5),
        'b_q_gain': gain((NB, HEAD_DIM)),
        'b_sinks': nrm((NB, N_Q_HEADS), 0.5),
        'b_wo': nrm((NB, N_Q_HEADS * HEAD_DIM, C), (N_Q_HEADS * HEAD_DIM) ** -0.5),
    }


def reference(x, ln_mix, ln_mlp, mlp_w1, mlp_w2, a_mu, a_w_rkv, a_w0, a_w1, a_w2, a_a0, a_a1, a_a2, a_g1, a_g2, a_k_k, a_k_a, a_r_k, a_gn_g, a_gn_b, a_wo, a_v0, a_v1, a_v2, kv_norm, w_kv, k_gain, b_wq, b_q_gain, b_sinks, b_wo):
    v_first = None
    k_sh = None
    v_sh = None
    for i in range(DEPTH):
        h = rms_norm(x, ln_mix[i])
        if i < N_A_LAYERS:
            j = i
            vres = None if j == 0 else (a_v0[j - 1], a_v1[j - 1], a_v2[j - 1])
            out, v_first = rwkv7_time_mix(h, v_first, vres, a_mu[j], a_w_rkv[j], a_w0[j], a_w1[j], a_w2[j], a_a0[j], a_a1[j], a_a2[j], a_g1[j], a_g2[j], a_k_k[j], a_k_a[j], a_r_k[j], a_gn_g[j], a_gn_b[j], a_wo[j])
        else:
            j = i - N_A_LAYERS
            out = swa_layer(h, k_sh, v_sh, b_wq[j], b_q_gain[j], b_sinks[j], b_wo[j])
        x = x + out
        x = x + squared_relu_mlp(rms_norm(x, ln_mlp[i]), mlp_w1[i], mlp_w2[i])
        if i == N_A_LAYERS - 1:
            k_sh, v_sh = shared_kv(x, kv_norm, w_kv, k_gain)
    return x
```

```python
import functools

import jax
import jax.numpy as jnp
from jax import lax
from jax.experimental import pallas as pl
from jax.experimental.pallas import tpu as pltpu

F32 = jnp.float32
BF16 = jnp.bfloat16

HEAD = 64
PAIR = 2 * HEAD
N_KV = 2
WINDOW = 128
CHUNK = 64
GN_EPS = 64e-5
RMS_EPS = 1e-6
NEG_BIG = -1e30
VMEM_LIMIT = 56 * 1024 * 1024

NT_DIMS = (((1,), (1,)), ((), ()))


def _dot(a, b):
    return jnp.dot(a, b, preferred_element_type=F32)


def _dot_nt(a, b):
    return lax.dot_general(a, b, NT_DIMS, preferred_element_type=F32)


def _rms(x, g):
    return x * lax.rsqrt(jnp.mean(x * x, axis=-1, keepdims=True) + RMS_EPS) * g


def _group_sum(x, ones):
    w = ones.shape[0]
    parts = [_dot(x[:, j:j + w].astype(BF16), ones) for j in range(0, x.shape[1], w)]
    return parts[0] if len(parts) == 1 else jnp.concatenate(parts, axis=1)


def _sigmoid(x):
    return 1.0 / (1.0 + jnp.exp(-x))


def _const_spec(shape):
    nd = len(shape)
    return pl.BlockSpec(shape, lambda *_: (0,) * nd, pipeline_mode=pl.Buffered(1))


_V_MU, _V_LN, _V_W0, _V_A0, _V_V0, _V_KK, _V_KA = 0, 6, 7, 8, 9, 10, 11


def _rwkv_pre_kernel(has_vres, tm, *refs):
    if has_vres:
        (x_ref, vf_ref, vec_ref, wrkv_ref, w1_ref, w2_ref, a1_ref, a2_ref, g1_ref, g2_ref,
         v1_ref, v2_ref, ones_ref,
         r_ref, lw_ref, k_ref, v_ref, av_ref, kk_ref, g_ref, carry_ref) = refs
    else:
        (x_ref, vec_ref, wrkv_ref, w1_ref, w2_ref, a1_ref, a2_ref, g1_ref, g2_ref, ones_ref,
         r_ref, lw_ref, k_ref, v_ref, av_ref, kk_ref, g_ref, carry_ref) = refs

    @pl.when(pl.program_id(1) == 0)
    def _():
        carry_ref[...] = jnp.zeros_like(carry_ref)

    def vec(i):
        return vec_ref[i:i + 1, :]

    h = _rms(x_ref[0], vec(_V_LN))
    row = lax.broadcasted_iota(jnp.int32, h.shape, 0)
    prev = jnp.where(row == 0, carry_ref[0:1, :], pltpu.roll(h, 1, axis=0))
    carry_ref[0:1, :] = h[tm - 1:tm, :]
    xx = prev - h

    def mix(i):
        return (h + xx * vec(_V_MU + i)).astype(BF16)

    r = _dot(mix(0), wrkv_ref[0])
    k = _dot(mix(1), wrkv_ref[1])
    xv = mix(2)
    v = _dot(xv, wrkv_ref[2])

    wl = vec(_V_W0) + _dot(jnp.tanh(_dot(mix(3), w1_ref[...])).astype(BF16), w2_ref[...])
    w_log = -(jnp.maximum(-wl, 0.0) + jnp.log1p(jnp.exp(-jnp.abs(wl)))) - 0.5
    lw_ref[0] = -jnp.exp(w_log)

    if has_vres:
        gate = _sigmoid(vec(_V_V0) + _dot(_dot(xv, v1_ref[...]).astype(BF16), v2_ref[...]))
        v = v + (vf_ref[0] - v) * gate
    a = _sigmoid(vec(_V_A0) + _dot(_dot(mix(4), a1_ref[...]).astype(BF16), a2_ref[...]))
    g_ref[0] = _dot(_sigmoid(_dot(mix(5), g1_ref[...])).astype(BF16), g2_ref[...])

    kk = k * vec(_V_KK)
    norm = jnp.sqrt(_group_sum(kk * kk, ones_ref[...]))
    kk_ref[0] = kk / jnp.maximum(norm, 1e-12)
    k_ref[0] = k * (1.0 + (a - 1.0) * vec(_V_KA))
    r_ref[0] = r
    v_ref[0] = v
    av_ref[0] = a


def _rwkv_pre(x, v_first, vecs, wrkv, loras, ones, tm):
    B, T, C = x.shape
    has_vres = v_first is not None
    act = pl.BlockSpec((1, tm, C), lambda b, t: (b, t, 0))
    ins = [x] + ([v_first] if has_vres else []) + [vecs, wrkv] + list(loras) + [ones]
    in_specs = [act] * (2 if has_vres else 1) + [_const_spec(a.shape) for a in ins[(2 if has_vres else 1):]]
    out = jax.ShapeDtypeStruct((B, T, C), F32)
    return pl.pallas_call(
        functools.partial(_rwkv_pre_kernel, has_vres, tm),
        grid=(B, T // tm),
        in_specs=in_specs,
        out_specs=[act] * 7,
        out_shape=[out] * 7,
        scratch_shapes=[pltpu.VMEM((8, C), F32)],
        compiler_params=pltpu.CompilerParams(
            dimension_semantics=("parallel", "arbitrary"), vmem_limit_bytes=VMEM_LIMIT),
        name="rwkv_pre",
    )(*ins)


def _cumsum_rows(x):
    row = lax.broadcasted_iota(jnp.int32, x.shape, 0)
    s = 1
    while s < x.shape[0]:
        x = x + jnp.where(row >= s, pltpu.roll(x, s, axis=0), 0.0)
        s *= 2
    return x


def _wkv_chunk(r, lw, k, v, av, kk, pvec, ones, h0):
    L = CHUNK
    bf = lambda t: t.astype(BF16)
    lane = lax.broadcasted_iota(jnp.int32, (L, PAIR), 1)
    head0 = lane < HEAD
    row = lax.broadcasted_iota(jnp.int32, (L, 2 * L), 0)
    col = lax.broadcasted_iota(jnp.int32, (L, 2 * L), 1)
    src = jnp.where(col >= L, col - L, col)
    strict = src < row
    incl = src <= row
    left = col < L

    def bd_rows(x):
        return jnp.concatenate([jnp.where(head0, x, 0.0), jnp.where(head0, 0.0, x)], axis=0)

    def bd_cols(x):
        return jnp.concatenate([jnp.where(left, x, 0.0), jnp.where(left, 0.0, x)], axis=0)

    a = -kk
    b = kk * av
    c = _cumsum_rows(lw)
    ec = jnp.exp(c)
    enc = jnp.exp(-c)
    rt = r * ec
    at = a * jnp.exp(c - lw)
    kt = k * enc
    bt = b * enc
    p_last = ec[L - 1:L, :]

    sc = _dot_nt(bf(jnp.concatenate([at, rt], axis=0)),
                 bf(jnp.concatenate([bd_rows(bt), bd_rows(kt)], axis=0)))
    a_ab = jnp.where(strict, sc[:L, :2 * L], 0.0)
    a_ak = jnp.where(strict, sc[:L, 2 * L:], 0.0)
    a_rb = jnp.where(incl, sc[L:, :2 * L], 0.0)
    a_rk = jnp.where(incl, sc[L:, 2 * L:], 0.0)

    bdv = bd_rows(v)
    akv = _dot(bf(a_ak), bf(bdv))
    z = jnp.concatenate([at, akv], axis=1)

    def bd_rows2(x):
        return jnp.concatenate([bd_rows(x[:, :PAIR]), bd_rows(x[:, PAIR:])], axis=1)

    p = a_ab
    s = 1
    while True:
        z = z + _dot(bf(p), bf(bd_rows2(z)))
        s *= 2
        if s >= L:
            break
        p = _dot(bf(p), bf(bd_cols(p)))
    ah, uh = z[:, :PAIR], z[:, PAIR:]

    bdz = bd_rows2(z)
    zv = jnp.concatenate([jnp.zeros((2 * L, PAIR), F32), bdv], axis=1)
    o2 = _dot(bf(jnp.concatenate([a_rb, a_rk], axis=1)),
              bf(jnp.concatenate([bdz, zv], axis=0)))
    rh = rt + o2[:, :PAIR]
    yh = o2[:, PAIR:]

    lhs3 = jnp.concatenate([bt * p_last, kt * p_last], axis=0)
    rhs3 = jnp.concatenate([z, jnp.concatenate([jnp.zeros((L, PAIR), F32), v], axis=1)], axis=0)
    o3 = _dot(bf(lhs3.T), bf(rhs3))
    r128 = lax.broadcasted_iota(jnp.int32, (PAIR, PAIR), 0)
    c128 = lax.broadcasted_iota(jnp.int32, (PAIR, PAIR), 1)
    same_head = (r128 < HEAD) == (c128 < HEAD)
    eye_p = jnp.where(r128 == c128, p_last, 0.0)
    m = jnp.where(same_head, o3[:, :PAIR], 0.0) + eye_p
    g = jnp.where(same_head, o3[:, PAIR:], 0.0)

    o = _dot(bf(jnp.concatenate([m, rh], axis=0)), bf(h0))
    h1 = o[:PAIR] + g
    y = o[PAIR:] + yh

    inv_n = 1.0 / HEAD
    mean = _group_sum(y, ones) * inv_n
    d = y - mean
    var = _group_sum(d * d, ones) * inv_n
    bonus = _group_sum(r * k * pvec[0:1, :], ones)
    out = d * lax.rsqrt(var + GN_EPS) * pvec[1:2, :] + pvec[2:3, :] + bonus * v
    return out, h1


def _wkv_kernel(n_chunks, r_ref, lw_ref, k_ref, v_ref, av_ref, kk_ref, pvec_ref, ones_ref, y_ref, h_ref):
    @pl.when(pl.program_id(2) == 0)
    def _():
        h_ref[...] = jnp.zeros_like(h_ref)

    pvec = pvec_ref[...]
    ones = ones_ref[...]

    def body(ci, carry):
        sl = (0, pl.ds(pl.multiple_of(ci * CHUNK, CHUNK), CHUNK), slice(None))
        out, h1 = _wkv_chunk(r_ref[sl], lw_ref[sl], k_ref[sl], v_ref[sl], av_ref[sl], kk_ref[sl],
                             pvec, ones, h_ref[...])
        h_ref[...] = h1
        y_ref[sl] = out
        return carry

    lax.fori_loop(0, n_chunks, body, 0)


def _wkv(r, lw, k, v, av, kk, pvec, ones, tt):
    B, T, C = r.shape
    act = pl.BlockSpec((1, tt, PAIR), lambda b, p, t: (b, t, p))
    return pl.pallas_call(
        functools.partial(_wkv_kernel, tt // CHUNK),
        grid=(B, C // PAIR, T // tt),
        in_specs=[act] * 6 + [pl.BlockSpec((8, PAIR), lambda b, p, t: (0, p)),
                              pl.BlockSpec(ones.shape, lambda b, p, t: (0, 0))],
        out_specs=act,
        out_shape=jax.ShapeDtypeStruct((B, T, C), F32),
        scratch_shapes=[pltpu.VMEM((PAIR, PAIR), F32)],
        compiler_params=pltpu.CompilerParams(
            dimension_semantics=("parallel", "parallel", "arbitrary"), vmem_limit_bytes=VMEM_LIMIT),
        name="wkv",
    )(r, lw, k, v, av, kk, pvec, ones)


def _post_mlp_kernel(has_gate, *refs):
    if has_gate:
        x_ref, y_ref, g_ref, wo_ref, ln_ref, w1_ref, w2_ref, o_ref = refs
        y = (y_ref[...] * g_ref[...]).astype(BF16)
    else:
        x_ref, y_ref, wo_ref, ln_ref, w1_ref, w2_ref, o_ref = refs
        y = y_ref[...].astype(BF16)
    xn = x_ref[...] + _dot(y, wo_ref[...])
    hid = _dot(_rms(xn, ln_ref[...]).astype(BF16), w1_ref[...])
    hid = jnp.square(jnp.maximum(hid, 0.0)).astype(BF16)
    o_ref[...] = xn + _dot(hid, w2_ref[...])


def _post_mlp(x, y, g, wo, ln, w1, w2, tm):
    M, C = x.shape
    act = pl.BlockSpec((tm, C), lambda i: (i, 0))
    has_gate = g is not None
    ins = [x, y] + ([g] if has_gate else []) + [wo, ln, w1, w2]
    n_act = 3 if has_gate else 2
    return pl.pallas_call(
        functools.partial(_post_mlp_kernel, has_gate),
        grid=(M // tm,),
        in_specs=[act] * n_act + [_const_spec(a.shape) for a in ins[n_act:]],
        out_specs=act,
        out_shape=jax.ShapeDtypeStruct((M, C), F32),
        compiler_params=pltpu.CompilerParams(
            dimension_semantics=("parallel",), vmem_limit_bytes=VMEM_LIMIT),
        name="post_mlp",
    )(*ins)


def _kv_proj_kernel(x_ref, ln_ref, w_ref, kg_ref, ones_ref, k_ref, v_ref):
    kv = _dot(_rms(x_ref[...], ln_ref[...]).astype(BF16), w_ref[...])
    half = kv.shape[1] // 2
    k = kv[:, :half]
    ms = _group_sum(k * k, ones_ref[...]) * (1.0 / HEAD)
    k_ref[...] = (k * lax.rsqrt(ms + RMS_EPS) * kg_ref[...]).astype(BF16)
    v_ref[...] = kv[:, half:].astype(BF16)


def _kv_proj(x, ln, w, kg, ones, tm):
    M, C = x.shape
    n = w.shape[1] // 2
    out = jax.ShapeDtypeStruct((M, n), BF16)
    return pl.pallas_call(
        _kv_proj_kernel,
        grid=(M // tm,),
        in_specs=[pl.BlockSpec((tm, C), lambda i: (i, 0))] + [_const_spec(a.shape) for a in (ln, w, kg, ones)],
        out_specs=[pl.BlockSpec((tm, n), lambda i: (i, 0))] * 2,
        out_shape=[out, out],
        compiler_params=pltpu.CompilerParams(
            dimension_semantics=("parallel",), vmem_limit_bytes=VMEM_LIMIT),
        name="kv_proj",
    )(x, ln, w, kg, ones)


def _q_proj_kernel(x_ref, ln_ref, w_ref, qg_ref, ones_ref, q_ref):
    q = _dot(_rms(x_ref[...], ln_ref[...]).astype(BF16), w_ref[...])
    ms = _group_sum(q * q, ones_ref[...]) * (1.0 / HEAD)
    q_ref[...] = (q * lax.rsqrt(ms + RMS_EPS) * qg_ref[...] * (HEAD ** -0.5)).astype(BF16)


def _q_proj(x, ln, w, qg, ones, tm):
    M, C = x.shape
    return pl.pallas_call(
        _q_proj_kernel,
        grid=(M // tm,),
        in_specs=[pl.BlockSpec((tm, C), lambda i: (i, 0))] + [_const_spec(a.shape) for a in (ln, w, qg, ones)],
        out_specs=pl.BlockSpec((tm, w.shape[1]), lambda i: (i, 0)),
        out_shape=jax.ShapeDtypeStruct((M, w.shape[1]), BF16),
        compiler_params=pltpu.CompilerParams(
            dimension_semantics=("parallel",), vmem_limit_bytes=VMEM_LIMIT),
        name="q_proj",
    )(x, ln, w, qg, ones)


def _swa_kernel(n_q_heads, slopes_ref, sinks_ref, q_ref, kp_ref, kc_ref, vp_ref, vc_ref, o_ref):
    W = WINDOW
    group = n_q_heads // N_KV
    pairs = group // 2
    blk = pl.program_id(1)
    qi = lax.broadcasted_iota(jnp.int32, (W, 2 * W), 0)
    kj = lax.broadcasted_iota(jnp.int32, (W, 2 * W), 1)
    dist = qi + W - kj
    first_key = jnp.where(blk > 0, 0, W)
    valid = (dist >= 0) & (dist < WINDOW) & (kj >= first_key)
    distf = dist.astype(F32)
    lane = lax.broadcasted_iota(jnp.int32, (W, PAIR), 1)
    head0 = lane < HEAD
    lane2 = lax.broadcasted_iota(jnp.int32, (2 * W, PAIR), 1)
    head0_kv = lane2 < HEAD
    zero = jnp.zeros((), BF16)

    for h in range(N_KV):
        ks = slice(h * PAIR, (h + 1) * PAIR)
        k2 = jnp.concatenate([kp_ref[0, :, ks], kc_ref[0, :, ks]], axis=0)
        v2 = jnp.concatenate([vp_ref[0, :, ks], vc_ref[0, :, ks]], axis=0)
        q_tiles = [q_ref[0, :, (h * pairs + j) * PAIR:(h * pairs + j + 1) * PAIR] for j in range(pairs)]
        lhs = jnp.concatenate([jnp.where(head0, t, zero) for t in q_tiles]
                              + [jnp.where(head0, zero, t) for t in q_tiles], axis=0)
        s_all = _dot_nt(lhs, k2)
        probs, denoms = [], []
        for i in range(group):
            head = h * group + 2 * (i % pairs) + i // pairs
            s = s_all[i * W:(i + 1) * W, :] - slopes_ref[head] * distf
            s = jnp.where(valid, s, NEG_BIG)
            sink = sinks_ref[head]
            mx = jnp.maximum(jnp.max(s, axis=-1, keepdims=True), sink)
            p = jnp.exp(s - mx)
            denoms.append(jnp.sum(p, axis=-1, keepdims=True) + jnp.exp(sink - mx))
            probs.append(p.astype(BF16))
        p_first = jnp.concatenate(probs[:pairs], axis=0)
        p_second = jnp.concatenate(probs[pairs:], axis=0)
        v_stack = jnp.concatenate([jnp.where(head0_kv, v2, zero), jnp.where(head0_kv, zero, v2)], axis=0)
        o = _dot(jnp.concatenate([p_first, p_second], axis=1), v_stack)
        for j in range(pairs):
            den = jnp.where(head0, denoms[j], denoms[pairs + j])
            col = (h * pairs + j) * PAIR
            o_ref[0, :, col:col + PAIR] = (o[j * W:(j + 1) * W, :] / den).astype(o_ref.dtype)


def _swa(q, k2, v2, slopes, sinks):
    B, T, C = q.shape
    W = WINDOW
    kvw = k2.shape[-1]
    cur = pl.BlockSpec((1, W, kvw), lambda b, n, *_: (b, n, 0))
    prev = pl.BlockSpec((1, W, kvw), lambda b, n, *_: (b, jnp.maximum(n - 1, 0), 0))
    qspec = pl.BlockSpec((1, W, C), lambda b, n, *_: (b, n, 0))
    return pl.pallas_call(
        functools.partial(_swa_kernel, C // HEAD),
        grid_spec=pltpu.PrefetchScalarGridSpec(
            num_scalar_prefetch=2,
            grid=(B, T // W),
            in_specs=[qspec, prev, cur, prev, cur],
            out_specs=qspec,
        ),
        out_shape=jax.ShapeDtypeStruct((B, T, C), BF16),
        compiler_params=pltpu.CompilerParams(
            dimension_semantics=("parallel", "arbitrary"), vmem_limit_bytes=VMEM_LIMIT),
        name="swa",
    )(slopes, sinks, q, k2, k2, v2, v2)


def _pad_cols(w, n):
    return jnp.pad(w, ((0, 0), (0, n - w.shape[1])))


def _pad_rows(w, n):
    return jnp.pad(w, ((0, n - w.shape[0]), (0, 0)))


def _round_up(n, m):
    return (n + m - 1) // m * m


def _lora(w_in, w_out):
    n = _round_up(w_in.shape[1], 128)
    return _pad_cols(w_in, n).astype(BF16), _pad_rows(w_out, n).astype(BF16)


def _dup_heads(w):
    c, n = w.shape
    w = w.reshape(c, n // HEAD, 1, HEAD)
    return jnp.broadcast_to(w, (c, n // HEAD, 2, HEAD)).reshape(c, 2 * n)


def _block_ones(n):
    i = jnp.arange(n) // HEAD
    return (i[:, None] == i[None, :]).astype(BF16)


def kernel(x, ln_mix, ln_mlp, mlp_w1, mlp_w2, a_mu, a_w_rkv, a_w0, a_w1, a_w2, a_a0, a_a1, a_a2, a_g1, a_g2, a_k_k, a_k_a, a_r_k, a_gn_g, a_gn_b, a_wo, a_v0, a_v1, a_v2, kv_norm, w_kv, k_gain, b_wq, b_q_gain, b_sinks, b_wo):
    B, T, C = x.shape
    M = B * T
    n_a = a_mu.shape[0]
    n_b = b_wq.shape[0]
    n_heads = C // HEAD
    tm = 256
    tt = 512
    ones256 = _block_ones(256)
    ones128 = _block_ones(PAIR)
    row = lambda v: v.reshape(1, -1).astype(F32)

    v_first = None
    k2 = v2 = None
    slopes = jnp.exp2(-8.0 * jnp.arange(1, n_heads + 1, dtype=F32) / n_heads)
    for i in range(n_a + n_b):
        if i < n_a:
            j = i
            zeros = jnp.zeros((1, C), F32)
            vecs = jnp.concatenate(
                [a_mu[j], row(ln_mix[i]), row(a_w0[j]), row(a_a0[j]),
                 row(a_v0[j - 1]) if j > 0 else zeros, row(a_k_k[j]), row(a_k_a[j])]
                + [zeros] * 4, axis=0)
            loras = list(_lora(a_w1[j], a_w2[j]) + _lora(a_a1[j], a_a2[j]) + _lora(a_g1[j], a_g2[j]))
            if j > 0:
                loras += list(_lora(a_v1[j - 1], a_v2[j - 1]))
            r, lw, k, v, av, kk, g = _rwkv_pre(
                x, v_first if j > 0 else None, vecs, a_w_rkv[j].astype(BF16), loras, ones256, tm)
            if j == 0:
                v_first = v
            pvec = jnp.concatenate([row(a_r_k[j]), row(a_gn_g[j]), row(a_gn_b[j])]
                                   + [jnp.zeros((1, C), F32)] * 5, axis=0)
            y = _wkv(r, lw, k, v, av, kk, pvec, ones128, tt)
            x = _post_mlp(x.reshape(M, C), y.reshape(M, C), g.reshape(M, C), a_wo[j].astype(BF16),
                          row(ln_mlp[i]), mlp_w1[i].astype(BF16), mlp_w2[i].astype(BF16), tm).reshape(B, T, C)
            if i == n_a - 1:
                kg = jnp.tile(row(k_gain), (1, 2 * N_KV))
                k2, v2 = _kv_proj(x.reshape(M, C), row(kv_norm), _dup_heads(w_kv).astype(BF16), kg, ones256, tm)
                k2 = k2.reshape(B, T, -1)
                v2 = v2.reshape(B, T, -1)
        else:
            j = i - n_a
            qg = jnp.tile(row(b_q_gain[j]), (1, n_heads))
            q = _q_proj(x.reshape(M, C), row(ln_mix[i]), b_wq[j].astype(BF16), qg, ones256, tm)
            o = _swa(q.reshape(B, T, C), k2, v2, slopes, b_sinks[j].astype(F32))
            x = _post_mlp(x.reshape(M, C), o.reshape(M, C), None, b_wo[j].astype(BF16),
                          row(ln_mlp[i]), mlp_w1[i].astype(BF16), mlp_w2[i].astype(BF16), tm).reshape(B, T, C)
    return x
```

```python
import functools

import jax
import jax.numpy as jnp
from jax import lax
from jax.experimental import pallas as pl
from jax.experimental.pallas import tpu as pltpu

F32 = jnp.float32
BF16 = jnp.bfloat16

HEAD = 64
PAIR = 2 * HEAD
N_KV = 2
WINDOW = 128
CHUNK = 64
GN_EPS = 64e-5
RMS_EPS = 1e-6
NEG_BIG = -1e30
VMEM_LIMIT = 56 * 1024 * 1024

NT_DIMS = (((1,), (1,)), ((), ()))


def _dot(a, b):
    return jnp.dot(a, b, preferred_element_type=F32)


def _dot_nt(a, b):
    return lax.dot_general(a, b, NT_DIMS, preferred_element_type=F32)


def _rms(x, g):
    return x * lax.rsqrt(jnp.mean(x * x, axis=-1, keepdims=True) + RMS_EPS) * g


def _group_sum(x, ones):
    w = ones.shape[0]
    parts = [_dot(x[:, j:j + w].astype(BF16), ones) for j in range(0, x.shape[1], w)]
    return parts[0] if len(parts) == 1 else jnp.concatenate(parts, axis=1)


def _sigmoid(x):
    return 1.0 / (1.0 + jnp.exp(-x))


def _const_spec(shape):
    nd = len(shape)
    return pl.BlockSpec(shape, lambda *_: (0,) * nd, pipeline_mode=pl.Buffered(1))


_V_MU, _V_LN, _V_W0, _V_A0, _V_V0, _V_KK, _V_KA = 0, 6, 7, 8, 9, 10, 11


def _rwkv_pre_kernel(has_vres, tm, *refs):
    if has_vres:
        (x_ref, vf_ref, vec_ref, wrkv_ref, w1_ref, w2_ref, a1_ref, a2_ref, g1_ref, g2_ref,
         v1_ref, v2_ref, ones_ref,
         r_ref, lw_ref, k_ref, v_ref, av_ref, kk_ref, g_ref, carry_ref) = refs
    else:
        (x_ref, vec_ref, wrkv_ref, w1_ref, w2_ref, a1_ref, a2_ref, g1_ref, g2_ref, ones_ref,
         r_ref, lw_ref, k_ref, v_ref, av_ref, kk_ref, g_ref, carry_ref) = refs

    @pl.when(pl.program_id(1) == 0)
    def _():
        carry_ref[...] = jnp.zeros_like(carry_ref)

    def vec(i):
        return vec_ref[i:i + 1, :]

    h = _rms(x_ref[0], vec(_V_LN))
    row = lax.broadcasted_iota(jnp.int32, h.shape, 0)
    prev = jnp.where(row == 0, carry_ref[0:1, :], pltpu.roll(h, 1, axis=0))
    carry_ref[0:1, :] = h[tm - 1:tm, :]
    xx = prev - h

    def mix(i):
        return (h + xx * vec(_V_MU + i)).astype(BF16)

    r = _dot(mix(0), wrkv_ref[0])
    k = _dot(mix(1), wrkv_ref[1])
    xv = mix(2)
    v = _dot(xv, wrkv_ref[2])

    wl = vec(_V_W0) + _dot(jnp.tanh(_dot(mix(3), w1_ref[...])).astype(BF16), w2_ref[...])
    w_log = -(jnp.maximum(-wl, 0.0) + jnp.log1p(jnp.exp(-jnp.abs(wl)))) - 0.5
    lw_ref[0] = -jnp.exp(w_log)

    if has_vres:
        gate = _sigmoid(vec(_V_V0) + _dot(_dot(xv, v1_ref[...]).astype(BF16), v2_ref[...]))
        v = v + (vf_ref[0] - v) * gate
    a = _sigmoid(vec(_V_A0) + _dot(_dot(mix(4), a1_ref[...]).astype(BF16), a2_ref[...]))
    g_ref[0] = _dot(_sigmoid(_dot(mix(5), g1_ref[...])).astype(BF16), g2_ref[...])

    kk = k * vec(_V_KK)
    norm = jnp.sqrt(_group_sum(kk * kk, ones_ref[...]))
    kk_ref[0] = kk / jnp.maximum(norm, 1e-12)
    k_ref[0] = k * (1.0 + (a - 1.0) * vec(_V_KA))
    r_ref[0] = r
    v_ref[0] = v
    av_ref[0] = a


def _rwkv_pre(x, v_first, vecs, wrkv, loras, ones, tm):
    B, T, C = x.shape
    has_vres = v_first is not None
    act = pl.BlockSpec((1, tm, C), lambda b, t: (b, t, 0))
    ins = [x] + ([v_first] if has_vres else []) + [vecs, wrkv] + list(loras) + [ones]
    in_specs = [act] * (2 if has_vres else 1) + [_const_spec(a.shape) for a in ins[(2 if has_vres else 1):]]
    out = jax.ShapeDtypeStruct((B, T, C), F32)
    return pl.pallas_call(
        functools.partial(_rwkv_pre_kernel, has_vres, tm),
        grid=(B, T // tm),
        in_specs=in_specs,
        out_specs=[act] * 7,
        out_shape=[out] * 7,
        scratch_shapes=[pltpu.VMEM((8, C), F32)],
        compiler_params=pltpu.CompilerParams(
            dimension_semantics=("parallel", "arbitrary"), vmem_limit_bytes=VMEM_LIMIT),
        name="rwkv_pre",
    )(*ins)


def _chunk_cumsum(x):
    pos = lax.broadcasted_iota(jnp.int32, x.shape, 0) & (CHUNK - 1)
    s = 1
    while s < CHUNK:
        x = x + jnp.where(pos >= s, pltpu.roll(x, s, axis=0), 0.0)
        s *= 2
    return x


def _each(f, *lists):
    return [f(*xs) for xs in zip(*lists)]


def _wkv_chunks_pre(at, rt, kt, bt, v, p_last):
    L = CHUNK
    bf = lambda t: t.astype(BF16)
    lane = lax.broadcasted_iota(jnp.int32, (L, PAIR), 1)
    head0 = lane < HEAD
    row = lax.broadcasted_iota(jnp.int32, (L, 2 * L), 0)
    col = lax.broadcasted_iota(jnp.int32, (L, 2 * L), 1)
    src = jnp.where(col >= L, col - L, col)
    strict = src < row
    incl = src <= row
    left = col < L

    def bd_rows(x):
        return jnp.concatenate([jnp.where(head0, x, 0.0), jnp.where(head0, 0.0, x)], axis=0)

    def bd_cols(x):
        return jnp.concatenate([jnp.where(left, x, 0.0), jnp.where(left, 0.0, x)], axis=0)

    def bd_rows2(x):
        return jnp.concatenate([bd_rows(x[:, :PAIR]), bd_rows(x[:, PAIR:])], axis=1)

    sc = _each(lambda a, r, b, k: _dot_nt(bf(jnp.concatenate([a, r], axis=0)),
                                          bf(jnp.concatenate([bd_rows(b), bd_rows(k)], axis=0))),
               at, rt, bt, kt)
    a_ak = [jnp.where(strict, s[:L, 2 * L:], 0.0) for s in sc]
    bdv = _each(bd_rows, v)
    akv = _each(lambda a, b: _dot(bf(a), bf(b)), a_ak, bdv)
    z = _each(lambda a, u: jnp.concatenate([a, u], axis=1), at, akv)

    p = [jnp.where(strict, s[:L, :2 * L], 0.0) for s in sc]
    s = 1
    while 2 * s < L:
        o = _each(lambda pp, zz: _dot(bf(pp), bf(jnp.concatenate([bd_rows2(zz), bd_cols(pp)], axis=1))), p, z)
        z = _each(lambda zz, oo: zz + oo[:, :2 * PAIR], z, o)
        p = [oo[:, 2 * PAIR:] for oo in o]
        s *= 2
    z = _each(lambda pp, zz: zz + _dot(bf(pp), bf(bd_rows2(zz))), p, z)

    zeros2 = jnp.zeros((2 * L, PAIR), F32)
    o2 = _each(lambda s_, zz, bv: _dot(
        bf(jnp.concatenate([jnp.where(incl, s_[L:, :2 * L], 0.0), jnp.where(incl, s_[L:, 2 * L:], 0.0)], axis=1)),
        bf(jnp.concatenate([bd_rows2(zz), jnp.concatenate([zeros2, bv], axis=1)], axis=0))), sc, z, bdv)
    rh = _each(lambda r, oo: r + oo[:, :PAIR], rt, o2)
    yh = [oo[:, PAIR:] for oo in o2]

    zeros1 = jnp.zeros((L, PAIR), F32)
    o3 = _each(lambda b, k, pl_, zz, vv: _dot(
        bf(jnp.concatenate([b * pl_, k * pl_], axis=0).T),
        bf(jnp.concatenate([zz, jnp.concatenate([zeros1, vv], axis=1)], axis=0))), bt, kt, p_last, z, v)
    r128 = lax.broadcasted_iota(jnp.int32, (PAIR, PAIR), 0)
    c128 = lax.broadcasted_iota(jnp.int32, (PAIR, PAIR), 1)
    same_head = (r128 < HEAD) == (c128 < HEAD)
    m = _each(lambda oo, pl_: jnp.where(same_head, oo[:, :PAIR], 0.0) + jnp.where(r128 == c128, pl_, 0.0), o3, p_last)
    g = [jnp.where(same_head, oo[:, PAIR:], 0.0) for oo in o3]
    return rh, yh, m, g


def _wkv_kernel(n_chunks, r_ref, lw_ref, k_ref, v_ref, av_ref, kk_ref, pvec_ref, ones_ref, y_ref, h_ref):
    L = CHUNK

    @pl.when(pl.program_id(2) == 0)
    def _():
        h_ref[...] = jnp.zeros_like(h_ref)

    r, lw, k, v = r_ref[0], lw_ref[0], k_ref[0], v_ref[0]
    kk = kk_ref[0]
    c = _chunk_cumsum(lw)
    ec = jnp.exp(c)
    enc = jnp.exp(-c)
    rt = r * ec
    at = -kk * jnp.exp(c - lw)
    kt = k * enc
    bt = kk * av_ref[0] * enc

    chunks = lambda x: [x[ci * L:(ci + 1) * L] for ci in range(n_chunks)]
    p_last = [ec[(ci + 1) * L - 1:(ci + 1) * L, :] for ci in range(n_chunks)]
    pre = _wkv_chunks_pre(chunks(at), chunks(rt), chunks(kt), chunks(bt), chunks(v), p_last)

    h = h_ref[...]
    ys = []
    for rh, yh, m, g in zip(*pre):
        o = _dot(jnp.concatenate([m, rh], axis=0).astype(BF16), h.astype(BF16))
        h = o[:PAIR] + g
        ys.append(o[PAIR:] + yh)
    h_ref[...] = h
    y = jnp.concatenate(ys, axis=0)

    ones = ones_ref[...]
    inv_n = 1.0 / HEAD
    d = y - _group_sum(y, ones) * inv_n
    var = _group_sum(d * d, ones) * inv_n
    bonus = _group_sum(r * k * pvec_ref[0:1, :], ones)
    y_ref[0] = d * lax.rsqrt(var + GN_EPS) * pvec_ref[1:2, :] + pvec_ref[2:3, :] + bonus * v


def _wkv(r, lw, k, v, av, kk, pvec, ones, tt):
    B, T, C = r.shape
    act = pl.BlockSpec((1, tt, PAIR), lambda b, p, t: (b, t, p))
    return pl.pallas_call(
        functools.partial(_wkv_kernel, tt // CHUNK),
        grid=(B, C // PAIR, T // tt),
        in_specs=[act] * 6 + [pl.BlockSpec((8, PAIR), lambda b, p, t: (0, p)),
                              pl.BlockSpec(ones.shape, lambda b, p, t: (0, 0))],
        out_specs=act,
        out_shape=jax.ShapeDtypeStruct((B, T, C), F32),
        scratch_shapes=[pltpu.VMEM((PAIR, PAIR), F32)],
        compiler_params=pltpu.CompilerParams(
            dimension_semantics=("parallel", "parallel", "arbitrary"), vmem_limit_bytes=VMEM_LIMIT),
        name="wkv",
    )(r, lw, k, v, av, kk, pvec, ones)


def _post_mlp_kernel(has_gate, *refs):
    if has_gate:
        x_ref, y_ref, g_ref, wo_ref, ln_ref, w1_ref, w2_ref, o_ref = refs
        y = (y_ref[...] * g_ref[...]).astype(BF16)
    else:
        x_ref, y_ref, wo_ref, ln_ref, w1_ref, w2_ref, o_ref = refs
        y = y_ref[...].astype(BF16)
    xn = x_ref[...] + _dot(y, wo_ref[...])
    hid = _dot(_rms(xn, ln_ref[...]).astype(BF16), w1_ref[...])
    hid = jnp.square(jnp.maximum(hid, 0.0)).astype(BF16)
    o_ref[...] = xn + _dot(hid, w2_ref[...])


def _post_mlp(x, y, g, wo, ln, w1, w2, tm):
    M, C = x.shape
    act = pl.BlockSpec((tm, C), lambda i: (i, 0))
    has_gate = g is not None
    ins = [x, y] + ([g] if has_gate else []) + [wo, ln, w1, w2]
    n_act = 3 if has_gate else 2
    return pl.pallas_call(
        functools.partial(_post_mlp_kernel, has_gate),
        grid=(M // tm,),
        in_specs=[act] * n_act + [_const_spec(a.shape) for a in ins[n_act:]],
        out_specs=act,
        out_shape=jax.ShapeDtypeStruct((M, C), F32),
        compiler_params=pltpu.CompilerParams(
            dimension_semantics=("parallel",), vmem_limit_bytes=VMEM_LIMIT),
        name="post_mlp",
    )(*ins)


def _kv_proj_kernel(x_ref, ln_ref, w_ref, kg_ref, ones_ref, k_ref, v_ref):
    kv = _dot(_rms(x_ref[...], ln_ref[...]).astype(BF16), w_ref[...])
    half = kv.shape[1] // 2
    k = kv[:, :half]
    ms = _group_sum(k * k, ones_ref[...]) * (1.0 / HEAD)
    k_ref[...] = (k * lax.rsqrt(ms + RMS_EPS) * kg_ref[...]).astype(BF16)
    v_ref[...] = kv[:, half:].astype(BF16)


def _kv_proj(x, ln, w, kg, ones, tm):
    M, C = x.shape
    n = w.shape[1] // 2
    out = jax.ShapeDtypeStruct((M, n), BF16)
    return pl.pallas_call(
        _kv_proj_kernel,
        grid=(M // tm,),
        in_specs=[pl.BlockSpec((tm, C), lambda i: (i, 0))] + [_const_spec(a.shape) for a in (ln, w, kg, ones)],
        out_specs=[pl.BlockSpec((tm, n), lambda i: (i, 0))] * 2,
        out_shape=[out, out],
        compiler_params=pltpu.CompilerParams(
            dimension_semantics=("parallel",), vmem_limit_bytes=VMEM_LIMIT),
        name="kv_proj",
    )(x, ln, w, kg, ones)


def _q_proj_kernel(x_ref, ln_ref, w_ref, qg_ref, ones_ref, q_ref):
    q = _dot(_rms(x_ref[...], ln_ref[...]).astype(BF16), w_ref[...])
    ms = _group_sum(q * q, ones_ref[...]) * (1.0 / HEAD)
    q_ref[...] = (q * lax.rsqrt(ms + RMS_EPS) * qg_ref[...] * (HEAD ** -0.5)).astype(BF16)


def _q_proj(x, ln, w, qg, ones, tm):
    M, C = x.shape
    return pl.pallas_call(
        _q_proj_kernel,
        grid=(M // tm,),
        in_specs=[pl.BlockSpec((tm, C), lambda i: (i, 0))] + [_const_spec(a.shape) for a in (ln, w, qg, ones)],
        out_specs=pl.BlockSpec((tm, w.shape[1]), lambda i: (i, 0)),
        out_shape=jax.ShapeDtypeStruct((M, w.shape[1]), BF16),
        compiler_params=pltpu.CompilerParams(
            dimension_semantics=("parallel",), vmem_limit_bytes=VMEM_LIMIT),
        name="q_proj",
    )(x, ln, w, qg, ones)


def _swa_kernel(n_q_heads, slopes_ref, sinks_ref, q_ref, kp_ref, kc_ref, vp_ref, vc_ref, o_ref):
    W = WINDOW
    group = n_q_heads // N_KV
    pairs = group // 2
    blk = pl.program_id(1)
    qi = lax.broadcasted_iota(jnp.int32, (W, 2 * W), 0)
    kj = lax.broadcasted_iota(jnp.int32, (W, 2 * W), 1)
    dist = qi + W - kj
    first_key = jnp.where(blk > 0, 0, W)
    valid = (dist >= 0) & (dist < WINDOW) & (kj >= first_key)
    distf = dist.astype(F32)
    lane = lax.broadcasted_iota(jnp.int32, (W, PAIR), 1)
    head0 = lane < HEAD
    lane2 = lax.broadcasted_iota(jnp.int32, (2 * W, PAIR), 1)
    head0_kv = lane2 < HEAD
    zero = jnp.zeros((), BF16)

    for h in range(N_KV):
        ks = slice(h * PAIR, (h + 1) * PAIR)
        k2 = jnp.concatenate([kp_ref[0, :, ks], kc_ref[0, :, ks]], axis=0)
        v2 = jnp.concatenate([vp_ref[0, :, ks], vc_ref[0, :, ks]], axis=0)
        q_tiles = [q_ref[0, :, (h * pairs + j) * PAIR:(h * pairs + j + 1) * PAIR] for j in range(pairs)]
        lhs = jnp.concatenate([jnp.where(head0, t, zero) for t in q_tiles]
                              + [jnp.where(head0, zero, t) for t in q_tiles], axis=0)
        s_all = _dot_nt(lhs, k2)
        probs, denoms = [], []
        for i in range(group):
            head = h * group + 2 * (i % pairs) + i // pairs
            s = s_all[i * W:(i + 1) * W, :] - slopes_ref[head] * distf
            s = jnp.where(valid, s, NEG_BIG)
            sink = sinks_ref[head]
            mx = jnp.maximum(jnp.max(s, axis=-1, keepdims=True), sink)
            p = jnp.exp(s - mx)
            denoms.append(jnp.sum(p, axis=-1, keepdims=True) + jnp.exp(sink - mx))
            probs.append(p.astype(BF16))
        p_first = jnp.concatenate(probs[:pairs], axis=0)
        p_second = jnp.concatenate(probs[pairs:], axis=0)
        v_stack = jnp.concatenate([jnp.where(head0_kv, v2, zero), jnp.where(head0_kv, zero, v2)], axis=0)
        o = _dot(jnp.concatenate([p_first, p_second], axis=1), v_stack)
        for j in range(pairs):
            den = jnp.where(head0, denoms[j], denoms[pairs + j])
            col = (h * pairs + j) * PAIR
            o_ref[0, :, col:col + PAIR] = (o[j * W:(j + 1) * W, :] / den).astype(o_ref.dtype)


def _swa(q, k2, v2, slopes, sinks):
    B, T, C = q.shape
    W = WINDOW
    kvw = k2.shape[-1]
    cur = pl.BlockSpec((1, W, kvw), lambda b, n, *_: (b, n, 0))
    prev = pl.BlockSpec((1, W, kvw), lambda b, n, *_: (b, jnp.maximum(n - 1, 0), 0))
    qspec = pl.BlockSpec((1, W, C), lambda b, n, *_: (b, n, 0))
    return pl.pallas_call(
        functools.partial(_swa_kernel, C // HEAD),
        grid_spec=pltpu.PrefetchScalarGridSpec(
            num_scalar_prefetch=2,
            grid=(B, T // W),
            in_specs=[qspec, prev, cur, prev, cur],
            out_specs=qspec,
        ),
        out_shape=jax.ShapeDtypeStruct((B, T, C), BF16),
        compiler_params=pltpu.CompilerParams(
            dimension_semantics=("parallel", "arbitrary"), vmem_limit_bytes=VMEM_LIMIT),
        name="swa",
    )(slopes, sinks, q, k2, k2, v2, v2)


def _pad_cols(w, n):
    return jnp.pad(w, ((0, 0), (0, n - w.shape[1])))


def _pad_rows(w, n):
    return jnp.pad(w, ((0, n - w.shape[0]), (0, 0)))


def _round_up(n, m):
    return (n + m - 1) // m * m


def _lora(w_in, w_out):
    n = _round_up(w_in.shape[1], 128)
    return _pad_cols(w_in, n).astype(BF16), _pad_rows(w_out, n).astype(BF16)


def _dup_heads(w):
    c, n = w.shape
    w = w.reshape(c, n // HEAD, 1, HEAD)
    return jnp.broadcast_to(w, (c, n // HEAD, 2, HEAD)).reshape(c, 2 * n)


def _block_ones(n):
    i = jnp.arange(n) // HEAD
    return (i[:, None] == i[None, :]).astype(BF16)


def kernel(x, ln_mix, ln_mlp, mlp_w1, mlp_w2, a_mu, a_w_rkv, a_w0, a_w1, a_w2, a_a0, a_a1, a_a2, a_g1, a_g2, a_k_k, a_k_a, a_r_k, a_gn_g, a_gn_b, a_wo, a_v0, a_v1, a_v2, kv_norm, w_kv, k_gain, b_wq, b_q_gain, b_sinks, b_wo):
    B, T, C = x.shape
    M = B * T
    n_a = a_mu.shape[0]
    n_b = b_wq.shape[0]
    n_heads = C // HEAD
    tm = 256
    tt = 512
    ones256 = _block_ones(256)
    ones128 = _block_ones(PAIR)
    row = lambda v: v.reshape(1, -1).astype(F32)

    v_first = None
    k2 = v2 = None
    slopes = jnp.exp2(-8.0 * jnp.arange(1, n_heads + 1, dtype=F32) / n_heads)
    for i in range(n_a + n_b):
        if i < n_a:
            j = i
            zeros = jnp.zeros((1, C), F32)
            vecs = jnp.concatenate(
                [a_mu[j], row(ln_mix[i]), row(a_w0[j]), row(a_a0[j]),
                 row(a_v0[j - 1]) if j > 0 else zeros, row(a_k_k[j]), row(a_k_a[j])]
                + [zeros] * 4, axis=0)
            loras = list(_lora(a_w1[j], a_w2[j]) + _lora(a_a1[j], a_a2[j]) + _lora(a_g1[j], a_g2[j]))
            if j > 0:
                loras += list(_lora(a_v1[j - 1], a_v2[j - 1]))
            r, lw, k, v, av, kk, g = _rwkv_pre(
                x, v_first if j > 0 else None, vecs, a_w_rkv[j].astype(BF16), loras, ones256, tm)
            if j == 0:
                v_first = v
            pvec = jnp.concatenate([row(a_r_k[j]), row(a_gn_g[j]), row(a_gn_b[j])]
                                   + [jnp.zeros((1, C), F32)] * 5, axis=0)
            y = _wkv(r, lw, k, v, av, kk, pvec, ones128, tt)
            x = _post_mlp(x.reshape(M, C), y.reshape(M, C), g.reshape(M, C), a_wo[j].astype(BF16),
                          row(ln_mlp[i]), mlp_w1[i].astype(BF16), mlp_w2[i].astype(BF16), tm).reshape(B, T, C)
            if i == n_a - 1:
                kg = jnp.tile(row(k_gain), (1, 2 * N_KV))
                k2, v2 = _kv_proj(x.reshape(M, C), row(kv_norm), _dup_heads(w_kv).astype(BF16), kg, ones256, tm)
                k2 = k2.reshape(B, T, -1)
                v2 = v2.reshape(B, T, -1)
        else:
            j = i - n_a
            qg = jnp.tile(row(b_q_gain[j]), (1, n_heads))
            q = _q_proj(x.reshape(M, C), row(ln_mix[i]), b_wq[j].astype(BF16), qg, ones256, tm)
            o = _swa(q.reshape(B, T, C), k2, v2, slopes, b_sinks[j].astype(F32))
            x = _post_mlp(x.reshape(M, C), o.reshape(M, C), None, b_wo[j].astype(BF16),
                          row(ln_mlp[i]), mlp_w1[i].astype(BF16), mlp_w2[i].astype(BF16), tm).reshape(B, T, C)
    return x
```

```python
import functools

import jax
import jax.numpy as jnp
from jax import lax
from jax.experimental import pallas as pl
from jax.experimental.pallas import tpu as pltpu

F32 = jnp.float32
BF16 = jnp.bfloat16

HEAD = 64
PAIR = 2 * HEAD
N_KV = 2
WINDOW = 128
CHUNK = 64
GN_EPS = 64e-5
RMS_EPS = 1e-6
NEG_BIG = -1e30
VMEM_LIMIT = 56 * 1024 * 1024

NT_DIMS = (((1,), (1,)), ((), ()))


def _dot(a, b):
    return jnp.dot(a, b, preferred_element_type=F32)


def _dot_nt(a, b):
    return lax.dot_general(a, b, NT_DIMS, preferred_element_type=F32)


def _rms(x, g):
    return x * lax.rsqrt(jnp.mean(x * x, axis=-1, keepdims=True) + RMS_EPS) * g


def _group_sum(x, ones):
    w = ones.shape[0]
    parts = [_dot(x[:, j:j + w].astype(BF16), ones) for j in range(0, x.shape[1], w)]
    return parts[0] if len(parts) == 1 else jnp.concatenate(parts, axis=1)


def _sigmoid(x):
    return 1.0 / (1.0 + jnp.exp(-x))


def _const_spec(shape):
    nd = len(shape)
    return pl.BlockSpec(shape, lambda *_: (0,) * nd, pipeline_mode=pl.Buffered(1))


_V_MU, _V_LN, _V_W0, _V_A0, _V_V0, _V_KK, _V_KA = 0, 6, 7, 8, 9, 10, 11


def _rwkv_pre_kernel(has_vres, tm, *refs):
    if has_vres:
        (x_ref, vf_ref, vec_ref, wrkv_ref, w1_ref, w2_ref, a1_ref, a2_ref, g1_ref, g2_ref,
         v1_ref, v2_ref, ones_ref,
         r_ref, lw_ref, k_ref, v_ref, av_ref, kk_ref, g_ref, carry_ref) = refs
    else:
        (x_ref, vec_ref, wrkv_ref, w1_ref, w2_ref, a1_ref, a2_ref, g1_ref, g2_ref, ones_ref,
         r_ref, lw_ref, k_ref, v_ref, av_ref, kk_ref, g_ref, carry_ref) = refs

    @pl.when(pl.program_id(1) == 0)
    def _():
        carry_ref[...] = jnp.zeros_like(carry_ref)

    def vec(i):
        return vec_ref[i:i + 1, :]

    h = _rms(x_ref[0], vec(_V_LN))
    row = lax.broadcasted_iota(jnp.int32, h.shape, 0)
    prev = jnp.where(row == 0, carry_ref[0:1, :], pltpu.roll(h, 1, axis=0))
    carry_ref[0:1, :] = h[tm - 1:tm, :]
    xx = prev - h

    def mix(i):
        return (h + xx * vec(_V_MU + i)).astype(BF16)

    r = _dot(mix(0), wrkv_ref[0])
    k = _dot(mix(1), wrkv_ref[1])
    xv = mix(2)
    v = _dot(xv, wrkv_ref[2])

    wl = vec(_V_W0) + _dot(jnp.tanh(_dot(mix(3), w1_ref[...])).astype(BF16), w2_ref[...])
    w_log = -(jnp.maximum(-wl, 0.0) + jnp.log1p(jnp.exp(-jnp.abs(wl)))) - 0.5
    lw_ref[0] = -jnp.exp(w_log)

    if has_vres:
        gate = _sigmoid(vec(_V_V0) + _dot(_dot(xv, v1_ref[...]).astype(BF16), v2_ref[...]))
        v = v + (vf_ref[0] - v) * gate
    a = _sigmoid(vec(_V_A0) + _dot(_dot(mix(4), a1_ref[...]).astype(BF16), a2_ref[...]))
    g_ref[0] = _dot(_sigmoid(_dot(mix(5), g1_ref[...])).astype(BF16), g2_ref[...])

    kk = k * vec(_V_KK)
    norm = jnp.sqrt(_group_sum(kk * kk, ones_ref[...]))
    kk_ref[0] = kk / jnp.maximum(norm, 1e-12)
    k_ref[0] = k * (1.0 + (a - 1.0) * vec(_V_KA))
    r_ref[0] = r
    v_ref[0] = v
    av_ref[0] = a


def _rwkv_pre(x, v_first, vecs, wrkv, loras, ones, tm):
    B, T, C = x.shape
    has_vres = v_first is not None
    act = pl.BlockSpec((1, tm, C), lambda b, t: (b, t, 0))
    ins = [x] + ([v_first] if has_vres else []) + [vecs, wrkv] + list(loras) + [ones]
    in_specs = [act] * (2 if has_vres else 1) + [_const_spec(a.shape) for a in ins[(2 if has_vres else 1):]]
    out = jax.ShapeDtypeStruct((B, T, C), F32)
    return pl.pallas_call(
        functools.partial(_rwkv_pre_kernel, has_vres, tm),
        grid=(B, T // tm),
        in_specs=in_specs,
        out_specs=[act] * 7,
        out_shape=[out] * 7,
        scratch_shapes=[pltpu.VMEM((8, C), F32)],
        compiler_params=pltpu.CompilerParams(
            dimension_semantics=("parallel", "arbitrary"), vmem_limit_bytes=VMEM_LIMIT),
        name="rwkv_pre",
    )(*ins)


def _chunk_cumsum(x):
    pos = lax.broadcasted_iota(jnp.int32, x.shape, 0) & (CHUNK - 1)
    s = 1
    while s < CHUNK:
        x = x + jnp.where(pos >= s, pltpu.roll(x, s, axis=0), 0.0)
        s *= 2
    return x


def _each(f, *lists):
    return [f(*xs) for xs in zip(*lists)]


_WKV_PRE_STAGES = 8


def _wkv_chunks_pre(at, rt, kt, bt, v, p_last):
    L = CHUNK
    bf = lambda t: t.astype(BF16)
    lane = lax.broadcasted_iota(jnp.int32, (L, PAIR), 1)
    head0 = lane < HEAD
    row = lax.broadcasted_iota(jnp.int32, (L, 2 * L), 0)
    col = lax.broadcasted_iota(jnp.int32, (L, 2 * L), 1)
    src = jnp.where(col >= L, col - L, col)
    strict = src < row
    incl = src <= row
    left = col < L
    eye = jnp.where(src == row, 1.0, 0.0)

    def bd_rows(x):
        return jnp.concatenate([jnp.where(head0, x, 0.0), jnp.where(head0, 0.0, x)], axis=0)

    def bd_cols(x):
        return jnp.concatenate([jnp.where(left, x, 0.0), jnp.where(left, 0.0, x)], axis=0)

    def bd_rows2(x):
        return jnp.concatenate([bd_rows(x[:, :PAIR]), bd_rows(x[:, PAIR:])], axis=1)

    sc = _each(lambda a, r, b, k: _dot_nt(bf(jnp.concatenate([a, r], axis=0)),
                                          bf(jnp.concatenate([bd_rows(b), bd_rows(k)], axis=0))),
               at, rt, bt, kt)
    yield
    a_ab = [jnp.where(strict, s[:L, :2 * L], 0.0) for s in sc]
    a_ak = [jnp.where(strict, s[:L, 2 * L:], 0.0) for s in sc]
    bdv = _each(bd_rows, v)
    akv = _each(lambda a, b: _dot(bf(a), bf(b)), a_ak, bdv)
    p = _each(lambda a: _dot(bf(a), bf(bd_cols(a))), a_ab)
    t = [eye + a for a in a_ab]
    yield
    s = 2
    while 2 * s < L:
        o = _each(lambda pp, tt: _dot(bf(pp), bf(jnp.concatenate([bd_cols(tt), bd_cols(pp)], axis=1))), p, t)
        t = _each(lambda tt, oo: tt + oo[:, :2 * L], t, o)
        p = [oo[:, 2 * L:] for oo in o]
        s *= 2
        yield
    t = _each(lambda pp, tt: tt + _dot(bf(pp), bf(bd_cols(tt))), p, t)
    yield
    z = _each(lambda tt, a, u: _dot(bf(tt), bf(bd_rows2(jnp.concatenate([a, u], axis=1)))), t, at, akv)
    yield
    zeros2 = jnp.zeros((2 * L, PAIR), F32)
    o2 = _each(lambda s_, zz, bv: _dot(
        bf(jnp.concatenate([jnp.where(incl, s_[L:, :2 * L], 0.0), jnp.where(incl, s_[L:, 2 * L:], 0.0)], axis=1)),
        bf(jnp.concatenate([bd_rows2(zz), jnp.concatenate([zeros2, bv], axis=1)], axis=0))), sc, z, bdv)
    rh = _each(lambda r, oo: r + oo[:, :PAIR], rt, o2)
    yh = [oo[:, PAIR:] for oo in o2]
    yield
    zeros1 = jnp.zeros((L, PAIR), F32)
    o3 = _each(lambda b, k, pl_, zz, vv: _dot(
        bf(jnp.concatenate([b * pl_, k * pl_], axis=0).T),
        bf(jnp.concatenate([zz, jnp.concatenate([zeros1, vv], axis=1)], axis=0))), bt, kt, p_last, z, v)
    r128 = lax.broadcasted_iota(jnp.int32, (PAIR, PAIR), 0)
    c128 = lax.broadcasted_iota(jnp.int32, (PAIR, PAIR), 1)
    same_head = (r128 < HEAD) == (c128 < HEAD)
    m = _each(lambda oo, pl_: jnp.where(same_head, oo[:, :PAIR], 0.0) + jnp.where(r128 == c128, pl_, 0.0), o3, p_last)
    g = [jnp.where(same_head, oo[:, PAIR:], 0.0) for oo in o3]
    return rh, yh, m, g


def _wkv_chain(n_chunks, first_tile, mrh_ref, g_ref, yh_ref, h_ref, ys):
    L = CHUNK
    h = jnp.where(first_tile, 0.0, h_ref[...])
    for ci in range(n_chunks):
        o = _dot(mrh_ref[ci], h.astype(BF16))
        h = o[:PAIR] + g_ref[ci]
        ys.append(o[PAIR:] + yh_ref[ci * L:(ci + 1) * L, :])
        yield
    h_ref[...] = h


def _wkv_kernel(n_chunks, tiles_per_seq,
                r_ref, lw_ref, k_ref, v_ref, av_ref, kk_ref, pvec_in_ref, pvec_out_ref, ones_ref,
                y_ref, h_ref, mrh_ref, g_ref, yh_ref, bv_ref):
    L = CHUNK
    s = pl.program_id(0)

    @pl.when(s == 0)
    def _():
        h_ref[...] = jnp.zeros_like(h_ref)
        mrh_ref[...] = jnp.zeros_like(mrh_ref)
        g_ref[...] = jnp.zeros_like(g_ref)
        yh_ref[...] = jnp.zeros_like(yh_ref)
        bv_ref[...] = jnp.zeros_like(bv_ref)

    cur = s % 2
    prev = 1 - cur
    ones = ones_ref[...]

    r, lw, k, v = r_ref[0], lw_ref[0], k_ref[0], v_ref[0]
    kk = kk_ref[0]
    c = _chunk_cumsum(lw)
    ec = jnp.exp(c)
    enc = jnp.exp(-c)
    rt = r * ec
    at = -kk * jnp.exp(c - lw)
    kt = k * enc
    bt = kk * av_ref[0] * enc

    chunks = lambda x: [x[ci * L:(ci + 1) * L] for ci in range(n_chunks)]
    p_last = [ec[(ci + 1) * L - 1:(ci + 1) * L, :] for ci in range(n_chunks)]

    ys = []
    first_tile = (s - 1) % tiles_per_seq == 0
    chain = _wkv_chain(n_chunks, first_tile, mrh_ref.at[prev], g_ref.at[prev], yh_ref.at[prev], h_ref, ys)
    pre = _wkv_chunks_pre(chunks(at), chunks(rt), chunks(kt), chunks(bt), chunks(v), p_last)
    chain_steps_per_stage = -(-n_chunks // _WKV_PRE_STAGES)
    while True:
        try:
            next(pre)
        except StopIteration as stop:
            rh, yh, m, g = stop.value
            break
        for _ in range(chain_steps_per_stage):
            next(chain, None)
    for _ in chain:
        pass

    y = jnp.concatenate(ys, axis=0)
    inv_n = 1.0 / HEAD
    d = y - _group_sum(y, ones) * inv_n
    var = _group_sum(d * d, ones) * inv_n
    y_ref[0] = d * lax.rsqrt(var + GN_EPS) * pvec_out_ref[1:2, :] + pvec_out_ref[2:3, :] + bv_ref[prev]

    for ci in range(n_chunks):
        mrh_ref[cur, ci] = jnp.concatenate([m[ci], rh[ci]], axis=0).astype(BF16)
        g_ref[cur, ci] = g[ci]
    yh_ref[cur] = jnp.concatenate(yh, axis=0)
    bv_ref[cur] = _group_sum(r * k * pvec_in_ref[0:1, :], ones) * v


def _wkv(r, lw, k, v, av, kk, pvec, ones, tt):
    B, T, C = r.shape
    n_pairs = C // PAIR
    n_tiles = T // tt
    n_chunks = tt // CHUNK
    steps = B * n_pairs * n_tiles

    def tile_index(i):
        return i // (n_pairs * n_tiles), i % n_tiles, (i // n_tiles) % n_pairs

    cur = lambda s: jnp.minimum(s, steps - 1)
    prev = lambda s: jnp.maximum(s - 1, 0)
    act_in = pl.BlockSpec((1, tt, PAIR), lambda s: tile_index(cur(s)))
    act_out = pl.BlockSpec((1, tt, PAIR), lambda s: tile_index(prev(s)))
    return pl.pallas_call(
        functools.partial(_wkv_kernel, n_chunks, n_tiles),
        grid=(steps + 1,),
        in_specs=[act_in] * 6 + [pl.BlockSpec((8, PAIR), lambda s: (0, tile_index(cur(s))[2])),
                                 pl.BlockSpec((8, PAIR), lambda s: (0, tile_index(prev(s))[2])),
                                 pl.BlockSpec(ones.shape, lambda s: (0, 0))],
        out_specs=act_out,
        out_shape=jax.ShapeDtypeStruct((B, T, C), F32),
        scratch_shapes=[pltpu.VMEM((PAIR, PAIR), F32),
                        pltpu.VMEM((2, n_chunks, PAIR + CHUNK, PAIR), BF16),
                        pltpu.VMEM((2, n_chunks, PAIR, PAIR), F32),
                        pltpu.VMEM((2, tt, PAIR), F32),
                        pltpu.VMEM((2, tt, PAIR), F32)],
        compiler_params=pltpu.CompilerParams(
            dimension_semantics=("arbitrary",), vmem_limit_bytes=VMEM_LIMIT),
        name="wkv",
    )(r, lw, k, v, av, kk, pvec, pvec, ones)


def _post_mlp_kernel(has_gate, *refs):
    if has_gate:
        x_ref, y_ref, g_ref, wo_ref, ln_ref, w1_ref, w2_ref, o_ref = refs
        y = (y_ref[...] * g_ref[...]).astype(BF16)
    else:
        x_ref, y_ref, wo_ref, ln_ref, w1_ref, w2_ref, o_ref = refs
        y = y_ref[...].astype(BF16)
    xn = x_ref[...] + _dot(y, wo_ref[...])
    hid = _dot(_rms(xn, ln_ref[...]).astype(BF16), w1_ref[...])
    hid = jnp.square(jnp.maximum(hid, 0.0)).astype(BF16)
    o_ref[...] = xn + _dot(hid, w2_ref[...])


def _post_mlp(x, y, g, wo, ln, w1, w2, tm):
    M, C = x.shape
    act = pl.BlockSpec((tm, C), lambda i: (i, 0))
    has_gate = g is not None
    ins = [x, y] + ([g] if has_gate else []) + [wo, ln, w1, w2]
    n_act = 3 if has_gate else 2
    return pl.pallas_call(
        functools.partial(_post_mlp_kernel, has_gate),
        grid=(M // tm,),
        in_specs=[act] * n_act + [_const_spec(a.shape) for a in ins[n_act:]],
        out_specs=act,
        out_shape=jax.ShapeDtypeStruct((M, C), F32),
        compiler_params=pltpu.CompilerParams(
            dimension_semantics=("parallel",), vmem_limit_bytes=VMEM_LIMIT),
        name="post_mlp",
    )(*ins)


def _kv_proj_kernel(x_ref, ln_ref, w_ref, kg_ref, ones_ref, k_ref, v_ref):
    kv = _dot(_rms(x_ref[...], ln_ref[...]).astype(BF16), w_ref[...])
    half = kv.shape[1] // 2
    k = kv[:, :half]
    ms = _group_sum(k * k, ones_ref[...]) * (1.0 / HEAD)
    k_ref[...] = (k * lax.rsqrt(ms + RMS_EPS) * kg_ref[...]).astype(BF16)
    v_ref[...] = kv[:, half:].astype(BF16)


def _kv_proj(x, ln, w, kg, ones, tm):
    M, C = x.shape
    n = w.shape[1] // 2
    out = jax.ShapeDtypeStruct((M, n), BF16)
    return pl.pallas_call(
        _kv_proj_kernel,
        grid=(M // tm,),
        in_specs=[pl.BlockSpec((tm, C), lambda i: (i, 0))] + [_const_spec(a.shape) for a in (ln, w, kg, ones)],
        out_specs=[pl.BlockSpec((tm, n), lambda i: (i, 0))] * 2,
        out_shape=[out, out],
        compiler_params=pltpu.CompilerParams(
            dimension_semantics=("parallel",), vmem_limit_bytes=VMEM_LIMIT),
        name="kv_proj",
    )(x, ln, w, kg, ones)


def _q_proj_kernel(x_ref, ln_ref, w_ref, qg_ref, ones_ref, q_ref):
    q = _dot(_rms(x_ref[...], ln_ref[...]).astype(BF16), w_ref[...])
    ms = _group_sum(q * q, ones_ref[...]) * (1.0 / HEAD)
    q_ref[...] = (q * lax.rsqrt(ms + RMS_EPS) * qg_ref[...] * (HEAD ** -0.5)).astype(BF16)


def _q_proj(x, ln, w, qg, ones, tm):
    M, C = x.shape
    return pl.pallas_call(
        _q_proj_kernel,
        grid=(M // tm,),
        in_specs=[pl.BlockSpec((tm, C), lambda i: (i, 0))] + [_const_spec(a.shape) for a in (ln, w, qg, ones)],
        out_specs=pl.BlockSpec((tm, w.shape[1]), lambda i: (i, 0)),
        out_shape=jax.ShapeDtypeStruct((M, w.shape[1]), BF16),
        compiler_params=pltpu.CompilerParams(
            dimension_semantics=("parallel",), vmem_limit_bytes=VMEM_LIMIT),
        name="q_proj",
    )(x, ln, w, qg, ones)


def _swa_kernel(n_q_heads, slopes_ref, sinks_ref, q_ref, kp_ref, kc_ref, vp_ref, vc_ref, o_ref):
    W = WINDOW
    group = n_q_heads // N_KV
    pairs = group // 2
    blk = pl.program_id(1)
    qi = lax.broadcasted_iota(jnp.int32, (W, 2 * W), 0)
    kj = lax.broadcasted_iota(jnp.int32, (W, 2 * W), 1)
    dist = qi + W - kj
    first_key = jnp.where(blk > 0, 0, W)
    valid = (dist >= 0) & (dist < WINDOW) & (kj >= first_key)
    distf = dist.astype(F32)
    lane = lax.broadcasted_iota(jnp.int32, (W, PAIR), 1)
    head0 = lane < HEAD
    lane2 = lax.broadcasted_iota(jnp.int32, (2 * W, PAIR), 1)
    head0_kv = lane2 < HEAD
    zero = jnp.zeros((), BF16)

    for h in range(N_KV):
        ks = slice(h * PAIR, (h + 1) * PAIR)
        k2 = jnp.concatenate([kp_ref[0, :, ks], kc_ref[0, :, ks]], axis=0)
        v2 = jnp.concatenate([vp_ref[0, :, ks], vc_ref[0, :, ks]], axis=0)
        q_tiles = [q_ref[0, :, (h * pairs + j) * PAIR:(h * pairs + j + 1) * PAIR] for j in range(pairs)]
        lhs = jnp.concatenate([jnp.where(head0, t, zero) for t in q_tiles]
                              + [jnp.where(head0, zero, t) for t in q_tiles], axis=0)
        s_all = _dot_nt(lhs, k2)
        probs, denoms = [], []
        for i in range(group):
            head = h * group + 2 * (i % pairs) + i // pairs
            s = s_all[i * W:(i + 1) * W, :] - slopes_ref[head] * distf
            s = jnp.where(valid, s, NEG_BIG)
            sink = sinks_ref[head]
            mx = jnp.maximum(jnp.max(s, axis=-1, keepdims=True), sink)
            p = jnp.exp(s - mx)
            denoms.append(jnp.sum(p, axis=-1, keepdims=True) + jnp.exp(sink - mx))
            probs.append(p.astype(BF16))
        p_first = jnp.concatenate(probs[:pairs], axis=0)
        p_second = jnp.concatenate(probs[pairs:], axis=0)
        v_stack = jnp.concatenate([jnp.where(head0_kv, v2, zero), jnp.where(head0_kv, zero, v2)], axis=0)
        o = _dot(jnp.concatenate([p_first, p_second], axis=1), v_stack)
        for j in range(pairs):
            den = jnp.where(head0, denoms[j], denoms[pairs + j])
            col = (h * pairs + j) * PAIR
            o_ref[0, :, col:col + PAIR] = (o[j * W:(j + 1) * W, :] / den).astype(o_ref.dtype)


def _swa(q, k2, v2, slopes, sinks):
    B, T, C = q.shape
    W = WINDOW
    kvw = k2.shape[-1]
    cur = pl.BlockSpec((1, W, kvw), lambda b, n, *_: (b, n, 0))
    prev = pl.BlockSpec((1, W, kvw), lambda b, n, *_: (b, jnp.maximum(n - 1, 0), 0))
    qspec = pl.BlockSpec((1, W, C), lambda b, n, *_: (b, n, 0))
    return pl.pallas_call(
        functools.partial(_swa_kernel, C // HEAD),
        grid_spec=pltpu.PrefetchScalarGridSpec(
            num_scalar_prefetch=2,
            grid=(B, T // W),
            in_specs=[qspec, prev, cur, prev, cur],
            out_specs=qspec,
        ),
        out_shape=jax.ShapeDtypeStruct((B, T, C), BF16),
        compiler_params=pltpu.CompilerParams(
            dimension_semantics=("parallel", "arbitrary"), vmem_limit_bytes=VMEM_LIMIT),
        name="swa",
    )(slopes, sinks, q, k2, k2, v2, v2)


def _pad_cols(w, n):
    return jnp.pad(w, ((0, 0), (0, n - w.shape[1])))


def _pad_rows(w, n):
    return jnp.pad(w, ((0, n - w.shape[0]), (0, 0)))


def _round_up(n, m):
    return (n + m - 1) // m * m


def _lora(w_in, w_out):
    n = _round_up(w_in.shape[1], 128)
    return _pad_cols(w_in, n).astype(BF16), _pad_rows(w_out, n).astype(BF16)


def _dup_heads(w):
    c, n = w.shape
    w = w.reshape(c, n // HEAD, 1, HEAD)
    return jnp.broadcast_to(w, (c, n // HEAD, 2, HEAD)).reshape(c, 2 * n)


def _block_ones(n):
    i = jnp.arange(n) // HEAD
    return (i[:, None] == i[None, :]).astype(BF16)


def kernel(x, ln_mix, ln_mlp, mlp_w1, mlp_w2, a_mu, a_w_rkv, a_w0, a_w1, a_w2, a_a0, a_a1, a_a2, a_g1, a_g2, a_k_k, a_k_a, a_r_k, a_gn_g, a_gn_b, a_wo, a_v0, a_v1, a_v2, kv_norm, w_kv, k_gain, b_wq, b_q_gain, b_sinks, b_wo):
    B, T, C = x.shape
    M = B * T
    n_a = a_mu.shape[0]
    n_b = b_wq.shape[0]
    n_heads = C // HEAD
    tm = 256
    tt = 1024
    ones256 = _block_ones(256)
    ones128 = _block_ones(PAIR)
    row = lambda v: v.reshape(1, -1).astype(F32)

    v_first = None
    k2 = v2 = None
    slopes = jnp.exp2(-8.0 * jnp.arange(1, n_heads + 1, dtype=F32) / n_heads)
    for i in range(n_a + n_b):
        if i < n_a:
            j = i
            zeros = jnp.zeros((1, C), F32)
            vecs = jnp.concatenate(
                [a_mu[j], row(ln_mix[i]), row(a_w0[j]), row(a_a0[j]),
                 row(a_v0[j - 1]) if j > 0 else zeros, row(a_k_k[j]), row(a_k_a[j])]
                + [zeros] * 4, axis=0)
            loras = list(_lora(a_w1[j], a_w2[j]) + _lora(a_a1[j], a_a2[j]) + _lora(a_g1[j], a_g2[j]))
            if j > 0:
                loras += list(_lora(a_v1[j - 1], a_v2[j - 1]))
            r, lw, k, v, av, kk, g = _rwkv_pre(
                x, v_first if j > 0 else None, vecs, a_w_rkv[j].astype(BF16), loras, ones256, tm)
            if j == 0:
                v_first = v
            pvec = jnp.concatenate([row(a_r_k[j]), row(a_gn_g[j]), row(a_gn_b[j])]
                                   + [jnp.zeros((1, C), F32)] * 5, axis=0)
            y = _wkv(r, lw, k, v, av, kk, pvec, ones128, tt)
            x = _post_mlp(x.reshape(M, C), y.reshape(M, C), g.reshape(M, C), a_wo[j].astype(BF16),
                          row(ln_mlp[i]), mlp_w1[i].astype(BF16), mlp_w2[i].astype(BF16), tm).reshape(B, T, C)
            if i == n_a - 1:
                kg = jnp.tile(row(k_gain), (1, 2 * N_KV))
                k2, v2 = _kv_proj(x.reshape(M, C), row(kv_norm), _dup_heads(w_kv).astype(BF16), kg, ones256, tm)
                k2 = k2.reshape(B, T, -1)
                v2 = v2.reshape(B, T, -1)
        else:
            j = i - n_a
            qg = jnp.tile(row(b_q_gain[j]), (1, n_heads))
            q = _q_proj(x.reshape(M, C), row(ln_mix[i]), b_wq[j].astype(BF16), qg, ones256, tm)
            o = _swa(q.reshape(B, T, C), k2, v2, slopes, b_sinks[j].astype(F32))
            x = _post_mlp(x.reshape(M, C), o.reshape(M, C), None, b_wo[j].astype(BF16),
                          row(ln_mlp[i]), mlp_w1[i].astype(BF16), mlp_w2[i].astype(BF16), tm).reshape(B, T, C)
    return x
```

```python
import functools
import math

import jax
import jax.numpy as jnp
from jax import lax
from jax.experimental import pallas as pl
from jax.experimental.pallas import tpu as pltpu

F32 = jnp.float32
BF16 = jnp.bfloat16

HEAD = 64
PAIR = 2 * HEAD
N_KV = 2
WINDOW = 128
CHUNK = 64
GN_EPS = 64e-5
RMS_EPS = 1e-6
NEG_BIG = -1e30
LOG2E = math.log2(math.e)
VMEM_LIMIT = 56 * 1024 * 1024

NT_DIMS = (((1,), (1,)), ((), ()))


def _dot(a, b):
    return jnp.dot(a, b, preferred_element_type=F32)


def _dot_nt(a, b):
    return lax.dot_general(a, b, NT_DIMS, preferred_element_type=F32)


def _rms(x, g):
    return x * lax.rsqrt(jnp.mean(x * x, axis=-1, keepdims=True) + RMS_EPS) * g


def _group_sum(x, ones):
    w = ones.shape[0]
    parts = [_dot(x[:, j:j + w].astype(BF16), ones) for j in range(0, x.shape[1], w)]
    return parts[0] if len(parts) == 1 else jnp.concatenate(parts, axis=1)


def _sigmoid(x):
    return 0.5 * jnp.tanh(0.5 * x) + 0.5


def _const_spec(shape):
    nd = len(shape)
    return pl.BlockSpec(shape, lambda *_: (0,) * nd, pipeline_mode=pl.Buffered(1))


_V_MU, _V_LN, _V_W0, _V_A0, _V_V0, _V_KK, _V_KA = 0, 6, 7, 8, 9, 10, 11


def _rwkv_pre_kernel(has_vres, tm, *refs):
    if has_vres:
        (x_ref, vf_ref, vec_ref, wrkv_ref, w1_ref, w2_ref, a1_ref, a2_ref, g1_ref, g2_ref,
         v1_ref, v2_ref, ones_ref,
         r_ref, lw_ref, k_ref, v_ref, av_ref, kk_ref, g_ref, carry_ref) = refs
    else:
        (x_ref, vec_ref, wrkv_ref, w1_ref, w2_ref, a1_ref, a2_ref, g1_ref, g2_ref, ones_ref,
         r_ref, lw_ref, k_ref, v_ref, av_ref, kk_ref, g_ref, carry_ref) = refs

    @pl.when(pl.program_id(1) == 0)
    def _():
        carry_ref[...] = jnp.zeros_like(carry_ref)

    def vec(i):
        return vec_ref[i:i + 1, :]

    h = _rms(x_ref[0], vec(_V_LN))
    row = lax.broadcasted_iota(jnp.int32, h.shape, 0)
    prev = jnp.where(row == 0, carry_ref[0:1, :], pltpu.roll(h, 1, axis=0))
    carry_ref[0:1, :] = h[tm - 1:tm, :]
    xx = prev - h

    def mix(i):
        return (h + xx * vec(_V_MU + i)).astype(BF16)

    r = _dot(mix(0), wrkv_ref[0])
    k = _dot(mix(1), wrkv_ref[1])
    xv = mix(2)
    v = _dot(xv, wrkv_ref[2])

    wl = vec(_V_W0) + _dot(jnp.tanh(_dot(mix(3), w1_ref[...])).astype(BF16), w2_ref[...])
    lw_ref[0] = -math.exp(-0.5) * _sigmoid(wl)

    if has_vres:
        gate = _sigmoid(vec(_V_V0) + _dot(_dot(xv, v1_ref[...]).astype(BF16), v2_ref[...]))
        v = v + (vf_ref[0] - v) * gate
    a = _sigmoid(vec(_V_A0) + _dot(_dot(mix(4), a1_ref[...]).astype(BF16), a2_ref[...]))
    g_ref[0] = _dot(_sigmoid(_dot(mix(5), g1_ref[...])).astype(BF16), g2_ref[...])

    kk = k * vec(_V_KK)
    kk_ref[0] = kk * jnp.minimum(lax.rsqrt(_group_sum(kk * kk, ones_ref[...])), 1e12)
    k_ref[0] = k * (1.0 + (a - 1.0) * vec(_V_KA))
    r_ref[0] = r
    v_ref[0] = v
    av_ref[0] = a


def _rwkv_pre(x, v_first, vecs, wrkv, loras, ones, tm):
    B, T, C = x.shape
    has_vres = v_first is not None
    act = pl.BlockSpec((1, tm, C), lambda b, t: (b, t, 0))
    ins = [x] + ([v_first] if has_vres else []) + [vecs, wrkv] + list(loras) + [ones]
    in_specs = [act] * (2 if has_vres else 1) + [_const_spec(a.shape) for a in ins[(2 if has_vres else 1):]]
    out = jax.ShapeDtypeStruct((B, T, C), F32)
    return pl.pallas_call(
        functools.partial(_rwkv_pre_kernel, has_vres, tm),
        grid=(B, T // tm),
        in_specs=in_specs,
        out_specs=[act] * 7,
        out_shape=[out] * 7,
        scratch_shapes=[pltpu.VMEM((8, C), F32)],
        compiler_params=pltpu.CompilerParams(
            dimension_semantics=("parallel", "arbitrary"), vmem_limit_bytes=VMEM_LIMIT),
        name="rwkv_pre",
    )(*ins)


def _chunk_cumsum(x):
    pos = lax.broadcasted_iota(jnp.int32, x.shape, 0) & (CHUNK - 1)
    s = 1
    while s < CHUNK:
        x = x + jnp.where(pos >= s, pltpu.roll(x, s, axis=0), 0.0)
        s *= 2
    return x


def _each(f, *lists):
    return [f(*xs) for xs in zip(*lists)]


_WKV_PRE_STAGES = 8


def _wkv_chunks_pre(at, rt, kt, bt, v, p_last):
    L = CHUNK
    bf = lambda t: t.astype(BF16)
    lane = lax.broadcasted_iota(jnp.int32, (L, PAIR), 1)
    head0 = lane < HEAD
    row = lax.broadcasted_iota(jnp.int32, (L, 2 * L), 0)
    col = lax.broadcasted_iota(jnp.int32, (L, 2 * L), 1)
    src = jnp.where(col >= L, col - L, col)
    strict = src < row
    incl = src <= row
    left = col < L
    eye = jnp.where(src == row, 1.0, 0.0)

    def bd_rows(x):
        return jnp.concatenate([jnp.where(head0, x, 0.0), jnp.where(head0, 0.0, x)], axis=0)

    def bd_cols(x):
        return jnp.concatenate([jnp.where(left, x, 0.0), jnp.where(left, 0.0, x)], axis=0)

    def bd_rows2(x):
        return jnp.concatenate([bd_rows(x[:, :PAIR]), bd_rows(x[:, PAIR:])], axis=1)

    sc = _each(lambda a, r, b, k: _dot_nt(bf(jnp.concatenate([a, r], axis=0)),
                                          bf(jnp.concatenate([bd_rows(b), bd_rows(k)], axis=0))),
               at, rt, bt, kt)
    yield
    a_ab = [jnp.where(strict, s[:L, :2 * L], 0.0) for s in sc]
    a_ak = [jnp.where(strict, s[:L, 2 * L:], 0.0) for s in sc]
    bdv = _each(bd_rows, v)
    akv = _each(lambda a, b: _dot(bf(a), bf(b)), a_ak, bdv)
    p = _each(lambda a: _dot(bf(a), bf(bd_cols(a))), a_ab)
    t = [eye + a for a in a_ab]
    yield
    s = 2
    while 2 * s < L:
        o = _each(lambda pp, tt: _dot(bf(pp), bf(jnp.concatenate([bd_cols(tt), bd_cols(pp)], axis=1))), p, t)
        t = _each(lambda tt, oo: tt + oo[:, :2 * L], t, o)
        p = [oo[:, 2 * L:] for oo in o]
        s *= 2
        yield
    t = _each(lambda pp, tt: tt + _dot(bf(pp), bf(bd_cols(tt))), p, t)
    yield
    z = _each(lambda tt, a, u: _dot(bf(tt), bf(bd_rows2(jnp.concatenate([a, u], axis=1)))), t, at, akv)
    yield
    zeros2 = jnp.zeros((2 * L, PAIR), F32)
    o2 = _each(lambda s_, zz, bv: _dot(
        bf(jnp.concatenate([jnp.where(incl, s_[L:, :2 * L], 0.0), jnp.where(incl, s_[L:, 2 * L:], 0.0)], axis=1)),
        bf(jnp.concatenate([bd_rows2(zz), jnp.concatenate([zeros2, bv], axis=1)], axis=0))), sc, z, bdv)
    rh = _each(lambda r, oo: r + oo[:, :PAIR], rt, o2)
    yh = [oo[:, PAIR:] for oo in o2]
    yield
    zeros1 = jnp.zeros((L, PAIR), F32)
    o3 = _each(lambda b, k, pl_, zz, vv: _dot(
        bf(jnp.concatenate([b * pl_, k * pl_], axis=0).T),
        bf(jnp.concatenate([zz, jnp.concatenate([zeros1, vv], axis=1)], axis=0))), bt, kt, p_last, z, v)
    r128 = lax.broadcasted_iota(jnp.int32, (PAIR, PAIR), 0)
    c128 = lax.broadcasted_iota(jnp.int32, (PAIR, PAIR), 1)
    same_head = (r128 < HEAD) == (c128 < HEAD)
    m = _each(lambda oo, pl_: jnp.where(same_head, oo[:, :PAIR], 0.0) + jnp.where(r128 == c128, pl_, 0.0), o3, p_last)
    g = [jnp.where(same_head, oo[:, PAIR:], 0.0) for oo in o3]
    return rh, yh, m, g


def _wkv_chain(n_chunks, first_tile, mrh_ref, g_ref, yh_ref, h_ref, ys):
    L = CHUNK
    h = jnp.where(first_tile, 0.0, h_ref[...])
    for ci in range(n_chunks):
        o = _dot(mrh_ref[ci], h.astype(BF16))
        h = o[:PAIR] + g_ref[ci]
        ys.append(o[PAIR:] + yh_ref[ci * L:(ci + 1) * L, :])
        yield
    h_ref[...] = h


def _wkv_kernel(n_chunks, tiles_per_seq,
                r_ref, lw_ref, k_ref, v_ref, av_ref, kk_ref, pvec_in_ref, pvec_out_ref, ones_ref,
                y_ref, h_ref, mrh_ref, g_ref, yh_ref, bv_ref):
    L = CHUNK
    s = pl.program_id(0)

    @pl.when(s == 0)
    def _():
        h_ref[...] = jnp.zeros_like(h_ref)
        mrh_ref[...] = jnp.zeros_like(mrh_ref)
        g_ref[...] = jnp.zeros_like(g_ref)
        yh_ref[...] = jnp.zeros_like(yh_ref)
        bv_ref[...] = jnp.zeros_like(bv_ref)

    cur = s % 2
    prev = 1 - cur
    ones = ones_ref[...]

    r, lw, k, v = r_ref[0], lw_ref[0], k_ref[0], v_ref[0]
    kk = kk_ref[0]
    c = _chunk_cumsum(lw)
    ec = jnp.exp(c)
    enc = jnp.exp(-c)
    rt = r * ec
    at = -kk * jnp.exp(c - lw)
    kt = k * enc
    bt = kk * av_ref[0] * enc

    chunks = lambda x: [x[ci * L:(ci + 1) * L] for ci in range(n_chunks)]
    p_last = [ec[(ci + 1) * L - 1:(ci + 1) * L, :] for ci in range(n_chunks)]

    ys = []
    first_tile = (s - 1) % tiles_per_seq == 0
    chain = _wkv_chain(n_chunks, first_tile, mrh_ref.at[prev], g_ref.at[prev], yh_ref.at[prev], h_ref, ys)
    pre = _wkv_chunks_pre(chunks(at), chunks(rt), chunks(kt), chunks(bt), chunks(v), p_last)
    chain_steps_per_stage = -(-n_chunks // _WKV_PRE_STAGES)
    while True:
        try:
            next(pre)
        except StopIteration as stop:
            rh, yh, m, g = stop.value
            break
        for _ in range(chain_steps_per_stage):
            next(chain, None)
    for _ in chain:
        pass

    y = jnp.concatenate(ys, axis=0)
    inv_n = 1.0 / HEAD
    d = y - _group_sum(y, ones) * inv_n
    var = _group_sum(d * d, ones) * inv_n
    y_ref[0] = d * lax.rsqrt(var + GN_EPS) * pvec_out_ref[1:2, :] + pvec_out_ref[2:3, :] + bv_ref[prev]

    for ci in range(n_chunks):
        mrh_ref[cur, ci] = jnp.concatenate([m[ci], rh[ci]], axis=0).astype(BF16)
        g_ref[cur, ci] = g[ci]
    yh_ref[cur] = jnp.concatenate(yh, axis=0)
    bv_ref[cur] = _group_sum(r * k * pvec_in_ref[0:1, :], ones) * v


def _wkv(r, lw, k, v, av, kk, pvec, ones, tt):
    B, T, C = r.shape
    n_pairs = C // PAIR
    n_tiles = T // tt
    n_chunks = tt // CHUNK
    steps = B * n_pairs * n_tiles

    def tile_index(i):
        return i // (n_pairs * n_tiles), i % n_tiles, (i // n_tiles) % n_pairs

    cur = lambda s: jnp.minimum(s, steps - 1)
    prev = lambda s: jnp.maximum(s - 1, 0)
    act_in = pl.BlockSpec((1, tt, PAIR), lambda s: tile_index(cur(s)))
    act_out = pl.BlockSpec((1, tt, PAIR), lambda s: tile_index(prev(s)))
    return pl.pallas_call(
        functools.partial(_wkv_kernel, n_chunks, n_tiles),
        grid=(steps + 1,),
        in_specs=[act_in] * 6 + [pl.BlockSpec((8, PAIR), lambda s: (0, tile_index(cur(s))[2])),
                                 pl.BlockSpec((8, PAIR), lambda s: (0, tile_index(prev(s))[2])),
                                 pl.BlockSpec(ones.shape, lambda s: (0, 0))],
        out_specs=act_out,
        out_shape=jax.ShapeDtypeStruct((B, T, C), F32),
        scratch_shapes=[pltpu.VMEM((PAIR, PAIR), F32),
                        pltpu.VMEM((2, n_chunks, PAIR + CHUNK, PAIR), BF16),
                        pltpu.VMEM((2, n_chunks, PAIR, PAIR), F32),
                        pltpu.VMEM((2, tt, PAIR), F32),
                        pltpu.VMEM((2, tt, PAIR), F32)],
        compiler_params=pltpu.CompilerParams(
            dimension_semantics=("arbitrary",), vmem_limit_bytes=VMEM_LIMIT),
        name="wkv",
    )(r, lw, k, v, av, kk, pvec, pvec, ones)


def _post_mlp_kernel(has_gate, has_kv, has_q, *refs):
    refs = list(refs)
    x_ref, y_ref = refs[:2]
    del refs[:2]
    g_ref = refs.pop(0) if has_gate else None
    wo_ref, ln_ref, w1_ref, w2_ref = refs[:4]
    del refs[:4]
    if has_kv or has_q:
        ones_ref = refs.pop(0)
    if has_kv:
        kv_ln_ref, wkv_ref, kg_ref = refs[:3]
        del refs[:3]
    if has_q:
        q_ln_ref, wq_ref, qg_ref = refs[:3]
        del refs[:3]
    o_ref = refs.pop(0)
    if has_kv:
        k_ref, v_ref = refs[:2]
        del refs[:2]
    if has_q:
        q_ref = refs.pop(0)

    y = y_ref[...]
    if has_gate:
        y = y * g_ref[...]
    xn = x_ref[...] + _dot(y.astype(BF16), wo_ref[...])
    hid = _dot(_rms(xn, ln_ref[...]).astype(BF16), w1_ref[...])
    hid = jnp.square(jnp.maximum(hid, 0.0)).astype(BF16)
    out = xn + _dot(hid, w2_ref[...])
    o_ref[...] = out

    if has_kv:
        kv = _dot(_rms(out, kv_ln_ref[...]).astype(BF16), wkv_ref[...])
        half = kv.shape[1] // 2
        k = kv[:, :half]
        ms = _group_sum(k * k, ones_ref[...]) * (1.0 / HEAD)
        k_ref[...] = (k * lax.rsqrt(ms + RMS_EPS) * kg_ref[...]).astype(BF16)
        v_ref[...] = kv[:, half:].astype(BF16)
    if has_q:
        q = _dot(_rms(out, q_ln_ref[...]).astype(BF16), wq_ref[...])
        ms = _group_sum(q * q, ones_ref[...]) * (1.0 / HEAD)
        q_ref[...] = (q * lax.rsqrt(ms + RMS_EPS) * qg_ref[...] * (HEAD ** -0.5 * LOG2E)).astype(BF16)


def _post_mlp(x, y, g, wo, ln, w1, w2, tm, ones=None, kv=None, q=None):
    M, C = x.shape
    act = pl.BlockSpec((tm, C), lambda i: (i, 0))
    has_gate, has_kv, has_q = g is not None, kv is not None, q is not None
    acts = [x, y] + ([g] if has_gate else [])
    consts = [wo, ln, w1, w2] + ([ones] if has_kv or has_q else []) + list(kv or ()) + list(q or ())
    out_shape = [jax.ShapeDtypeStruct((M, C), F32)]
    out_specs = [act]
    if has_kv:
        n = kv[1].shape[1] // 2
        out_shape += [jax.ShapeDtypeStruct((M, n), BF16)] * 2
        out_specs += [pl.BlockSpec((tm, n), lambda i: (i, 0))] * 2
    if has_q:
        n = q[1].shape[1]
        out_shape.append(jax.ShapeDtypeStruct((M, n), BF16))
        out_specs.append(pl.BlockSpec((tm, n), lambda i: (i, 0)))
    return pl.pallas_call(
        functools.partial(_post_mlp_kernel, has_gate, has_kv, has_q),
        grid=(M // tm,),
        in_specs=[act] * len(acts) + [_const_spec(a.shape) for a in consts],
        out_specs=out_specs,
        out_shape=out_shape,
        compiler_params=pltpu.CompilerParams(
            dimension_semantics=("parallel",), vmem_limit_bytes=VMEM_LIMIT),
        name="post_mlp",
    )(*acts, *consts)


def _swa_kernel(n_q_heads, slopes_ref, sinks_ref, q_ref, kp_ref, kc_ref, vp_ref, vc_ref, o_ref, bias_ref):
    W = WINDOW
    group = n_q_heads // N_KV
    pairs = group // 2
    blk = pl.program_id(1)

    @pl.when((pl.program_id(0) == 0) & (blk == 0))
    def _():
        qi = lax.broadcasted_iota(jnp.int32, (W, 2 * W), 0)
        kj = lax.broadcasted_iota(jnp.int32, (W, 2 * W), 1)
        dist = qi + W - kj
        in_window = (dist >= 0) & (dist < WINDOW)
        in_window_cur = in_window & (kj >= W)
        distf = dist.astype(F32)
        for head in range(n_q_heads):
            alibi = -(slopes_ref[head] * LOG2E) * distf
            bias_ref[0, head] = jnp.where(in_window, alibi, NEG_BIG)
            bias_ref[1, head] = jnp.where(in_window_cur, alibi, NEG_BIG)

    plane = jnp.where(blk == 0, 1, 0)
    lane = lax.broadcasted_iota(jnp.int32, (W, PAIR), 1)
    head0 = lane < HEAD
    lane2 = lax.broadcasted_iota(jnp.int32, (2 * W, PAIR), 1)
    head0_kv = lane2 < HEAD
    zero = jnp.zeros((), BF16)

    for h in range(N_KV):
        ks = slice(h * PAIR, (h + 1) * PAIR)
        k2 = jnp.concatenate([kp_ref[0, :, ks], kc_ref[0, :, ks]], axis=0)
        v2 = jnp.concatenate([vp_ref[0, :, ks], vc_ref[0, :, ks]], axis=0)
        q_tiles = [q_ref[0, :, (h * pairs + j) * PAIR:(h * pairs + j + 1) * PAIR] for j in range(pairs)]
        lhs = jnp.concatenate([jnp.where(head0, t, zero) for t in q_tiles]
                              + [jnp.where(head0, zero, t) for t in q_tiles], axis=0)
        s_all = _dot_nt(lhs, k2)
        probs, denoms = [], []
        for i in range(group):
            head = h * group + 2 * (i % pairs) + i // pairs
            s = s_all[i * W:(i + 1) * W, :] + bias_ref[plane, head]
            sink = sinks_ref[head] * LOG2E
            mx = jnp.maximum(jnp.max(s, axis=-1, keepdims=True), sink)
            p = jnp.exp2(s - mx)
            denoms.append(jnp.sum(p, axis=-1, keepdims=True) + jnp.exp2(sink - mx))
            probs.append(p.astype(BF16))
        p_first = jnp.concatenate(probs[:pairs], axis=0)
        p_second = jnp.concatenate(probs[pairs:], axis=0)
        v_stack = jnp.concatenate([jnp.where(head0_kv, v2, zero), jnp.where(head0_kv, zero, v2)], axis=0)
        o = _dot(jnp.concatenate([p_first, p_second], axis=1), v_stack)
        for j in range(pairs):
            den = jnp.where(head0, denoms[j], denoms[pairs + j])
            col = (h * pairs + j) * PAIR
            o_ref[0, :, col:col + PAIR] = (o[j * W:(j + 1) * W, :] / den).astype(o_ref.dtype)


def _swa(q, k2, v2, slopes, sinks):
    B, T, C = q.shape
    W = WINDOW
    kvw = k2.shape[-1]
    cur = pl.BlockSpec((1, W, kvw), lambda b, n, *_: (b, n, 0))
    prev = pl.BlockSpec((1, W, kvw), lambda b, n, *_: (b, jnp.maximum(n - 1, 0), 0))
    qspec = pl.BlockSpec((1, W, C), lambda b, n, *_: (b, n, 0))
    return pl.pallas_call(
        functools.partial(_swa_kernel, C // HEAD),
        grid_spec=pltpu.PrefetchScalarGridSpec(
            num_scalar_prefetch=2,
            grid=(B, T // W),
            in_specs=[qspec, prev, cur, prev, cur],
            out_specs=qspec,
            scratch_shapes=[pltpu.VMEM((2, C // HEAD, W, 2 * W), F32)],
        ),
        out_shape=jax.ShapeDtypeStruct((B, T, C), BF16),
        compiler_params=pltpu.CompilerParams(
            dimension_semantics=("arbitrary", "arbitrary"), vmem_limit_bytes=VMEM_LIMIT),
        name="swa",
    )(slopes, sinks, q, k2, k2, v2, v2)


def _pad_cols(w, n):
    return jnp.pad(w, ((0, 0), (0, n - w.shape[1])))


def _pad_rows(w, n):
    return jnp.pad(w, ((0, n - w.shape[0]), (0, 0)))


def _round_up(n, m):
    return (n + m - 1) // m * m


def _lora(w_in, w_out):
    n = _round_up(w_in.shape[1], 128)
    return _pad_cols(w_in, n).astype(BF16), _pad_rows(w_out, n).astype(BF16)


def _dup_heads(w):
    c, n = w.shape
    w = w.reshape(c, n // HEAD, 1, HEAD)
    return jnp.broadcast_to(w, (c, n // HEAD, 2, HEAD)).reshape(c, 2 * n)


def _block_ones(n):
    i = jnp.arange(n) // HEAD
    return (i[:, None] == i[None, :]).astype(BF16)


def kernel(x, ln_mix, ln_mlp, mlp_w1, mlp_w2, a_mu, a_w_rkv, a_w0, a_w1, a_w2, a_a0, a_a1, a_a2, a_g1, a_g2, a_k_k, a_k_a, a_r_k, a_gn_g, a_gn_b, a_wo, a_v0, a_v1, a_v2, kv_norm, w_kv, k_gain, b_wq, b_q_gain, b_sinks, b_wo):
    B, T, C = x.shape
    M = B * T
    n_a = a_mu.shape[0]
    n_b = b_wq.shape[0]
    n_heads = C // HEAD
    tm = 256
    tt = 1024
    ones256 = _block_ones(256)
    ones128 = _block_ones(PAIR)
    row = lambda v: v.reshape(1, -1).astype(F32)
    slopes = jnp.exp2(-8.0 * jnp.arange(1, n_heads + 1, dtype=F32) / n_heads)

    def q_params(i):
        j = i - n_a
        return row(ln_mix[i]), b_wq[j].astype(BF16), jnp.tile(row(b_q_gain[j]), (1, n_heads))

    def mlp(i, x, y, g, wo):
        kv = q = None
        if i == n_a - 1:
            kv = (row(kv_norm), _dup_heads(w_kv).astype(BF16), jnp.tile(row(k_gain), (1, 2 * N_KV)))
        if n_a - 1 <= i < n_a + n_b - 1:
            q = q_params(i + 1)
        return _post_mlp(x.reshape(M, C), y.reshape(M, C), None if g is None else g.reshape(M, C),
                         wo.astype(BF16), row(ln_mlp[i]), mlp_w1[i].astype(BF16), mlp_w2[i].astype(BF16),
                         tm, ones256, kv, q)

    v_first = None
    k2 = v2 = q = None
    for i in range(n_a + n_b):
        if i < n_a:
            j = i
            zeros = jnp.zeros((1, C), F32)
            vecs = jnp.concatenate(
                [a_mu[j], row(ln_mix[i]), row(a_w0[j]), row(a_a0[j]),
                 row(a_v0[j - 1]) if j > 0 else zeros, row(a_k_k[j]), row(a_k_a[j])]
                + [zeros] * 4, axis=0)
            loras = list(_lora(a_w1[j], a_w2[j]) + _lora(a_a1[j], a_a2[j]) + _lora(a_g1[j], a_g2[j]))
            if j > 0:
                loras += list(_lora(a_v1[j - 1], a_v2[j - 1]))
            r, lw, k, v, av, kk, g = _rwkv_pre(
                x, v_first if j > 0 else None, vecs, a_w_rkv[j].astype(BF16), loras, ones256, tm)
            if j == 0:
                v_first = v
            pvec = jnp.concatenate([row(a_r_k[j]), row(a_gn_g[j]), row(a_gn_b[j])]
                                   + [jnp.zeros((1, C), F32)] * 5, axis=0)
            y = _wkv(r, lw, k, v, av, kk, pvec, ones128, tt)
            outs = mlp(i, x, y, g, a_wo[j])
        else:
            o = _swa(q.reshape(B, T, C), k2, v2, slopes, b_sinks[i - n_a].astype(F32))
            outs = mlp(i, x, o, None, b_wo[i - n_a])
        outs = list(outs)
        x = outs.pop(0).reshape(B, T, C)
        if i == n_a - 1:
            k2 = outs.pop(0).reshape(B, T, -1)
            v2 = outs.pop(0).reshape(B, T, -1)
        if outs:
            q = outs.pop(0)
    return x
```

```python
import functools
import math

import jax
import jax.numpy as jnp
from jax import lax
from jax.experimental import pallas as pl
from jax.experimental.pallas import tpu as pltpu

F32 = jnp.float32
BF16 = jnp.bfloat16

HEAD = 64
PAIR = 2 * HEAD
N_KV = 2
WINDOW = 128
CHUNK = 64
GN_EPS = 64e-5
RMS_EPS = 1e-6
NEG_BIG = -1e30
LOG2E = math.log2(math.e)
VMEM_LIMIT = 56 * 1024 * 1024

NT_DIMS = (((1,), (1,)), ((), ()))


def _dot(a, b):
    return jnp.dot(a, b, preferred_element_type=F32)


def _dot_nt(a, b):
    return lax.dot_general(a, b, NT_DIMS, preferred_element_type=F32)


def _rms(x, g):
    return x * lax.rsqrt(jnp.mean(x * x, axis=-1, keepdims=True) + RMS_EPS) * g


def _group_sum(x, ones):
    w = ones.shape[0]
    parts = [_dot(x[:, j:j + w].astype(BF16), ones) for j in range(0, x.shape[1], w)]
    return parts[0] if len(parts) == 1 else jnp.concatenate(parts, axis=1)


def _sigmoid(x):
    return 0.5 * jnp.tanh(0.5 * x) + 0.5


def _const_spec(shape):
    nd = len(shape)
    return pl.BlockSpec(shape, lambda *_: (0,) * nd, pipeline_mode=pl.Buffered(1))


_V_MU, _V_LN, _V_W0, _V_A0, _V_V0, _V_KK, _V_KA = 0, 6, 7, 8, 9, 10, 11


def _rwkv_pre_kernel(has_vres, tm, *refs):
    if has_vres:
        (x_ref, vf_ref, vec_ref, wrkv_ref, w1_ref, w2_ref, a1_ref, a2_ref, g1_ref, g2_ref,
         v1_ref, v2_ref, ones_ref,
         r_ref, lw_ref, k_ref, v_ref, av_ref, kk_ref, g_ref, carry_ref) = refs
    else:
        (x_ref, vec_ref, wrkv_ref, w1_ref, w2_ref, a1_ref, a2_ref, g1_ref, g2_ref, ones_ref,
         r_ref, lw_ref, k_ref, v_ref, av_ref, kk_ref, g_ref, carry_ref) = refs

    @pl.when(pl.program_id(1) == 0)
    def _():
        carry_ref[...] = jnp.zeros_like(carry_ref)

    def vec(i):
        return vec_ref[i:i + 1, :]

    h = _rms(x_ref[0], vec(_V_LN))
    row = lax.broadcasted_iota(jnp.int32, h.shape, 0)
    prev = jnp.where(row == 0, carry_ref[0:1, :], pltpu.roll(h, 1, axis=0))
    carry_ref[0:1, :] = h[tm - 1:tm, :]
    xx = prev - h

    def mix(i):
        return (h + xx * vec(_V_MU + i)).astype(BF16)

    r = _dot(mix(0), wrkv_ref[0])
    k = _dot(mix(1), wrkv_ref[1])
    xv = mix(2)
    v = _dot(xv, wrkv_ref[2])

    wl = vec(_V_W0) + _dot(jnp.tanh(_dot(mix(3), w1_ref[...])).astype(BF16), w2_ref[...])
    lw_ref[0] = -math.exp(-0.5) * _sigmoid(wl)

    if has_vres:
        gate = _sigmoid(vec(_V_V0) + _dot(_dot(xv, v1_ref[...]).astype(BF16), v2_ref[...]))
        v = v + (vf_ref[0] - v) * gate
    a = _sigmoid(vec(_V_A0) + _dot(_dot(mix(4), a1_ref[...]).astype(BF16), a2_ref[...]))
    g_ref[0] = _dot(_sigmoid(_dot(mix(5), g1_ref[...])).astype(BF16), g2_ref[...])

    kk = k * vec(_V_KK)
    kk_ref[0] = kk * jnp.minimum(lax.rsqrt(_group_sum(kk * kk, ones_ref[...])), 1e12)
    k_ref[0] = k * (1.0 + (a - 1.0) * vec(_V_KA))
    r_ref[0] = r
    v_ref[0] = v
    av_ref[0] = a


def _rwkv_pre(x, v_first, vecs, wrkv, loras, ones, tm):
    B, T, C = x.shape
    has_vres = v_first is not None
    act = pl.BlockSpec((1, tm, C), lambda b, t: (b, t, 0))
    ins = [x] + ([v_first] if has_vres else []) + [vecs, wrkv] + list(loras) + [ones]
    in_specs = [act] * (2 if has_vres else 1) + [_const_spec(a.shape) for a in ins[(2 if has_vres else 1):]]
    out = jax.ShapeDtypeStruct((B, T, C), F32)
    return pl.pallas_call(
        functools.partial(_rwkv_pre_kernel, has_vres, tm),
        grid=(B, T // tm),
        in_specs=in_specs,
        out_specs=[act] * 7,
        out_shape=[out] * 7,
        scratch_shapes=[pltpu.VMEM((8, C), F32)],
        compiler_params=pltpu.CompilerParams(
            dimension_semantics=("parallel", "arbitrary"), vmem_limit_bytes=VMEM_LIMIT),
        name="rwkv_pre",
    )(*ins)


def _chunk_cumsum(x):
    pos = lax.broadcasted_iota(jnp.int32, x.shape, 0) & (CHUNK - 1)
    s = 1
    while s < CHUNK:
        x = x + jnp.where(pos >= s, pltpu.roll(x, s, axis=0), 0.0)
        s *= 2
    return x


def _each(f, *lists):
    return [f(*xs) for xs in zip(*lists)]


_WKV_PRE_STAGES = 8


def _wkv_chunks_pre(at, rt, kt, bt, v, p_last):
    L = CHUNK
    bf = lambda t: t.astype(BF16)
    lane = lax.broadcasted_iota(jnp.int32, (L, PAIR), 1)
    head0 = lane < HEAD
    row = lax.broadcasted_iota(jnp.int32, (L, 2 * L), 0)
    col = lax.broadcasted_iota(jnp.int32, (L, 2 * L), 1)
    src = jnp.where(col >= L, col - L, col)
    strict = src < row
    incl = src <= row
    left = col < L
    eye = jnp.where(src == row, 1.0, 0.0)

    def bd_rows(x):
        return jnp.concatenate([jnp.where(head0, x, 0.0), jnp.where(head0, 0.0, x)], axis=0)

    def bd_cols(x):
        return jnp.concatenate([jnp.where(left, x, 0.0), jnp.where(left, 0.0, x)], axis=0)

    def bd_rows2(x):
        return jnp.concatenate([bd_rows(x[:, :PAIR]), bd_rows(x[:, PAIR:])], axis=1)

    sc = _each(lambda a, r, b, k: _dot_nt(bf(jnp.concatenate([a, r], axis=0)),
                                          bf(jnp.concatenate([bd_rows(b), bd_rows(k)], axis=0))),
               at, rt, bt, kt)
    yield
    a_ab = [jnp.where(strict, s[:L, :2 * L], 0.0) for s in sc]
    a_ak = [jnp.where(strict, s[:L, 2 * L:], 0.0) for s in sc]
    bdv = _each(bd_rows, v)
    akv = _each(lambda a, b: _dot(bf(a), bf(b)), a_ak, bdv)
    p = _each(lambda a: _dot(bf(a), bf(bd_cols(a))), a_ab)
    t = [eye + a for a in a_ab]
    yield
    s = 2
    while 2 * s < L:
        o = _each(lambda pp, tt: _dot(bf(pp), bf(jnp.concatenate([bd_cols(tt), bd_cols(pp)], axis=1))), p, t)
        t = _each(lambda tt, oo: tt + oo[:, :2 * L], t, o)
        p = [oo[:, 2 * L:] for oo in o]
        s *= 2
        yield
    t = _each(lambda pp, tt: tt + _dot(bf(pp), bf(bd_cols(tt))), p, t)
    yield
    z = _each(lambda tt, a, u: _dot(bf(tt), bf(bd_rows2(jnp.concatenate([a, u], axis=1)))), t, at, akv)
    yield
    zeros2 = jnp.zeros((2 * L, PAIR), F32)
    o2 = _each(lambda s_, zz, bv: _dot(
        bf(jnp.concatenate([jnp.where(incl, s_[L:, :2 * L], 0.0), jnp.where(incl, s_[L:, 2 * L:], 0.0)], axis=1)),
        bf(jnp.concatenate([bd_rows2(zz), jnp.concatenate([zeros2, bv], axis=1)], axis=0))), sc, z, bdv)
    rh = _each(lambda r, oo: r + oo[:, :PAIR], rt, o2)
    yh = [oo[:, PAIR:] for oo in o2]
    yield
    zeros1 = jnp.zeros((L, PAIR), F32)
    o3 = _each(lambda b, k, pl_, zz, vv: _dot(
        bf(jnp.concatenate([b * pl_, k * pl_], axis=0).T),
        bf(jnp.concatenate([zz, jnp.concatenate([zeros1, vv], axis=1)], axis=0))), bt, kt, p_last, z, v)
    r128 = lax.broadcasted_iota(jnp.int32, (PAIR, PAIR), 0)
    c128 = lax.broadcasted_iota(jnp.int32, (PAIR, PAIR), 1)
    same_head = (r128 < HEAD) == (c128 < HEAD)
    m = _each(lambda oo, pl_: jnp.where(same_head, oo[:, :PAIR], 0.0) + jnp.where(r128 == c128, pl_, 0.0), o3, p_last)
    g = [jnp.where(same_head, oo[:, PAIR:], 0.0) for oo in o3]
    return rh, yh, m, g


def _wkv_chain(n_chunks, first_tile, mrh_ref, g_ref, yh_ref, h_ref, ys):
    L = CHUNK
    h = jnp.where(first_tile, 0.0, h_ref[...])
    for ci in range(n_chunks):
        o = _dot(mrh_ref[ci], h.astype(BF16))
        h = o[:PAIR] + g_ref[ci]
        ys.append(o[PAIR:] + yh_ref[ci * L:(ci + 1) * L, :])
        yield
    h_ref[...] = h


def _wkv_kernel(n_chunks, tiles_per_seq,
                r_ref, lw_ref, k_ref, v_ref, av_ref, kk_ref, pvec_in_ref, pvec_out_ref, ones_ref,
                y_ref, h_ref, mrh_ref, g_ref, yh_ref, bv_ref):
    L = CHUNK
    s = pl.program_id(0)

    @pl.when(s == 0)
    def _():
        h_ref[...] = jnp.zeros_like(h_ref)
        mrh_ref[...] = jnp.zeros_like(mrh_ref)
        g_ref[...] = jnp.zeros_like(g_ref)
        yh_ref[...] = jnp.zeros_like(yh_ref)
        bv_ref[...] = jnp.zeros_like(bv_ref)

    cur = s % 2
    prev = 1 - cur
    ones = ones_ref[...]

    r, lw, k, v = r_ref[0], lw_ref[0], k_ref[0], v_ref[0]
    kk = kk_ref[0]
    c = _chunk_cumsum(lw)
    ec = jnp.exp(c)
    enc = jnp.exp(-c)
    rt = r * ec
    at = -kk * jnp.exp(c - lw)
    kt = k * enc
    bt = kk * av_ref[0] * enc

    chunks = lambda x: [x[ci * L:(ci + 1) * L] for ci in range(n_chunks)]
    p_last = [ec[(ci + 1) * L - 1:(ci + 1) * L, :] for ci in range(n_chunks)]

    ys = []
    first_tile = (s - 1) % tiles_per_seq == 0
    chain = _wkv_chain(n_chunks, first_tile, mrh_ref.at[prev], g_ref.at[prev], yh_ref.at[prev], h_ref, ys)
    pre = _wkv_chunks_pre(chunks(at), chunks(rt), chunks(kt), chunks(bt), chunks(v), p_last)
    chain_steps_per_stage = -(-n_chunks // _WKV_PRE_STAGES)
    while True:
        try:
            next(pre)
        except StopIteration as stop:
            rh, yh, m, g = stop.value
            break
        for _ in range(chain_steps_per_stage):
            next(chain, None)
    for _ in chain:
        pass

    y = jnp.concatenate(ys, axis=0)
    inv_n = 1.0 / HEAD
    d = y - _group_sum(y, ones) * inv_n
    var = _group_sum(d * d, ones) * inv_n
    y_ref[0] = d * lax.rsqrt(var + GN_EPS) * pvec_out_ref[1:2, :] + pvec_out_ref[2:3, :] + bv_ref[prev]

    for ci in range(n_chunks):
        mrh_ref[cur, ci] = jnp.concatenate([m[ci], rh[ci]], axis=0).astype(BF16)
        g_ref[cur, ci] = g[ci]
    yh_ref[cur] = jnp.concatenate(yh, axis=0)
    bv_ref[cur] = _group_sum(r * k * pvec_in_ref[0:1, :], ones) * v


def _wkv(r, lw, k, v, av, kk, pvec, ones, tt):
    B, T, C = r.shape
    n_pairs = C // PAIR
    n_tiles = T // tt
    n_chunks = tt // CHUNK
    steps = B * n_pairs * n_tiles

    def tile_index(i):
        return i // (n_pairs * n_tiles), i % n_tiles, (i // n_tiles) % n_pairs

    cur = lambda s: jnp.minimum(s, steps - 1)
    prev = lambda s: jnp.maximum(s - 1, 0)
    act_in = pl.BlockSpec((1, tt, PAIR), lambda s: tile_index(cur(s)))
    act_out = pl.BlockSpec((1, tt, PAIR), lambda s: tile_index(prev(s)))
    return pl.pallas_call(
        functools.partial(_wkv_kernel, n_chunks, n_tiles),
        grid=(steps + 1,),
        in_specs=[act_in] * 6 + [pl.BlockSpec((8, PAIR), lambda s: (0, tile_index(cur(s))[2])),
                                 pl.BlockSpec((8, PAIR), lambda s: (0, tile_index(prev(s))[2])),
                                 pl.BlockSpec(ones.shape, lambda s: (0, 0))],
        out_specs=act_out,
        out_shape=jax.ShapeDtypeStruct((B, T, C), F32),
        scratch_shapes=[pltpu.VMEM((PAIR, PAIR), F32),
                        pltpu.VMEM((2, n_chunks, PAIR + CHUNK, PAIR), BF16),
                        pltpu.VMEM((2, n_chunks, PAIR, PAIR), F32),
                        pltpu.VMEM((2, tt, PAIR), F32),
                        pltpu.VMEM((2, tt, PAIR), F32)],
        compiler_params=pltpu.CompilerParams(
            dimension_semantics=("arbitrary",), vmem_limit_bytes=VMEM_LIMIT),
        name="wkv",
    )(r, lw, k, v, av, kk, pvec, pvec, ones)


def _post_mlp_kernel(has_gate, has_kv, has_q, *refs):
    refs = list(refs)
    x_ref, y_ref = refs[:2]
    del refs[:2]
    g_ref = refs.pop(0) if has_gate else None
    wo_ref, ln_ref, w1_ref, w2_ref = refs[:4]
    del refs[:4]
    if has_kv or has_q:
        ones_ref = refs.pop(0)
    if has_kv:
        kv_ln_ref, wkv_ref, kg_ref = refs[:3]
        del refs[:3]
    if has_q:
        q_ln_ref, wq_ref, qg_ref = refs[:3]
        del refs[:3]
    o_ref = refs.pop(0)
    if has_kv:
        k_ref, v_ref = refs[:2]
        del refs[:2]
    if has_q:
        q_ref = refs.pop(0)

    y = y_ref[...]
    if has_gate:
        y = y * g_ref[...]
    xn = x_ref[...] + _dot(y.astype(BF16), wo_ref[...])
    hid = _dot(_rms(xn, ln_ref[...]).astype(BF16), w1_ref[...])
    hid = jnp.square(jnp.maximum(hid, 0.0)).astype(BF16)
    out = xn + _dot(hid, w2_ref[...])
    o_ref[...] = out

    if has_kv:
        kv = _dot(_rms(out, kv_ln_ref[...]).astype(BF16), wkv_ref[...])
        half = kv.shape[1] // 2
        k = kv[:, :half]
        ms = _group_sum(k * k, ones_ref[...]) * (1.0 / HEAD)
        k_ref[...] = (k * lax.rsqrt(ms + RMS_EPS) * kg_ref[...]).astype(BF16)
        v_ref[...] = kv[:, half:].astype(BF16)
    if has_q:
        q = _dot(_rms(out, q_ln_ref[...]).astype(BF16), wq_ref[...])
        ms = _group_sum(q * q, ones_ref[...]) * (1.0 / HEAD)
        q_ref[...] = (q * lax.rsqrt(ms + RMS_EPS) * qg_ref[...] * (HEAD ** -0.5 * LOG2E)).astype(BF16)


def _post_mlp(x, y, g, wo, ln, w1, w2, tm, ones=None, kv=None, q=None):
    M, C = x.shape
    act = pl.BlockSpec((tm, C), lambda i: (i, 0))
    has_gate, has_kv, has_q = g is not None, kv is not None, q is not None
    acts = [x, y] + ([g] if has_gate else [])
    consts = [wo, ln, w1, w2] + ([ones] if has_kv or has_q else []) + list(kv or ()) + list(q or ())
    out_shape = [jax.ShapeDtypeStruct((M, C), F32)]
    out_specs = [act]
    if has_kv:
        n = kv[1].shape[1] // 2
        out_shape += [jax.ShapeDtypeStruct((M, n), BF16)] * 2
        out_specs += [pl.BlockSpec((tm, n), lambda i: (i, 0))] * 2
    if has_q:
        n = q[1].shape[1]
        out_shape.append(jax.ShapeDtypeStruct((M, n), BF16))
        out_specs.append(pl.BlockSpec((tm, n), lambda i: (i, 0)))
    return pl.pallas_call(
        functools.partial(_post_mlp_kernel, has_gate, has_kv, has_q),
        grid=(M // tm,),
        in_specs=[act] * len(acts) + [_const_spec(a.shape) for a in consts],
        out_specs=out_specs,
        out_shape=out_shape,
        compiler_params=pltpu.CompilerParams(
            dimension_semantics=("parallel",), vmem_limit_bytes=VMEM_LIMIT),
        name="post_mlp",
    )(*acts, *consts)


def _swa_bias_init(n_q_heads, slopes_ref, bias_ref):
    W = WINDOW
    qi = lax.broadcasted_iota(jnp.int32, (W, 2 * W), 0)
    kj = lax.broadcasted_iota(jnp.int32, (W, 2 * W), 1)
    dist = qi + W - kj
    in_window = (dist >= 0) & (dist < WINDOW)
    in_window_cur = in_window & (kj >= W)
    distf = dist.astype(F32)
    for head in range(n_q_heads):
        alibi = -(slopes_ref[head] * LOG2E) * distf
        bias_ref[0, head] = jnp.where(in_window, alibi, NEG_BIG)
        bias_ref[1, head] = jnp.where(in_window_cur, alibi, NEG_BIG)


def _swa_tile(n_q_heads, first_in_row, sinks_ref, q_ref, k_refs, v_refs, bias_ref, attn_ref):
    W = WINDOW
    group = n_q_heads // N_KV
    pairs = group // 2
    lane = lax.broadcasted_iota(jnp.int32, (W, PAIR), 1)
    head0 = lane < HEAD
    lane2 = lax.broadcasted_iota(jnp.int32, (2 * W, PAIR), 1)
    head0_kv = lane2 < HEAD
    zero = jnp.zeros((), BF16)

    for qb in range(q_ref.shape[0] // W):
        rows = slice(qb * W, (qb + 1) * W)
        plane = jnp.where(first_in_row, 1, 0) if qb == 0 else 0
        for h in range(N_KV):
            ks = slice(h * PAIR, (h + 1) * PAIR)
            k2 = jnp.concatenate([k_refs[qb][:, ks], k_refs[qb + 1][:, ks]], axis=0)
            v2 = jnp.concatenate([v_refs[qb][:, ks], v_refs[qb + 1][:, ks]], axis=0)
            q_tiles = [q_ref[rows, (h * pairs + j) * PAIR:(h * pairs + j + 1) * PAIR] for j in range(pairs)]
            lhs = jnp.concatenate([jnp.where(head0, t, zero) for t in q_tiles]
                                  + [jnp.where(head0, zero, t) for t in q_tiles], axis=0)
            s_all = _dot_nt(lhs, k2)
            yield
            probs, denoms = [], []
            for i in range(group):
                head = h * group + 2 * (i % pairs) + i // pairs
                s = s_all[i * W:(i + 1) * W, :] + bias_ref[plane, head]
                sink = sinks_ref[head] * LOG2E
                mx = jnp.maximum(jnp.max(s, axis=-1, keepdims=True), sink)
                p = jnp.exp2(s - mx)
                denoms.append(jnp.sum(p, axis=-1, keepdims=True) + jnp.exp2(sink - mx))
                probs.append(p.astype(BF16))
            p_first = jnp.concatenate(probs[:pairs], axis=0)
            p_second = jnp.concatenate(probs[pairs:], axis=0)
            v_stack = jnp.concatenate([jnp.where(head0_kv, v2, zero), jnp.where(head0_kv, zero, v2)], axis=0)
            o = _dot(jnp.concatenate([p_first, p_second], axis=1), v_stack)
            for j in range(pairs):
                den = jnp.where(head0, denoms[j], denoms[pairs + j])
                col = (h * pairs + j) * PAIR
                attn_ref[rows, col:col + PAIR] = (o[j * W:(j + 1) * W, :] / den).astype(attn_ref.dtype)


_MLP_SPLIT = 4


def _mlp_stages(x_ref, attn_ref, wo_ref, ln_ref, w1_ref, w2_ref, result):
    xn = x_ref[...] + _dot(attn_ref[...], wo_ref[...])
    h = _rms(xn, ln_ref[...]).astype(BF16)
    yield
    step = w1_ref.shape[1] // _MLP_SPLIT
    hids = []
    for c in range(_MLP_SPLIT):
        hid = _dot(h, w1_ref[:, c * step:(c + 1) * step])
        hids.append(jnp.square(jnp.maximum(hid, 0.0)).astype(BF16))
        if c % 2 == 1:
            yield
    acc = xn
    for c in range(_MLP_SPLIT):
        acc = acc + _dot(hids[c], w2_ref[c * step:(c + 1) * step, :])
        if c % 2 == 1 and c + 1 < _MLP_SPLIT:
            yield
    result.append(acc)


def _swa_mlp_kernel(has_q, n_q_heads, tiles_per_row, slopes_ref, sinks_ref, *refs):
    refs = list(refs)
    x_ref, q_ref, kp_ref, k0_ref, k1_ref, vp_ref, v0_ref, v1_ref, wo_ref, ln_ref, w1_ref, w2_ref = refs[:12]
    del refs[:12]
    if has_q:
        ones_ref, q_ln_ref, wq_ref, qg_ref = refs[:4]
        del refs[:4]
    o_ref = refs.pop(0)
    qn_ref = refs.pop(0) if has_q else None
    attn_ref, bias_ref = refs
    s = pl.program_id(0)

    @pl.when(s == 0)
    def _():
        attn_ref[...] = jnp.zeros_like(attn_ref)
        _swa_bias_init(n_q_heads, slopes_ref, bias_ref)

    cur = s % 2
    first_in_row = s % tiles_per_row == 0
    result = []
    swa = _swa_tile(n_q_heads, first_in_row, sinks_ref, q_ref, [kp_ref, k0_ref, k1_ref],
                    [vp_ref, v0_ref, v1_ref], bias_ref, attn_ref.at[cur])
    mlp = _mlp_stages(x_ref, attn_ref.at[1 - cur], wo_ref, ln_ref, w1_ref, w2_ref, result)
    for _ in swa:
        next(mlp, None)
    for _ in mlp:
        pass
    out = result[0]
    o_ref[...] = out
    if has_q:
        q = _dot(_rms(out, q_ln_ref[...]).astype(BF16), wq_ref[...])
        ms = _group_sum(q * q, ones_ref[...]) * (1.0 / HEAD)
        qn_ref[...] = (q * lax.rsqrt(ms + RMS_EPS) * qg_ref[...] * (HEAD ** -0.5 * LOG2E)).astype(BF16)


def _swa_mlp(x, q, k2, v2, slopes, sinks, wo, ln, w1, w2, tm, tokens_per_row, ones=None, q_next=None):
    M, C = x.shape
    W = WINDOW
    steps = M // tm
    blocks = tm // W
    assert blocks == 2 and tokens_per_row % tm == 0
    has_q = q_next is not None
    t_att = lambda s: jnp.minimum(s, steps - 1)
    t_mlp = lambda s: jnp.maximum(s - 1, 0)
    kv_spec = lambda off: pl.BlockSpec(
        (W, k2.shape[1]), lambda s, *_: (jnp.maximum(t_att(s) * blocks + off, 0), 0))
    act_mlp = pl.BlockSpec((tm, C), lambda s, *_: (t_mlp(s), 0))
    consts = [wo, ln, w1, w2] + ([ones] + list(q_next) if has_q else [])
    out_shape = [jax.ShapeDtypeStruct((M, C), F32)]
    out_specs = [act_mlp]
    if has_q:
        out_shape.append(jax.ShapeDtypeStruct((M, q_next[1].shape[1]), BF16))
        out_specs.append(pl.BlockSpec((tm, q_next[1].shape[1]), lambda s, *_: (t_mlp(s), 0)))
    return pl.pallas_call(
        functools.partial(_swa_mlp_kernel, has_q, C // HEAD, tokens_per_row // tm),
        grid_spec=pltpu.PrefetchScalarGridSpec(
            num_scalar_prefetch=2,
            grid=(steps + 1,),
            in_specs=[act_mlp, pl.BlockSpec((tm, C), lambda s, *_: (t_att(s), 0)),
                      kv_spec(-1), kv_spec(0), kv_spec(1), kv_spec(-1), kv_spec(0), kv_spec(1)]
                     + [_const_spec(a.shape) for a in consts],
            out_specs=out_specs,
            scratch_shapes=[pltpu.VMEM((2, tm, C), BF16),
                            pltpu.VMEM((2, C // HEAD, W, 2 * W), F32)],
        ),
        out_shape=out_shape,
        compiler_params=pltpu.CompilerParams(
            dimension_semantics=("arbitrary",), vmem_limit_bytes=VMEM_LIMIT),
        name="swa_mlp",
    )(slopes, sinks, x, q, k2, k2, k2, v2, v2, v2, *consts)


def _pad_cols(w, n):
    return jnp.pad(w, ((0, 0), (0, n - w.shape[1])))


def _pad_rows(w, n):
    return jnp.pad(w, ((0, n - w.shape[0]), (0, 0)))


def _round_up(n, m):
    return (n + m - 1) // m * m


def _lora(w_in, w_out):
    n = _round_up(w_in.shape[1], 128)
    return _pad_cols(w_in, n).astype(BF16), _pad_rows(w_out, n).astype(BF16)


def _dup_heads(w):
    c, n = w.shape
    w = w.reshape(c, n // HEAD, 1, HEAD)
    return jnp.broadcast_to(w, (c, n // HEAD, 2, HEAD)).reshape(c, 2 * n)


def _block_ones(n):
    i = jnp.arange(n) // HEAD
    return (i[:, None] == i[None, :]).astype(BF16)


def kernel(x, ln_mix, ln_mlp, mlp_w1, mlp_w2, a_mu, a_w_rkv, a_w0, a_w1, a_w2, a_a0, a_a1, a_a2, a_g1, a_g2, a_k_k, a_k_a, a_r_k, a_gn_g, a_gn_b, a_wo, a_v0, a_v1, a_v2, kv_norm, w_kv, k_gain, b_wq, b_q_gain, b_sinks, b_wo):
    B, T, C = x.shape
    M = B * T
    n_a = a_mu.shape[0]
    n_b = b_wq.shape[0]
    n_heads = C // HEAD
    tm = 256
    tm_pre = 512
    tt = 1024
    ones256 = _block_ones(256)
    ones128 = _block_ones(PAIR)
    row = lambda v: v.reshape(1, -1).astype(F32)
    slopes = jnp.exp2(-8.0 * jnp.arange(1, n_heads + 1, dtype=F32) / n_heads)

    def q_params(i):
        j = i - n_a
        return row(ln_mix[i]), b_wq[j].astype(BF16), jnp.tile(row(b_q_gain[j]), (1, n_heads))

    def mlp(i, x, y, g, wo):
        kv = q = None
        if i == n_a - 1:
            kv = (row(kv_norm), _dup_heads(w_kv).astype(BF16), jnp.tile(row(k_gain), (1, 2 * N_KV)))
        if n_a - 1 <= i < n_a + n_b - 1:
            q = q_params(i + 1)
        return _post_mlp(x.reshape(M, C), y.reshape(M, C), None if g is None else g.reshape(M, C),
                         wo.astype(BF16), row(ln_mlp[i]), mlp_w1[i].astype(BF16), mlp_w2[i].astype(BF16),
                         tm, ones256, kv, q)

    v_first = None
    k2 = v2 = q = None
    for i in range(n_a + n_b):
        if i < n_a:
            j = i
            zeros = jnp.zeros((1, C), F32)
            vecs = jnp.concatenate(
                [a_mu[j], row(ln_mix[i]), row(a_w0[j]), row(a_a0[j]),
                 row(a_v0[j - 1]) if j > 0 else zeros, row(a_k_k[j]), row(a_k_a[j])]
                + [zeros] * 4, axis=0)
            loras = list(_lora(a_w1[j], a_w2[j]) + _lora(a_a1[j], a_a2[j]) + _lora(a_g1[j], a_g2[j]))
            if j > 0:
                loras += list(_lora(a_v1[j - 1], a_v2[j - 1]))
            r, lw, k, v, av, kk, g = _rwkv_pre(
                x, v_first if j > 0 else None, vecs, a_w_rkv[j].astype(BF16), loras, ones256, tm_pre)
            if j == 0:
                v_first = v
            pvec = jnp.concatenate([row(a_r_k[j]), row(a_gn_g[j]), row(a_gn_b[j])]
                                   + [jnp.zeros((1, C), F32)] * 5, axis=0)
            y = _wkv(r, lw, k, v, av, kk, pvec, ones128, tt)
            outs = mlp(i, x, y, g, a_wo[j])
        else:
            outs = _swa_mlp(x.reshape(M, C), q, k2, v2, slopes, b_sinks[i - n_a].astype(F32),
                            b_wo[i - n_a].astype(BF16), row(ln_mlp[i]), mlp_w1[i].astype(BF16),
                            mlp_w2[i].astype(BF16), tm, T, ones256, q_params(i + 1) if i + 1 < n_a + n_b else None)
        outs = list(outs)
        x = outs.pop(0).reshape(B, T, C)
        if i == n_a - 1:
            k2 = outs.pop(0)
            v2 = outs.pop(0)
        if outs:
            q = outs.pop(0)
    return x
```

```python
import functools
import math

import jax
import jax.numpy as jnp
from jax import lax
from jax.experimental import pallas as pl
from jax.experimental.pallas import tpu as pltpu

F32 = jnp.float32
BF16 = jnp.bfloat16

HEAD = 64
PAIR = 2 * HEAD
N_KV = 2
WINDOW = 128
CHUNK = 64
GN_EPS = 64e-5
RMS_EPS = 1e-6
NEG_BIG = -1e30
LOG2E = math.log2(math.e)
VMEM_LIMIT = 56 * 1024 * 1024

NT_DIMS = (((1,), (1,)), ((), ()))


def _dot(a, b):
    return jnp.dot(a, b, preferred_element_type=F32)


def _dot_nt(a, b):
    return lax.dot_general(a, b, NT_DIMS, preferred_element_type=F32)


def _rms(x, g):
    return x * lax.rsqrt(jnp.mean(x * x, axis=-1, keepdims=True) + RMS_EPS) * g


def _group_sum(x, ones):
    w = ones.shape[0]
    parts = [_dot(x[:, j:j + w].astype(BF16), ones) for j in range(0, x.shape[1], w)]
    return parts[0] if len(parts) == 1 else jnp.concatenate(parts, axis=1)


def _sigmoid(x):
    return 0.5 * jnp.tanh(0.5 * x) + 0.5


def _const_spec(shape):
    nd = len(shape)
    return pl.BlockSpec(shape, lambda *_: (0,) * nd, pipeline_mode=pl.Buffered(1))


_V_MU, _V_LN, _V_W0, _V_A0, _V_V0, _V_KK, _V_KA = 0, 6, 7, 8, 9, 10, 11


def _rwkv_pre_kernel(has_vres, tm, *refs):
    if has_vres:
        (x_ref, vf_ref, vec_ref, wrkv_ref, w1_ref, w2_ref, a1_ref, a2_ref, g1_ref, g2_ref,
         v1_ref, v2_ref, ones_ref,
         r_ref, lw_ref, k_ref, v_ref, av_ref, kk_ref, g_ref, carry_ref) = refs
    else:
        (x_ref, vec_ref, wrkv_ref, w1_ref, w2_ref, a1_ref, a2_ref, g1_ref, g2_ref, ones_ref,
         r_ref, lw_ref, k_ref, v_ref, av_ref, kk_ref, g_ref, carry_ref) = refs

    @pl.when(pl.program_id(1) == 0)
    def _():
        carry_ref[...] = jnp.zeros_like(carry_ref)

    def vec(i):
        return vec_ref[i:i + 1, :]

    h = _rms(x_ref[0], vec(_V_LN))
    row = lax.broadcasted_iota(jnp.int32, h.shape, 0)
    prev = jnp.where(row == 0, carry_ref[0:1, :], pltpu.roll(h, 1, axis=0))
    carry_ref[0:1, :] = h[tm - 1:tm, :]
    xx = prev - h

    def mix(i):
        return (h + xx * vec(_V_MU + i)).astype(BF16)

    r = _dot(mix(0), wrkv_ref[0])
    k = _dot(mix(1), wrkv_ref[1])
    xv = mix(2)
    v = _dot(xv, wrkv_ref[2])

    wl = vec(_V_W0) + _dot(jnp.tanh(_dot(mix(3), w1_ref[...])).astype(BF16), w2_ref[...])
    lw_ref[0] = -math.exp(-0.5) * _sigmoid(wl)

    if has_vres:
        gate = _sigmoid(vec(_V_V0) + _dot(_dot(xv, v1_ref[...]).astype(BF16), v2_ref[...]))
        v = v + (vf_ref[0] - v) * gate
    a = _sigmoid(vec(_V_A0) + _dot(_dot(mix(4), a1_ref[...]).astype(BF16), a2_ref[...]))
    g_ref[0] = _dot(_sigmoid(_dot(mix(5), g1_ref[...])).astype(BF16), g2_ref[...])

    kk = k * vec(_V_KK)
    kk_ref[0] = kk * jnp.minimum(lax.rsqrt(_group_sum(kk * kk, ones_ref[...])), 1e12)
    k_ref[0] = k * (1.0 + (a - 1.0) * vec(_V_KA))
    r_ref[0] = r
    v_ref[0] = v
    av_ref[0] = a


def _rwkv_pre(x, v_first, vecs, wrkv, loras, ones, tm):
    B, T, C = x.shape
    has_vres = v_first is not None
    act = pl.BlockSpec((1, tm, C), lambda b, t: (b, t, 0))
    ins = [x] + ([v_first] if has_vres else []) + [vecs, wrkv] + list(loras) + [ones]
    in_specs = [act] * (2 if has_vres else 1) + [_const_spec(a.shape) for a in ins[(2 if has_vres else 1):]]
    out = jax.ShapeDtypeStruct((B, T, C), F32)
    return pl.pallas_call(
        functools.partial(_rwkv_pre_kernel, has_vres, tm),
        grid=(B, T // tm),
        in_specs=in_specs,
        out_specs=[act] * 7,
        out_shape=[out] * 7,
        scratch_shapes=[pltpu.VMEM((8, C), F32)],
        compiler_params=pltpu.CompilerParams(
            dimension_semantics=("parallel", "arbitrary"), vmem_limit_bytes=VMEM_LIMIT),
        name="rwkv_pre",
    )(*ins)


def _chunk_cumsum(x):
    pos = lax.broadcasted_iota(jnp.int32, x.shape, 0) & (CHUNK - 1)
    s = 1
    while s < CHUNK:
        x = x + jnp.where(pos >= s, pltpu.roll(x, s, axis=0), 0.0)
        s *= 2
    return x


def _each(f, *lists):
    return [f(*xs) for xs in zip(*lists)]


_WKV_PRE_STAGES = 8
_T_AT, _T_RT, _T_KT, _T_BT, _T_V, _T_EC = range(6)


def _wkv_phase0(n_blocks, never, r_ref, lw_ref, k_ref, v_ref, av_ref, kk_ref, rk, ones, tilde_ref, bv_ref):
    rows = r_ref.shape[1] // n_blocks
    for i in range(n_blocks):
        after = yield
        sl = slice(i * rows, (i + 1) * rows)
        r, lw, k, v, kk = r_ref[0, sl], lw_ref[0, sl], k_ref[0, sl], v_ref[0, sl], kk_ref[0, sl]
        if after is not None:
            lw = lw + jnp.where(never, jnp.concatenate([after] * (rows // after.shape[0]), axis=0), 0.0)
        c = _chunk_cumsum(lw)
        ec = jnp.exp(c)
        enc = jnp.exp(-c)
        tilde_ref[_T_AT, sl] = -kk * jnp.exp(c - lw)
        tilde_ref[_T_RT, sl] = r * ec
        tilde_ref[_T_KT, sl] = k * enc
        tilde_ref[_T_BT, sl] = kk * av_ref[0, sl] * enc
        tilde_ref[_T_V, sl] = v
        tilde_ref[_T_EC, sl] = ec
        bv_ref[sl] = _group_sum(r * k * rk, ones) * v
    yield


def _wkv_chunks_pre(n_chunks, tilde_ref):
    L = CHUNK
    bf = lambda t: t.astype(BF16)
    lane = lax.broadcasted_iota(jnp.int32, (L, PAIR), 1)
    head0 = lane < HEAD
    row = lax.broadcasted_iota(jnp.int32, (L, 2 * L), 0)
    col = lax.broadcasted_iota(jnp.int32, (L, 2 * L), 1)
    src = jnp.where(col >= L, col - L, col)
    strict = src < row
    incl = src <= row
    left = col < L
    eye = jnp.where(src == row, 1.0, 0.0)
    cis = list(range(n_chunks))
    mid = n_chunks // 2 - 1

    def plane(which):
        vals = [tilde_ref[which, ci * L:(ci + 1) * L, :] for ci in cis]
        return lambda ci: vals[ci]

    at, rt, kt, bt, v = plane(_T_AT), plane(_T_RT), plane(_T_KT), plane(_T_BT), plane(_T_V)
    p_last_vals = [tilde_ref[_T_EC, (ci + 1) * L - 1:(ci + 1) * L, :] for ci in cis]
    p_last = lambda ci: p_last_vals[ci]

    def bd_rows(x):
        return jnp.concatenate([jnp.where(head0, x, 0.0), jnp.where(head0, 0.0, x)], axis=0)

    def bd_cols(x):
        return jnp.concatenate([jnp.where(left, x, 0.0), jnp.where(left, 0.0, x)], axis=0)

    def bd_rows2(x):
        return jnp.concatenate([bd_rows(x[:, :PAIR]), bd_rows(x[:, PAIR:])], axis=1)

    sc = [_dot_nt(bf(jnp.concatenate([at(ci), rt(ci)], axis=0)),
                  bf(jnp.concatenate([bd_rows(bt(ci)), bd_rows(kt(ci))], axis=0))) for ci in cis]
    yield sc[mid][:L, :PAIR], sc[-1][:L, :PAIR]
    a_ab = [jnp.where(strict, s[:L, :2 * L], 0.0) for s in sc]
    akv = [_dot(bf(jnp.where(strict, sc[ci][:L, 2 * L:], 0.0)), bf(bd_rows(v(ci)))) for ci in cis]
    p = _each(lambda a: _dot(bf(a), bf(bd_cols(a))), a_ab)
    t = [eye + a for a in a_ab]
    yield p[mid], p[-1]
    s = 2
    while 2 * s < L:
        o = _each(lambda pp, tt: _dot(bf(pp), bf(jnp.concatenate([bd_cols(tt), bd_cols(pp)], axis=1))), p, t)
        t = _each(lambda tt, oo: tt + oo[:, :2 * L], t, o)
        p = [oo[:, 2 * L:] for oo in o]
        s *= 2
        yield p[mid], p[-1]
    t = _each(lambda pp, tt: tt + _dot(bf(pp), bf(bd_cols(tt))), p, t)
    yield t[mid], t[-1]
    z = [_dot(bf(t[ci]), bf(bd_rows2(jnp.concatenate([at(ci), akv[ci]], axis=1)))) for ci in cis]
    yield z[mid][:, :PAIR], z[-1][:, :PAIR]
    zeros2 = jnp.zeros((2 * L, PAIR), F32)
    o2 = [_dot(
        bf(jnp.concatenate([jnp.where(incl, sc[ci][L:, :2 * L], 0.0), jnp.where(incl, sc[ci][L:, 2 * L:], 0.0)], axis=1)),
        bf(jnp.concatenate([bd_rows2(z[ci]), jnp.concatenate([zeros2, bd_rows(v(ci))], axis=1)], axis=0)))
        for ci in cis]
    rh = [rt(ci) + o2[ci][:, :PAIR] for ci in cis]
    yh = [oo[:, PAIR:] for oo in o2]
    yield yh[mid], yh[-1]
    zeros1 = jnp.zeros((L, PAIR), F32)
    o3 = [_dot(
        bf(jnp.concatenate([bt(ci) * p_last(ci), kt(ci) * p_last(ci)], axis=0).T),
        bf(jnp.concatenate([z[ci], jnp.concatenate([zeros1, v(ci)], axis=1)], axis=0))) for ci in cis]
    r128 = lax.broadcasted_iota(jnp.int32, (PAIR, PAIR), 0)
    c128 = lax.broadcasted_iota(jnp.int32, (PAIR, PAIR), 1)
    same_head = (r128 < HEAD) == (c128 < HEAD)
    m = [jnp.where(same_head, o3[ci][:, :PAIR], 0.0) + jnp.where(r128 == c128, p_last(ci), 0.0) for ci in cis]
    g = [jnp.where(same_head, oo[:, PAIR:], 0.0) for oo in o3]
    return rh, yh, m, g


def _wkv_chain(n_chunks, first_tile, mrh_ref, g_ref, yh_ref, h_ref, ys):
    L = CHUNK
    h = jnp.where(first_tile, 0.0, h_ref[...])
    for ci in range(n_chunks):
        o = _dot(mrh_ref[ci], h.astype(BF16))
        h = o[:PAIR] + g_ref[ci]
        ys.append(o[PAIR:] + yh_ref[ci * L:(ci + 1) * L, :])
        yield
    h_ref[...] = h


def _wkv_kernel(n_chunks, tiles_per_seq,
                r_ref, lw_ref, k_ref, v_ref, av_ref, kk_ref, pvec_in_ref, pvec_out_ref, ones_ref,
                y_ref, h_ref, tilde_ref, bv_ref, mrh_ref, g_ref, yh_ref):
    s = pl.program_id(0)

    @pl.when(s == 0)
    def _():
        for ref in (h_ref, tilde_ref, bv_ref, mrh_ref, g_ref, yh_ref):
            ref[...] = jnp.zeros_like(ref)

    p0 = s % 2
    p1 = 1 - p0
    b_in = s % 3
    b_out = (s + 1) % 3
    ones = ones_ref[...]

    ys = []
    first_tile = (s - 2) % tiles_per_seq == 0
    chain = _wkv_chain(n_chunks, first_tile, mrh_ref.at[p0], g_ref.at[p0], yh_ref.at[p0], h_ref, ys)
    pre = _wkv_chunks_pre(n_chunks, tilde_ref.at[p1])
    phase0 = _wkv_phase0(2 * _WKV_PRE_STAGES, s < 0, r_ref, lw_ref, k_ref, v_ref, av_ref, kk_ref,
                         pvec_in_ref[0:1, :], ones, tilde_ref.at[p0], bv_ref.at[b_in])
    next(phase0)
    chain_steps_per_stage = -(-n_chunks // _WKV_PRE_STAGES)
    blocks_left = 2 * _WKV_PRE_STAGES
    while True:
        try:
            deps = next(pre)
        except StopIteration as stop:
            rh, yh, m, g = stop.value
            break
        for _ in range(chain_steps_per_stage):
            next(chain, None)
        for dep in deps:
            if blocks_left:
                phase0.send(dep)
                blocks_left -= 1
    for _ in chain:
        pass
    assert blocks_left == 0

    y = jnp.concatenate(ys, axis=0)
    inv_n = 1.0 / HEAD
    d = y - _group_sum(y, ones) * inv_n
    var = _group_sum(d * d, ones) * inv_n
    y_ref[0] = d * lax.rsqrt(var + GN_EPS) * pvec_out_ref[1:2, :] + pvec_out_ref[2:3, :] + bv_ref[b_out]

    for ci in range(n_chunks):
        mrh_ref[p1, ci] = jnp.concatenate([m[ci], rh[ci]], axis=0).astype(BF16)
        g_ref[p1, ci] = g[ci]
    yh_ref[p1] = jnp.concatenate(yh, axis=0)


def _wkv(r, lw, k, v, av, kk, pvec, ones, tt):
    B, T, C = r.shape
    n_pairs = C // PAIR
    n_tiles = T // tt
    n_chunks = tt // CHUNK
    steps = B * n_pairs * n_tiles

    def tile_index(i):
        return i // (n_pairs * n_tiles), i % n_tiles, (i // n_tiles) % n_pairs

    t_in = lambda s: jnp.minimum(s, steps - 1)
    t_out = lambda s: jnp.clip(s - 2, 0, steps - 1)
    act_in = pl.BlockSpec((1, tt, PAIR), lambda s: tile_index(t_in(s)))
    act_out = pl.BlockSpec((1, tt, PAIR), lambda s: tile_index(t_out(s)))
    return pl.pallas_call(
        functools.partial(_wkv_kernel, n_chunks, n_tiles),
        grid=(steps + 2,),
        in_specs=[act_in] * 6 + [pl.BlockSpec((8, PAIR), lambda s: (0, tile_index(t_in(s))[2])),
                                 pl.BlockSpec((8, PAIR), lambda s: (0, tile_index(t_out(s))[2])),
                                 pl.BlockSpec(ones.shape, lambda s: (0, 0))],
        out_specs=act_out,
        out_shape=jax.ShapeDtypeStruct((B, T, C), F32),
        scratch_shapes=[pltpu.VMEM((PAIR, PAIR), F32),
                        pltpu.VMEM((2, 6, tt, PAIR), F32),
                        pltpu.VMEM((3, tt, PAIR), F32),
                        pltpu.VMEM((2, n_chunks, PAIR + CHUNK, PAIR), BF16),
                        pltpu.VMEM((2, n_chunks, PAIR, PAIR), F32),
                        pltpu.VMEM((2, tt, PAIR), F32)],
        compiler_params=pltpu.CompilerParams(
            dimension_semantics=("arbitrary",), vmem_limit_bytes=VMEM_LIMIT),
        name="wkv",
    )(r, lw, k, v, av, kk, pvec, pvec, ones)


def _post_mlp_kernel(has_gate, has_kv, has_q, *refs):
    refs = list(refs)
    x_ref, y_ref = refs[:2]
    del refs[:2]
    g_ref = refs.pop(0) if has_gate else None
    wo_ref, ln_ref, w1_ref, w2_ref = refs[:4]
    del refs[:4]
    if has_kv or has_q:
        ones_ref = refs.pop(0)
    if has_kv:
        kv_ln_ref, wkv_ref, kg_ref = refs[:3]
        del refs[:3]
    if has_q:
        q_ln_ref, wq_ref, qg_ref = refs[:3]
        del refs[:3]
    o_ref = refs.pop(0)
    if has_kv:
        k_ref, v_ref = refs[:2]
        del refs[:2]
    if has_q:
        q_ref = refs.pop(0)

    y = y_ref[...]
    if has_gate:
        y = y * g_ref[...]
    xn = x_ref[...] + _dot(y.astype(BF16), wo_ref[...])
    hid = _dot(_rms(xn, ln_ref[...]).astype(BF16), w1_ref[...])
    hid = jnp.square(jnp.maximum(hid, 0.0)).astype(BF16)
    out = xn + _dot(hid, w2_ref[...])
    o_ref[...] = out

    if has_kv:
        kv = _dot(_rms(out, kv_ln_ref[...]).astype(BF16), wkv_ref[...])
        half = kv.shape[1] // 2
        k = kv[:, :half]
        ms = _group_sum(k * k, ones_ref[...]) * (1.0 / HEAD)
        k_ref[...] = (k * lax.rsqrt(ms + RMS_EPS) * kg_ref[...]).astype(BF16)
        v_ref[...] = kv[:, half:].astype(BF16)
    if has_q:
        q = _dot(_rms(out, q_ln_ref[...]).astype(BF16), wq_ref[...])
        ms = _group_sum(q * q, ones_ref[...]) * (1.0 / HEAD)
        q_ref[...] = (q * lax.rsqrt(ms + RMS_EPS) * qg_ref[...] * (HEAD ** -0.5 * LOG2E)).astype(BF16)


def _post_mlp(x, y, g, wo, ln, w1, w2, tm, ones=None, kv=None, q=None):
    M, C = x.shape
    act = pl.BlockSpec((tm, C), lambda i: (i, 0))
    has_gate, has_kv, has_q = g is not None, kv is not None, q is not None
    acts = [x, y] + ([g] if has_gate else [])
    consts = [wo, ln, w1, w2] + ([ones] if has_kv or has_q else []) + list(kv or ()) + list(q or ())
    out_shape = [jax.ShapeDtypeStruct((M, C), F32)]
    out_specs = [act]
    if has_kv:
        n = kv[1].shape[1] // 2
        out_shape += [jax.ShapeDtypeStruct((M, n), BF16)] * 2
        out_specs += [pl.BlockSpec((tm, n), lambda i: (i, 0))] * 2
    if has_q:
        n = q[1].shape[1]
        out_shape.append(jax.ShapeDtypeStruct((M, n), BF16))
        out_specs.append(pl.BlockSpec((tm, n), lambda i: (i, 0)))
    return pl.pallas_call(
        functools.partial(_post_mlp_kernel, has_gate, has_kv, has_q),
        grid=(M // tm,),
        in_specs=[act] * len(acts) + [_const_spec(a.shape) for a in consts],
        out_specs=out_specs,
        out_shape=out_shape,
        compiler_params=pltpu.CompilerParams(
            dimension_semantics=("parallel",), vmem_limit_bytes=VMEM_LIMIT),
        name="post_mlp",
    )(*acts, *consts)


def _swa_bias_init(n_q_heads, slopes_ref, bias_ref):
    W = WINDOW
    qi = lax.broadcasted_iota(jnp.int32, (W, 2 * W), 0)
    kj = lax.broadcasted_iota(jnp.int32, (W, 2 * W), 1)
    dist = qi + W - kj
    in_window = (dist >= 0) & (dist < WINDOW)
    in_window_cur = in_window & (kj >= W)
    distf = dist.astype(F32)
    for head in range(n_q_heads):
        alibi = -(slopes_ref[head] * LOG2E) * distf
        bias_ref[0, head] = jnp.where(in_window, alibi, NEG_BIG)
        bias_ref[1, head] = jnp.where(in_window_cur, alibi, NEG_BIG)


def _swa_tile(n_q_heads, first_in_row, sinks_ref, q_ref, k_refs, v_refs, bias_ref, attn_ref):
    W = WINDOW
    group = n_q_heads // N_KV
    pairs = group // 2
    lane = lax.broadcasted_iota(jnp.int32, (W, PAIR), 1)
    head0 = lane < HEAD
    lane2 = lax.broadcasted_iota(jnp.int32, (2 * W, PAIR), 1)
    head0_kv = lane2 < HEAD
    zero = jnp.zeros((), BF16)

    for qb in range(q_ref.shape[0] // W):
        rows = slice(qb * W, (qb + 1) * W)
        plane = jnp.where(first_in_row, 1, 0) if qb == 0 else 0
        for h in range(N_KV):
            ks = slice(h * PAIR, (h + 1) * PAIR)
            k2 = jnp.concatenate([k_refs[qb][:, ks], k_refs[qb + 1][:, ks]], axis=0)
            v2 = jnp.concatenate([v_refs[qb][:, ks], v_refs[qb + 1][:, ks]], axis=0)
            q_tiles = [q_ref[rows, (h * pairs + j) * PAIR:(h * pairs + j + 1) * PAIR] for j in range(pairs)]
            lhs = jnp.concatenate([jnp.where(head0, t, zero) for t in q_tiles]
                                  + [jnp.where(head0, zero, t) for t in q_tiles], axis=0)
            s_all = _dot_nt(lhs, k2)
            yield
            probs, denoms = [], []
            for i in range(group):
                head = h * group + 2 * (i % pairs) + i // pairs
                s = s_all[i * W:(i + 1) * W, :] + bias_ref[plane, head]
                sink = sinks_ref[head] * LOG2E
                mx = jnp.maximum(jnp.max(s, axis=-1, keepdims=True), sink)
                p = jnp.exp2(s - mx)
                denoms.append(jnp.sum(p, axis=-1, keepdims=True) + jnp.exp2(sink - mx))
                probs.append(p.astype(BF16))
            p_first = jnp.concatenate(probs[:pairs], axis=0)
            p_second = jnp.concatenate(probs[pairs:], axis=0)
            v_stack = jnp.concatenate([jnp.where(head0_kv, v2, zero), jnp.where(head0_kv, zero, v2)], axis=0)
            o = _dot(jnp.concatenate([p_first, p_second], axis=1), v_stack)
            for j in range(pairs):
                den = jnp.where(head0, denoms[j], denoms[pairs + j])
                col = (h * pairs + j) * PAIR
                attn_ref[rows, col:col + PAIR] = (o[j * W:(j + 1) * W, :] / den).astype(attn_ref.dtype)


_MLP_SPLIT = 4


def _mlp_stages(x_ref, attn_ref, wo_ref, ln_ref, w1_ref, w2_ref, result):
    xn = x_ref[...] + _dot(attn_ref[...], wo_ref[...])
    h = _rms(xn, ln_ref[...]).astype(BF16)
    yield
    step = w1_ref.shape[1] // _MLP_SPLIT
    hids = []
    for c in range(_MLP_SPLIT):
        hid = _dot(h, w1_ref[:, c * step:(c + 1) * step])
        hids.append(jnp.square(jnp.maximum(hid, 0.0)).astype(BF16))
        if c % 2 == 1:
            yield
    acc = xn
    for c in range(_MLP_SPLIT):
        acc = acc + _dot(hids[c], w2_ref[c * step:(c + 1) * step, :])
        if c % 2 == 1 and c + 1 < _MLP_SPLIT:
            yield
    result.append(acc)


def _swa_mlp_kernel(has_q, n_q_heads, tiles_per_row, slopes_ref, sinks_ref, *refs):
    refs = list(refs)
    x_ref, q_ref, kp_ref, k0_ref, k1_ref, vp_ref, v0_ref, v1_ref, wo_ref, ln_ref, w1_ref, w2_ref = refs[:12]
    del refs[:12]
    if has_q:
        ones_ref, q_ln_ref, wq_ref, qg_ref = refs[:4]
        del refs[:4]
    o_ref = refs.pop(0)
    qn_ref = refs.pop(0) if has_q else None
    attn_ref, bias_ref = refs
    s = pl.program_id(0)

    @pl.when(s == 0)
    def _():
        attn_ref[...] = jnp.zeros_like(attn_ref)
        _swa_bias_init(n_q_heads, slopes_ref, bias_ref)

    cur = s % 2
    first_in_row = s % tiles_per_row == 0
    result = []
    swa = _swa_tile(n_q_heads, first_in_row, sinks_ref, q_ref, [kp_ref, k0_ref, k1_ref],
                    [vp_ref, v0_ref, v1_ref], bias_ref, attn_ref.at[cur])
    mlp = _mlp_stages(x_ref, attn_ref.at[1 - cur], wo_ref, ln_ref, w1_ref, w2_ref, result)
    for _ in swa:
        next(mlp, None)
    for _ in mlp:
        pass
    out = result[0]
    o_ref[...] = out
    if has_q:
        q = _dot(_rms(out, q_ln_ref[...]).astype(BF16), wq_ref[...])
        ms = _group_sum(q * q, ones_ref[...]) * (1.0 / HEAD)
        qn_ref[...] = (q * lax.rsqrt(ms + RMS_EPS) * qg_ref[...] * (HEAD ** -0.5 * LOG2E)).astype(BF16)


def _swa_mlp(x, q, k2, v2, slopes, sinks, wo, ln, w1, w2, tm, tokens_per_row, ones=None, q_next=None):
    M, C = x.shape
    W = WINDOW
    steps = M // tm
    blocks = tm // W
    assert blocks == 2 and tokens_per_row % tm == 0
    has_q = q_next is not None
    t_att = lambda s: jnp.minimum(s, steps - 1)
    t_mlp = lambda s: jnp.maximum(s - 1, 0)
    kv_spec = lambda off: pl.BlockSpec(
        (W, k2.shape[1]), lambda s, *_: (jnp.maximum(t_att(s) * blocks + off, 0), 0))
    act_mlp = pl.BlockSpec((tm, C), lambda s, *_: (t_mlp(s), 0))
    consts = [wo, ln, w1, w2] + ([ones] + list(q_next) if has_q else [])
    out_shape = [jax.ShapeDtypeStruct((M, C), F32)]
    out_specs = [act_mlp]
    if has_q:
        out_shape.append(jax.ShapeDtypeStruct((M, q_next[1].shape[1]), BF16))
        out_specs.append(pl.BlockSpec((tm, q_next[1].shape[1]), lambda s, *_: (t_mlp(s), 0)))
    return pl.pallas_call(
        functools.partial(_swa_mlp_kernel, has_q, C // HEAD, tokens_per_row // tm),
        grid_spec=pltpu.PrefetchScalarGridSpec(
            num_scalar_prefetch=2,
            grid=(steps + 1,),
            in_specs=[act_mlp, pl.BlockSpec((tm, C), lambda s, *_: (t_att(s), 0)),
                      kv_spec(-1), kv_spec(0), kv_spec(1), kv_spec(-1), kv_spec(0), kv_spec(1)]
                     + [_const_spec(a.shape) for a in consts],
            out_specs=out_specs,
            scratch_shapes=[pltpu.VMEM((2, tm, C), BF16),
                            pltpu.VMEM((2, C // HEAD, W, 2 * W), F32)],
        ),
        out_shape=out_shape,
        compiler_params=pltpu.CompilerParams(
            dimension_semantics=("arbitrary",), vmem_limit_bytes=VMEM_LIMIT),
        name="swa_mlp",
    )(slopes, sinks, x, q, k2, k2, k2, v2, v2, v2, *consts)


def _pad_cols(w, n):
    return jnp.pad(w, ((0, 0), (0, n - w.shape[1])))


def _pad_rows(w, n):
    return jnp.pad(w, ((0, n - w.shape[0]), (0, 0)))


def _round_up(n, m):
    return (n + m - 1) // m * m


def _lora(w_in, w_out):
    n = _round_up(w_in.shape[1], 128)
    return _pad_cols(w_in, n).astype(BF16), _pad_rows(w_out, n).astype(BF16)


def _dup_heads(w):
    c, n = w.shape
    w = w.reshape(c, n // HEAD, 1, HEAD)
    return jnp.broadcast_to(w, (c, n // HEAD, 2, HEAD)).reshape(c, 2 * n)


def _block_ones(n):
    i = jnp.arange(n) // HEAD
    return (i[:, None] == i[None, :]).astype(BF16)


def kernel(x, ln_mix, ln_mlp, mlp_w1, mlp_w2, a_mu, a_w_rkv, a_w0, a_w1, a_w2, a_a0, a_a1, a_a2, a_g1, a_g2, a_k_k, a_k_a, a_r_k, a_gn_g, a_gn_b, a_wo, a_v0, a_v1, a_v2, kv_norm, w_kv, k_gain, b_wq, b_q_gain, b_sinks, b_wo):
    B, T, C = x.shape
    M = B * T
    n_a = a_mu.shape[0]
    n_b = b_wq.shape[0]
    n_heads = C // HEAD
    tm = 256
    tm_pre = 512
    tt = 1024
    ones256 = _block_ones(256)
    ones128 = _block_ones(PAIR)
    row = lambda v: v.reshape(1, -1).astype(F32)
    slopes = jnp.exp2(-8.0 * jnp.arange(1, n_heads + 1, dtype=F32) / n_heads)

    def q_params(i):
        j = i - n_a
        return row(ln_mix[i]), b_wq[j].astype(BF16), jnp.tile(row(b_q_gain[j]), (1, n_heads))

    def mlp(i, x, y, g, wo):
        kv = q = None
        if i == n_a - 1:
            kv = (row(kv_norm), _dup_heads(w_kv).astype(BF16), jnp.tile(row(k_gain), (1, 2 * N_KV)))
        if n_a - 1 <= i < n_a + n_b - 1:
            q = q_params(i + 1)
        return _post_mlp(x.reshape(M, C), y.reshape(M, C), None if g is None else g.reshape(M, C),
                         wo.astype(BF16), row(ln_mlp[i]), mlp_w1[i].astype(BF16), mlp_w2[i].astype(BF16),
                         512, ones256, kv, q)

    v_first = None
    k2 = v2 = q = None
    for i in range(n_a + n_b):
        if i < n_a:
            j = i
            zeros = jnp.zeros((1, C), F32)
            vecs = jnp.concatenate(
                [a_mu[j], row(ln_mix[i]), row(a_w0[j]), row(a_a0[j]),
                 row(a_v0[j - 1]) if j > 0 else zeros, row(a_k_k[j]), row(a_k_a[j])]
                + [zeros] * 4, axis=0)
            loras = list(_lora(a_w1[j], a_w2[j]) + _lora(a_a1[j], a_a2[j]) + _lora(a_g1[j], a_g2[j]))
            if j > 0:
                loras += list(_lora(a_v1[j - 1], a_v2[j - 1]))
            r, lw, k, v, av, kk, g = _rwkv_pre(
                x, v_first if j > 0 else None, vecs, a_w_rkv[j].astype(BF16), loras, ones256, tm_pre)
            if j == 0:
                v_first = v
            pvec = jnp.concatenate([row(a_r_k[j]), row(a_gn_g[j]), row(a_gn_b[j])]
                                   + [jnp.zeros((1, C), F32)] * 5, axis=0)
            y = _wkv(r, lw, k, v, av, kk, pvec, ones128, tt)
            outs = mlp(i, x, y, g, a_wo[j])
        else:
            outs = _swa_mlp(x.reshape(M, C), q, k2, v2, slopes, b_sinks[i - n_a].astype(F32),
                            b_wo[i - n_a].astype(BF16), row(ln_mlp[i]), mlp_w1[i].astype(BF16),
                            mlp_w2[i].astype(BF16), tm, T, ones256, q_params(i + 1) if i + 1 < n_a + n_b else None)
        outs = list(outs)
        x = outs.pop(0).reshape(B, T, C)
        if i == n_a - 1:
            k2 = outs.pop(0)
            v2 = outs.pop(0)
        if outs:
            q = outs.pop(0)
    return x
```

```python
import functools
import math

import jax
import jax.numpy as jnp
from jax import lax
from jax.experimental import pallas as pl
from jax.experimental.pallas import tpu as pltpu

F32 = jnp.float32
BF16 = jnp.bfloat16

HEAD = 64
PAIR = 2 * HEAD
N_KV = 2
WINDOW = 128
CHUNK = 64
GN_EPS = 64e-5
RMS_EPS = 1e-6
NEG_BIG = -1e30
LOG2E = math.log2(math.e)
VMEM_LIMIT = 56 * 1024 * 1024

NT_DIMS = (((1,), (1,)), ((), ()))


def _dot(a, b):
    return jnp.dot(a, b, preferred_element_type=F32)


def _dot_nt(a, b):
    return lax.dot_general(a, b, NT_DIMS, preferred_element_type=F32)


def _rms(x, g):
    return x * lax.rsqrt(jnp.mean(x * x, axis=-1, keepdims=True) + RMS_EPS) * g


def _group_sum(x, ones):
    w = ones.shape[0]
    parts = [_dot(x[:, j:j + w].astype(BF16), ones) for j in range(0, x.shape[1], w)]
    return parts[0] if len(parts) == 1 else jnp.concatenate(parts, axis=1)


def _sigmoid(x):
    return 0.5 * jnp.tanh(0.5 * x) + 0.5


def _store_pairs(ref, x):
    for p in range(x.shape[1] // PAIR):
        ref[0, p] = x[:, p * PAIR:(p + 1) * PAIR]


def _load_pairs(ref):
    return jnp.concatenate([ref[0, p] for p in range(ref.shape[1])], axis=1)


def _const_spec(shape):
    nd = len(shape)
    return pl.BlockSpec(shape, lambda *_: (0,) * nd, pipeline_mode=pl.Buffered(1))


_V_MU, _V_LN, _V_W0, _V_A0, _V_V0, _V_KK, _V_KA = 0, 6, 7, 8, 9, 10, 11


def _rwkv_pre_kernel(has_vres, tm, *refs):
    if has_vres:
        (x_ref, vf_ref, vec_ref, wrkv_ref, w1_ref, w2_ref, a1_ref, a2_ref, g1_ref, g2_ref,
         v1_ref, v2_ref, ones_ref,
         r_ref, lw_ref, k_ref, v_ref, av_ref, kk_ref, g_ref, carry_ref) = refs
    else:
        (x_ref, vec_ref, wrkv_ref, w1_ref, w2_ref, a1_ref, a2_ref, g1_ref, g2_ref, ones_ref,
         r_ref, lw_ref, k_ref, v_ref, av_ref, kk_ref, g_ref, carry_ref) = refs

    @pl.when(pl.program_id(1) == 0)
    def _():
        carry_ref[...] = jnp.zeros_like(carry_ref)

    def vec(i):
        return vec_ref[i:i + 1, :]

    h = _rms(x_ref[0], vec(_V_LN))
    row = lax.broadcasted_iota(jnp.int32, h.shape, 0)
    prev = jnp.where(row == 0, carry_ref[0:1, :], pltpu.roll(h, 1, axis=0))
    carry_ref[0:1, :] = h[tm - 1:tm, :]
    xx = prev - h

    def mix(i):
        return (h + xx * vec(_V_MU + i)).astype(BF16)

    r = _dot(mix(0), wrkv_ref[0])
    k = _dot(mix(1), wrkv_ref[1])
    xv = mix(2)
    v = _dot(xv, wrkv_ref[2])

    wl = vec(_V_W0) + _dot(jnp.tanh(_dot(mix(3), w1_ref[...])).astype(BF16), w2_ref[...])
    _store_pairs(lw_ref, -math.exp(-0.5) * _sigmoid(wl))

    if has_vres:
        gate = _sigmoid(vec(_V_V0) + _dot(_dot(xv, v1_ref[...]).astype(BF16), v2_ref[...]))
        v = v + (_load_pairs(vf_ref) - v) * gate
    a = _sigmoid(vec(_V_A0) + _dot(_dot(mix(4), a1_ref[...]).astype(BF16), a2_ref[...]))
    _store_pairs(g_ref, _dot(_sigmoid(_dot(mix(5), g1_ref[...])).astype(BF16), g2_ref[...]))

    kk = k * vec(_V_KK)
    _store_pairs(kk_ref, kk * jnp.minimum(lax.rsqrt(_group_sum(kk * kk, ones_ref[...])), 1e12))
    _store_pairs(k_ref, k * (1.0 + (a - 1.0) * vec(_V_KA)))
    _store_pairs(r_ref, r)
    _store_pairs(v_ref, v)
    _store_pairs(av_ref, a)


def _rwkv_pre(x, v_first, vecs, wrkv, loras, ones, tm):
    B, T, C = x.shape
    has_vres = v_first is not None
    act = pl.BlockSpec((1, tm, C), lambda b, t: (b, t, 0))
    pairs = pl.BlockSpec((1, C // PAIR, tm, PAIR), lambda b, t: (b, 0, t, 0))
    ins = [x] + ([v_first] if has_vres else []) + [vecs, wrkv] + list(loras) + [ones]
    in_specs = [act] + ([pairs] if has_vres else []) + [_const_spec(a.shape) for a in ins[(2 if has_vres else 1):]]
    out = jax.ShapeDtypeStruct((B, C // PAIR, T, PAIR), F32)
    return pl.pallas_call(
        functools.partial(_rwkv_pre_kernel, has_vres, tm),
        grid=(B, T // tm),
        in_specs=in_specs,
        out_specs=[pairs] * 7,
        out_shape=[out] * 7,
        scratch_shapes=[pltpu.VMEM((8, C), F32)],
        compiler_params=pltpu.CompilerParams(
            dimension_semantics=("parallel", "arbitrary"), vmem_limit_bytes=VMEM_LIMIT),
        name="rwkv_pre",
    )(*ins)


def _chunk_cumsum(x):
    pos = lax.broadcasted_iota(jnp.int32, x.shape, 0) & (CHUNK - 1)
    s = 1
    while s < CHUNK:
        x = x + jnp.where(pos >= s, pltpu.roll(x, s, axis=0), 0.0)
        s *= 2
    return x


def _each(f, *lists):
    return [f(*xs) for xs in zip(*lists)]


_WKV_PRE_STAGES = 8
_T_AT, _T_RT, _T_KT, _T_BT, _T_V, _T_EC = range(6)


def _wkv_phase0(n_blocks, never, r_ref, lw_ref, k_ref, v_ref, av_ref, kk_ref, rk, ones, tilde_ref, bv_ref):
    rows = r_ref.shape[2] // n_blocks
    for i in range(n_blocks):
        after = yield
        sl = slice(i * rows, (i + 1) * rows)
        r, lw, k, v, kk = r_ref[0, 0, sl], lw_ref[0, 0, sl], k_ref[0, 0, sl], v_ref[0, 0, sl], kk_ref[0, 0, sl]
        if after is not None:
            lw = lw + jnp.where(never, jnp.concatenate([after] * (rows // after.shape[0]), axis=0), 0.0)
        c = _chunk_cumsum(lw)
        ec = jnp.exp(c)
        enc = jnp.exp(-c)
        tilde_ref[_T_AT, sl] = -kk * jnp.exp(c - lw)
        tilde_ref[_T_RT, sl] = r * ec
        tilde_ref[_T_KT, sl] = k * enc
        tilde_ref[_T_BT, sl] = kk * av_ref[0, 0, sl] * enc
        tilde_ref[_T_V, sl] = v
        tilde_ref[_T_EC, sl] = ec
        bv_ref[sl] = _group_sum(r * k * rk, ones) * v
    yield


def _wkv_chunks_pre(n_chunks, tilde_ref):
    L = CHUNK
    bf = lambda t: t.astype(BF16)
    lane = lax.broadcasted_iota(jnp.int32, (L, PAIR), 1)
    head0 = lane < HEAD
    row = lax.broadcasted_iota(jnp.int32, (L, 2 * L), 0)
    col = lax.broadcasted_iota(jnp.int32, (L, 2 * L), 1)
    src = jnp.where(col >= L, col - L, col)
    strict = src < row
    incl = src <= row
    left = col < L
    eye = jnp.where(src == row, 1.0, 0.0)
    cis = list(range(n_chunks))
    mid = n_chunks // 2 - 1

    def plane(which):
        vals = [tilde_ref[which, ci * L:(ci + 1) * L, :] for ci in cis]
        return lambda ci: vals[ci]

    at, rt, kt, bt, v = plane(_T_AT), plane(_T_RT), plane(_T_KT), plane(_T_BT), plane(_T_V)
    p_last_vals = [tilde_ref[_T_EC, (ci + 1) * L - 1:(ci + 1) * L, :] for ci in cis]
    p_last = lambda ci: p_last_vals[ci]

    def bd_rows(x):
        return jnp.concatenate([jnp.where(head0, x, 0.0), jnp.where(head0, 0.0, x)], axis=0)

    def bd_cols(x):
        return jnp.concatenate([jnp.where(left, x, 0.0), jnp.where(left, 0.0, x)], axis=0)

    def bd_rows2(x):
        return jnp.concatenate([bd_rows(x[:, :PAIR]), bd_rows(x[:, PAIR:])], axis=1)

    sc = [_dot_nt(bf(jnp.concatenate([at(ci), rt(ci)], axis=0)),
                  bf(jnp.concatenate([bd_rows(bt(ci)), bd_rows(kt(ci))], axis=0))) for ci in cis]
    yield sc[mid][:L, :PAIR], sc[-1][:L, :PAIR]
    a_ab = [jnp.where(strict, s[:L, :2 * L], 0.0) for s in sc]
    akv = [_dot(bf(jnp.where(strict, sc[ci][:L, 2 * L:], 0.0)), bf(bd_rows(v(ci)))) for ci in cis]
    p = _each(lambda a: _dot(bf(a), bf(bd_cols(a))), a_ab)
    t = [eye + a for a in a_ab]
    yield p[mid], p[-1]
    s = 2
    while 2 * s < L:
        o = _each(lambda pp, tt: _dot(bf(pp), bf(jnp.concatenate([bd_cols(tt), bd_cols(pp)], axis=1))), p, t)
        t = _each(lambda tt, oo: tt + oo[:, :2 * L], t, o)
        p = [oo[:, 2 * L:] for oo in o]
        s *= 2
        yield p[mid], p[-1]
    t = _each(lambda pp, tt: tt + _dot(bf(pp), bf(bd_cols(tt))), p, t)
    yield t[mid], t[-1]
    z = [_dot(bf(t[ci]), bf(bd_rows2(jnp.concatenate([at(ci), akv[ci]], axis=1)))) for ci in cis]
    yield z[mid][:, :PAIR], z[-1][:, :PAIR]
    zeros2 = jnp.zeros((2 * L, PAIR), F32)
    o2 = [_dot(
        bf(jnp.concatenate([jnp.where(incl, sc[ci][L:, :2 * L], 0.0), jnp.where(incl, sc[ci][L:, 2 * L:], 0.0)], axis=1)),
        bf(jnp.concatenate([bd_rows2(z[ci]), jnp.concatenate([zeros2, bd_rows(v(ci))], axis=1)], axis=0)))
        for ci in cis]
    rh = [rt(ci) + o2[ci][:, :PAIR] for ci in cis]
    yh = [oo[:, PAIR:] for oo in o2]
    yield yh[mid], yh[-1]
    zeros1 = jnp.zeros((L, PAIR), F32)
    o3 = [_dot(
        bf(jnp.concatenate([bt(ci) * p_last(ci), kt(ci) * p_last(ci)], axis=0).T),
        bf(jnp.concatenate([z[ci], jnp.concatenate([zeros1, v(ci)], axis=1)], axis=0))) for ci in cis]
    r128 = lax.broadcasted_iota(jnp.int32, (PAIR, PAIR), 0)
    c128 = lax.broadcasted_iota(jnp.int32, (PAIR, PAIR), 1)
    same_head = (r128 < HEAD) == (c128 < HEAD)
    m = [jnp.where(same_head, o3[ci][:, :PAIR], 0.0) + jnp.where(r128 == c128, p_last(ci), 0.0) for ci in cis]
    g = [jnp.where(same_head, oo[:, PAIR:], 0.0) for oo in o3]
    return rh, yh, m, g


def _wkv_chain(n_chunks, first_tile, mrh_ref, g_ref, yh_ref, h_ref, ys):
    L = CHUNK
    h = jnp.where(first_tile, 0.0, h_ref[...])
    for ci in range(n_chunks):
        o = _dot(mrh_ref[ci], h.astype(BF16))
        h = o[:PAIR] + g_ref[ci]
        ys.append(o[PAIR:] + yh_ref[ci * L:(ci + 1) * L, :])
        yield
    h_ref[...] = h


def _wkv_kernel(n_chunks, tiles_per_seq,
                r_ref, lw_ref, k_ref, v_ref, av_ref, kk_ref, pvec_in_ref, pvec_out_ref, ones_ref,
                y_ref, h_ref, tilde_ref, bv_ref, mrh_ref, g_ref, yh_ref):
    s = pl.program_id(0)

    @pl.when(s == 0)
    def _():
        for ref in (h_ref, tilde_ref, bv_ref, mrh_ref, g_ref, yh_ref):
            ref[...] = jnp.zeros_like(ref)

    p0 = s % 2
    p1 = 1 - p0
    b_in = s % 3
    b_out = (s + 1) % 3
    ones = ones_ref[...]

    ys = []
    first_tile = (s - 2) % tiles_per_seq == 0
    chain = _wkv_chain(n_chunks, first_tile, mrh_ref.at[p0], g_ref.at[p0], yh_ref.at[p0], h_ref, ys)
    pre = _wkv_chunks_pre(n_chunks, tilde_ref.at[p1])
    phase0 = _wkv_phase0(2 * _WKV_PRE_STAGES, s < 0, r_ref, lw_ref, k_ref, v_ref, av_ref, kk_ref,
                         pvec_in_ref[0:1, :], ones, tilde_ref.at[p0], bv_ref.at[b_in])
    next(phase0)
    chain_steps_per_stage = -(-n_chunks // _WKV_PRE_STAGES)
    blocks_left = 2 * _WKV_PRE_STAGES
    while True:
        try:
            deps = next(pre)
        except StopIteration as stop:
            rh, yh, m, g = stop.value
            break
        for _ in range(chain_steps_per_stage):
            next(chain, None)
        for dep in deps:
            if blocks_left:
                phase0.send(dep)
                blocks_left -= 1
    for _ in chain:
        pass
    assert blocks_left == 0

    y = jnp.concatenate(ys, axis=0)
    inv_n = 1.0 / HEAD
    d = y - _group_sum(y, ones) * inv_n
    var = _group_sum(d * d, ones) * inv_n
    y_ref[0, 0] = d * lax.rsqrt(var + GN_EPS) * pvec_out_ref[1:2, :] + pvec_out_ref[2:3, :] + bv_ref[b_out]

    for ci in range(n_chunks):
        mrh_ref[p1, ci] = jnp.concatenate([m[ci], rh[ci]], axis=0).astype(BF16)
        g_ref[p1, ci] = g[ci]
    yh_ref[p1] = jnp.concatenate(yh, axis=0)


def _wkv(r, lw, k, v, av, kk, pvec, ones, tt):
    B, n_pairs, T, _ = r.shape
    n_tiles = T // tt
    n_chunks = tt // CHUNK
    steps = B * n_pairs * n_tiles

    def tile_index(i):
        return i // (n_pairs * n_tiles), (i // n_tiles) % n_pairs, i % n_tiles

    t_in = lambda s: jnp.minimum(s, steps - 1)
    t_out = lambda s: jnp.clip(s - 2, 0, steps - 1)
    act_in = pl.BlockSpec((1, 1, tt, PAIR), lambda s: tile_index(t_in(s)) + (0,))
    act_out = pl.BlockSpec((1, 1, tt, PAIR), lambda s: tile_index(t_out(s)) + (0,))
    return pl.pallas_call(
        functools.partial(_wkv_kernel, n_chunks, n_tiles),
        grid=(steps + 2,),
        in_specs=[act_in] * 6 + [pl.BlockSpec((8, PAIR), lambda s: (0, tile_index(t_in(s))[1])),
                                 pl.BlockSpec((8, PAIR), lambda s: (0, tile_index(t_out(s))[1])),
                                 pl.BlockSpec(ones.shape, lambda s: (0, 0))],
        out_specs=act_out,
        out_shape=jax.ShapeDtypeStruct((B, n_pairs, T, PAIR), F32),
        scratch_shapes=[pltpu.VMEM((PAIR, PAIR), F32),
                        pltpu.VMEM((2, 6, tt, PAIR), F32),
                        pltpu.VMEM((3, tt, PAIR), F32),
                        pltpu.VMEM((2, n_chunks, PAIR + CHUNK, PAIR), BF16),
                        pltpu.VMEM((2, n_chunks, PAIR, PAIR), F32),
                        pltpu.VMEM((2, tt, PAIR), F32)],
        compiler_params=pltpu.CompilerParams(
            dimension_semantics=("arbitrary",), vmem_limit_bytes=VMEM_LIMIT),
        name="wkv",
    )(r, lw, k, v, av, kk, pvec, pvec, ones)


def _post_mlp_kernel(has_gate, has_kv, has_q, *refs):
    refs = list(refs)
    x_ref, y_ref = refs[:2]
    del refs[:2]
    g_ref = refs.pop(0) if has_gate else None
    wo_ref, ln_ref, w1_ref, w2_ref = refs[:4]
    del refs[:4]
    if has_kv or has_q:
        ones_ref = refs.pop(0)
    if has_kv:
        kv_ln_ref, wkv_ref, kg_ref = refs[:3]
        del refs[:3]
    if has_q:
        q_ln_ref, wq_ref, qg_ref = refs[:3]
        del refs[:3]
    o_ref = refs.pop(0)
    if has_kv:
        k_ref, v_ref = refs[:2]
        del refs[:2]
    if has_q:
        q_ref = refs.pop(0)

    y = _load_pairs(y_ref)
    if has_gate:
        y = y * _load_pairs(g_ref)
    xn = x_ref[...] + _dot(y.astype(BF16), wo_ref[...])
    hid = _dot(_rms(xn, ln_ref[...]).astype(BF16), w1_ref[...])
    hid = jnp.square(jnp.maximum(hid, 0.0)).astype(BF16)
    out = xn + _dot(hid, w2_ref[...])
    o_ref[...] = out

    if has_kv:
        kv = _dot(_rms(out, kv_ln_ref[...]).astype(BF16), wkv_ref[...])
        half = kv.shape[1] // 2
        k = kv[:, :half]
        ms = _group_sum(k * k, ones_ref[...]) * (1.0 / HEAD)
        k_ref[...] = (k * lax.rsqrt(ms + RMS_EPS) * kg_ref[...]).astype(BF16)
        v_ref[...] = kv[:, half:].astype(BF16)
    if has_q:
        q = _dot(_rms(out, q_ln_ref[...]).astype(BF16), wq_ref[...])
        ms = _group_sum(q * q, ones_ref[...]) * (1.0 / HEAD)
        q_ref[...] = (q * lax.rsqrt(ms + RMS_EPS) * qg_ref[...] * (HEAD ** -0.5 * LOG2E)).astype(BF16)


def _post_mlp(x, y, g, wo, ln, w1, w2, tm, ones=None, kv=None, q=None):
    M, C = x.shape
    tiles_per_row = y.shape[2] // tm
    act = pl.BlockSpec((tm, C), lambda i: (i, 0))
    pairs = pl.BlockSpec((1, C // PAIR, tm, PAIR), lambda i: (i // tiles_per_row, 0, i % tiles_per_row, 0))
    has_gate, has_kv, has_q = g is not None, kv is not None, q is not None
    acts = [x, y] + ([g] if has_gate else [])
    consts = [wo, ln, w1, w2] + ([ones] if has_kv or has_q else []) + list(kv or ()) + list(q or ())
    out_shape = [jax.ShapeDtypeStruct((M, C), F32)]
    out_specs = [act]
    if has_kv:
        n = kv[1].shape[1] // 2
        out_shape += [jax.ShapeDtypeStruct((M, n), BF16)] * 2
        out_specs += [pl.BlockSpec((tm, n), lambda i: (i, 0))] * 2
    if has_q:
        n = q[1].shape[1]
        out_shape.append(jax.ShapeDtypeStruct((M, n), BF16))
        out_specs.append(pl.BlockSpec((tm, n), lambda i: (i, 0)))
    return pl.pallas_call(
        functools.partial(_post_mlp_kernel, has_gate, has_kv, has_q),
        grid=(M // tm,),
        in_specs=[act] + [pairs] * (len(acts) - 1) + [_const_spec(a.shape) for a in consts],
        out_specs=out_specs,
        out_shape=out_shape,
        compiler_params=pltpu.CompilerParams(
            dimension_semantics=("parallel",), vmem_limit_bytes=VMEM_LIMIT),
        name="post_mlp",
    )(*acts, *consts)


def _swa_bias_init(n_q_heads, slopes_ref, bias_ref):
    W = WINDOW
    qi = lax.broadcasted_iota(jnp.int32, (W, 2 * W), 0)
    kj = lax.broadcasted_iota(jnp.int32, (W, 2 * W), 1)
    dist = qi + W - kj
    in_window = (dist >= 0) & (dist < WINDOW)
    in_window_cur = in_window & (kj >= W)
    distf = dist.astype(F32)
    for head in range(n_q_heads):
        alibi = -(slopes_ref[head] * LOG2E) * distf
        bias_ref[0, head] = jnp.where(in_window, alibi, NEG_BIG)
        bias_ref[1, head] = jnp.where(in_window_cur, alibi, NEG_BIG)


def _swa_tile(n_q_heads, first_in_row, sinks_ref, q_ref, k_refs, v_refs, bias_ref, attn_ref):
    W = WINDOW
    group = n_q_heads // N_KV
    pairs = group // 2
    lane = lax.broadcasted_iota(jnp.int32, (W, PAIR), 1)
    head0 = lane < HEAD
    lane2 = lax.broadcasted_iota(jnp.int32, (2 * W, PAIR), 1)
    head0_kv = lane2 < HEAD
    zero = jnp.zeros((), BF16)

    for qb in range(q_ref.shape[0] // W):
        rows = slice(qb * W, (qb + 1) * W)
        plane = jnp.where(first_in_row, 1, 0) if qb == 0 else 0
        for h in range(N_KV):
            ks = slice(h * PAIR, (h + 1) * PAIR)
            k2 = jnp.concatenate([k_refs[qb][:, ks], k_refs[qb + 1][:, ks]], axis=0)
            v2 = jnp.concatenate([v_refs[qb][:, ks], v_refs[qb + 1][:, ks]], axis=0)
            q_tiles = [q_ref[rows, (h * pairs + j) * PAIR:(h * pairs + j + 1) * PAIR] for j in range(pairs)]
            lhs = jnp.concatenate([jnp.where(head0, t, zero) for t in q_tiles]
                                  + [jnp.where(head0, zero, t) for t in q_tiles], axis=0)
            s_all = _dot_nt(lhs, k2)
            yield
            probs, denoms = [], []
            for i in range(group):
                head = h * group + 2 * (i % pairs) + i // pairs
                s = s_all[i * W:(i + 1) * W, :] + bias_ref[plane, head]
                sink = sinks_ref[head] * LOG2E
                mx = jnp.maximum(jnp.max(s, axis=-1, keepdims=True), sink)
                p = jnp.exp2(s - mx)
                denoms.append(jnp.sum(p, axis=-1, keepdims=True) + jnp.exp2(sink - mx))
                probs.append(p.astype(BF16))
            p_first = jnp.concatenate(probs[:pairs], axis=0)
            p_second = jnp.concatenate(probs[pairs:], axis=0)
            v_stack = jnp.concatenate([jnp.where(head0_kv, v2, zero), jnp.where(head0_kv, zero, v2)], axis=0)
            o = _dot(jnp.concatenate([p_first, p_second], axis=1), v_stack)
            for j in range(pairs):
                den = jnp.where(head0, denoms[j], denoms[pairs + j])
                col = (h * pairs + j) * PAIR
                attn_ref[rows, col:col + PAIR] = (o[j * W:(j + 1) * W, :] / den).astype(attn_ref.dtype)


_MLP_SPLIT = 4


def _mlp_stages(x_ref, attn_ref, wo_ref, ln_ref, w1_ref, w2_ref, result):
    xn = x_ref[...] + _dot(attn_ref[...], wo_ref[...])
    h = _rms(xn, ln_ref[...]).astype(BF16)
    yield
    step = w1_ref.shape[1] // _MLP_SPLIT
    hids = []
    for c in range(_MLP_SPLIT):
        hid = _dot(h, w1_ref[:, c * step:(c + 1) * step])
        hids.append(jnp.square(jnp.maximum(hid, 0.0)).astype(BF16))
        if c % 2 == 1:
            yield
    acc = xn
    for c in range(_MLP_SPLIT):
        acc = acc + _dot(hids[c], w2_ref[c * step:(c + 1) * step, :])
        if c % 2 == 1 and c + 1 < _MLP_SPLIT:
            yield
    result.append(acc)


def _swa_mlp_kernel(has_q, n_q_heads, tiles_per_row, slopes_ref, sinks_ref, *refs):
    refs = list(refs)
    x_ref, q_ref, kp_ref, k0_ref, k1_ref, vp_ref, v0_ref, v1_ref, wo_ref, ln_ref, w1_ref, w2_ref = refs[:12]
    del refs[:12]
    if has_q:
        ones_ref, q_ln_ref, wq_ref, qg_ref = refs[:4]
        del refs[:4]
    o_ref = refs.pop(0)
    qn_ref = refs.pop(0) if has_q else None
    attn_ref, bias_ref = refs
    s = pl.program_id(0)

    @pl.when(s == 0)
    def _():
        attn_ref[...] = jnp.zeros_like(attn_ref)
        _swa_bias_init(n_q_heads, slopes_ref, bias_ref)

    cur = s % 2
    first_in_row = s % tiles_per_row == 0
    result = []
    swa = _swa_tile(n_q_heads, first_in_row, sinks_ref, q_ref, [kp_ref, k0_ref, k1_ref],
                    [vp_ref, v0_ref, v1_ref], bias_ref, attn_ref.at[cur])
    mlp = _mlp_stages(x_ref, attn_ref.at[1 - cur], wo_ref, ln_ref, w1_ref, w2_ref, result)
    for _ in swa:
        next(mlp, None)
    for _ in mlp:
        pass
    out = result[0]
    o_ref[...] = out
    if has_q:
        q = _dot(_rms(out, q_ln_ref[...]).astype(BF16), wq_ref[...])
        ms = _group_sum(q * q, ones_ref[...]) * (1.0 / HEAD)
        qn_ref[...] = (q * lax.rsqrt(ms + RMS_EPS) * qg_ref[...] * (HEAD ** -0.5 * LOG2E)).astype(BF16)


def _swa_mlp(x, q, k2, v2, slopes, sinks, wo, ln, w1, w2, tm, tokens_per_row, ones=None, q_next=None):
    M, C = x.shape
    W = WINDOW
    steps = M // tm
    blocks = tm // W
    assert blocks == 2 and tokens_per_row % tm == 0
    has_q = q_next is not None
    t_att = lambda s: jnp.minimum(s, steps - 1)
    t_mlp = lambda s: jnp.maximum(s - 1, 0)
    kv_spec = lambda off: pl.BlockSpec(
        (W, k2.shape[1]), lambda s, *_: (jnp.maximum(t_att(s) * blocks + off, 0), 0))
    act_mlp = pl.BlockSpec((tm, C), lambda s, *_: (t_mlp(s), 0))
    consts = [wo, ln, w1, w2] + ([ones] + list(q_next) if has_q else [])
    out_shape = [jax.ShapeDtypeStruct((M, C), F32)]
    out_specs = [act_mlp]
    if has_q:
        out_shape.append(jax.ShapeDtypeStruct((M, q_next[1].shape[1]), BF16))
        out_specs.append(pl.BlockSpec((tm, q_next[1].shape[1]), lambda s, *_: (t_mlp(s), 0)))
    return pl.pallas_call(
        functools.partial(_swa_mlp_kernel, has_q, C // HEAD, tokens_per_row // tm),
        grid_spec=pltpu.PrefetchScalarGridSpec(
            num_scalar_prefetch=2,
            grid=(steps + 1,),
            in_specs=[act_mlp, pl.BlockSpec((tm, C), lambda s, *_: (t_att(s), 0)),
                      kv_spec(-1), kv_spec(0), kv_spec(1), kv_spec(-1), kv_spec(0), kv_spec(1)]
                     + [_const_spec(a.shape) for a in consts],
            out_specs=out_specs,
            scratch_shapes=[pltpu.VMEM((2, tm, C), BF16),
                            pltpu.VMEM((2, C // HEAD, W, 2 * W), F32)],
        ),
        out_shape=out_shape,
        compiler_params=pltpu.CompilerParams(
            dimension_semantics=("arbitrary",), vmem_limit_bytes=VMEM_LIMIT),
        name="swa_mlp",
    )(slopes, sinks, x, q, k2, k2, k2, v2, v2, v2, *consts)


def _pad_cols(w, n):
    return jnp.pad(w, ((0, 0), (0, n - w.shape[1])))


def _pad_rows(w, n):
    return jnp.pad(w, ((0, n - w.shape[0]), (0, 0)))


def _round_up(n, m):
    return (n + m - 1) // m * m


def _lora(w_in, w_out):
    n = _round_up(w_in.shape[1], 128)
    return _pad_cols(w_in, n).astype(BF16), _pad_rows(w_out, n).astype(BF16)


def _dup_heads(w):
    c, n = w.shape
    w = w.reshape(c, n // HEAD, 1, HEAD)
    return jnp.broadcast_to(w, (c, n // HEAD, 2, HEAD)).reshape(c, 2 * n)


def _block_ones(n):
    i = jnp.arange(n) // HEAD
    return (i[:, None] == i[None, :]).astype(BF16)


def kernel(x, ln_mix, ln_mlp, mlp_w1, mlp_w2, a_mu, a_w_rkv, a_w0, a_w1, a_w2, a_a0, a_a1, a_a2, a_g1, a_g2, a_k_k, a_k_a, a_r_k, a_gn_g, a_gn_b, a_wo, a_v0, a_v1, a_v2, kv_norm, w_kv, k_gain, b_wq, b_q_gain, b_sinks, b_wo):
    B, T, C = x.shape
    M = B * T
    n_a = a_mu.shape[0]
    n_b = b_wq.shape[0]
    n_heads = C // HEAD
    tm = 256
    tm_pre = 512
    tt = 1024
    ones256 = _block_ones(256)
    ones128 = _block_ones(PAIR)
    row = lambda v: v.reshape(1, -1).astype(F32)
    slopes = jnp.exp2(-8.0 * jnp.arange(1, n_heads + 1, dtype=F32) / n_heads)

    def q_params(i):
        j = i - n_a
        return row(ln_mix[i]), b_wq[j].astype(BF16), jnp.tile(row(b_q_gain[j]), (1, n_heads))

    def mlp(i, x, y, g, wo):
        kv = q = None
        if i == n_a - 1:
            kv = (row(kv_norm), _dup_heads(w_kv).astype(BF16), jnp.tile(row(k_gain), (1, 2 * N_KV)))
        if n_a - 1 <= i < n_a + n_b - 1:
            q = q_params(i + 1)
        return _post_mlp(x.reshape(M, C), y, g, wo.astype(BF16), row(ln_mlp[i]), mlp_w1[i].astype(BF16), mlp_w2[i].astype(BF16),
                         512, ones256, kv, q)

    v_first = None
    k2 = v2 = q = None
    for i in range(n_a + n_b):
        if i < n_a:
            j = i
            zeros = jnp.zeros((1, C), F32)
            vecs = jnp.concatenate(
                [a_mu[j], row(ln_mix[i]), row(a_w0[j]), row(a_a0[j]),
                 row(a_v0[j - 1]) if j > 0 else zeros, row(a_k_k[j]), row(a_k_a[j])]
                + [zeros] * 4, axis=0)
            loras = list(_lora(a_w1[j], a_w2[j]) + _lora(a_a1[j], a_a2[j]) + _lora(a_g1[j], a_g2[j]))
            if j > 0:
                loras += list(_lora(a_v1[j - 1], a_v2[j - 1]))
            r, lw, k, v, av, kk, g = _rwkv_pre(
                x, v_first if j > 0 else None, vecs, a_w_rkv[j].astype(BF16), loras, ones256, tm_pre)
            if j == 0:
                v_first = v
            pvec = jnp.concatenate([row(a_r_k[j]), row(a_gn_g[j]), row(a_gn_b[j])]
                                   + [jnp.zeros((1, C), F32)] * 5, axis=0)
            y = _wkv(r, lw, k, v, av, kk, pvec, ones128, tt)
            outs = mlp(i, x, y, g, a_wo[j])
        else:
            outs = _swa_mlp(x.reshape(M, C), q, k2, v2, slopes, b_sinks[i - n_a].astype(F32),
                            b_wo[i - n_a].astype(BF16), row(ln_mlp[i]), mlp_w1[i].astype(BF16),
                            mlp_w2[i].astype(BF16), tm, T, ones256, q_params(i + 1) if i + 1 < n_a + n_b else None)
        outs = list(outs)
        x = outs.pop(0).reshape(B, T, C)
        if i == n_a - 1:
            k2 = outs.pop(0)
            v2 = outs.pop(0)
        if outs:
            q = outs.pop(0)
    return x
```

```python
import functools
import math

import jax
import jax.numpy as jnp
from jax import lax
from jax.experimental import pallas as pl
from jax.experimental.pallas import tpu as pltpu

F32 = jnp.float32
BF16 = jnp.bfloat16

HEAD = 64
PAIR = 2 * HEAD
N_KV = 2
WINDOW = 128
CHUNK = 64
GN_EPS = 64e-5
RMS_EPS = 1e-6
NEG_BIG = -1e30
LOG2E = math.log2(math.e)
VMEM_LIMIT = 60 * 1024 * 1024

NT_DIMS = (((1,), (1,)), ((), ()))


def _dot(a, b):
    return jnp.dot(a, b, preferred_element_type=F32)


def _dot_nt(a, b):
    return lax.dot_general(a, b, NT_DIMS, preferred_element_type=F32)


def _rms(x, g):
    return x * lax.rsqrt(jnp.mean(x * x, axis=-1, keepdims=True) + RMS_EPS) * g


def _group_sum(x, ones):
    w = ones.shape[0]
    parts = [_dot(x[:, j:j + w].astype(BF16), ones) for j in range(0, x.shape[1], w)]
    return parts[0] if len(parts) == 1 else jnp.concatenate(parts, axis=1)


def _sigmoid(x):
    return 0.5 * jnp.tanh(0.5 * x) + 0.5


def _store_pairs(ref, x):
    for p in range(x.shape[1] // PAIR):
        ref[0, p] = x[:, p * PAIR:(p + 1) * PAIR]


def _load_pairs(ref):
    return jnp.concatenate([ref[0, p] for p in range(ref.shape[1])], axis=1)


def _const_spec(shape):
    nd = len(shape)
    return pl.BlockSpec(shape, lambda *_: (0,) * nd, pipeline_mode=pl.Buffered(1))


class _Layer:
    def __init__(self, stacked, layer):
        self.stacked, self.layer, self.shape = stacked, layer, stacked.shape[1:]


def _weight_spec(w):
    if isinstance(w, _Layer):
        nd = len(w.shape)
        return pl.BlockSpec((None,) + w.shape, lambda *_: (w.layer,) + (0,) * nd, pipeline_mode=pl.Buffered(1))
    return _const_spec(w.shape)


def _weight_arg(w):
    return w.stacked if isinstance(w, _Layer) else w


_V_MU, _V_LN, _V_W0, _V_A0, _V_V0, _V_KK, _V_KA = 0, 6, 7, 8, 9, 10, 11


def _rwkv_pre_kernel(has_vres, tm, *refs):
    if has_vres:
        (x_ref, vf_ref, vec_ref, wrkv_ref, w1_ref, w2_ref, a1_ref, a2_ref, g1_ref, g2_ref,
         v1_ref, v2_ref, ones_ref,
         r_ref, lw_ref, k_ref, v_ref, av_ref, kk_ref, g_ref, carry_ref) = refs
    else:
        (x_ref, vec_ref, wrkv_ref, w1_ref, w2_ref, a1_ref, a2_ref, g1_ref, g2_ref, ones_ref,
         r_ref, lw_ref, k_ref, v_ref, av_ref, kk_ref, g_ref, carry_ref) = refs

    @pl.when(pl.program_id(1) == 0)
    def _():
        carry_ref[...] = jnp.zeros_like(carry_ref)

    def vec(i):
        return vec_ref[i:i + 1, :]

    h = _rms(x_ref[0], vec(_V_LN))
    row = lax.broadcasted_iota(jnp.int32, h.shape, 0)
    prev = jnp.where(row == 0, carry_ref[0:1, :], pltpu.roll(h, 1, axis=0))
    carry_ref[0:1, :] = h[tm - 1:tm, :]
    xx = prev - h

    def mix(i):
        return (h + xx * vec(_V_MU + i)).astype(BF16)

    r = _dot(mix(0), wrkv_ref[0])
    k = _dot(mix(1), wrkv_ref[1])
    xv = mix(2)
    v = _dot(xv, wrkv_ref[2])

    wl = vec(_V_W0) + _dot(jnp.tanh(_dot(mix(3), w1_ref[...])).astype(BF16), w2_ref[...])
    _store_pairs(lw_ref, -math.exp(-0.5) * _sigmoid(wl))

    if has_vres:
        gate = _sigmoid(vec(_V_V0) + _dot(_dot(xv, v1_ref[...]).astype(BF16), v2_ref[...]))
        v = v + (_load_pairs(vf_ref) - v) * gate
    a = _sigmoid(vec(_V_A0) + _dot(_dot(mix(4), a1_ref[...]).astype(BF16), a2_ref[...]))
    _store_pairs(g_ref, _dot(_sigmoid(_dot(mix(5), g1_ref[...])).astype(BF16), g2_ref[...]))

    kk = k * vec(_V_KK)
    _store_pairs(kk_ref, kk * jnp.minimum(lax.rsqrt(_group_sum(kk * kk, ones_ref[...])), 1e12))
    _store_pairs(k_ref, k * (1.0 + (a - 1.0) * vec(_V_KA)))
    _store_pairs(r_ref, r)
    _store_pairs(v_ref, v)
    _store_pairs(av_ref, a)


def _rwkv_pre(x, v_first, vecs, wrkv, loras, ones, tm):
    B, T, C = x.shape
    has_vres = v_first is not None
    act = pl.BlockSpec((1, tm, C), lambda b, t: (b, t, 0))
    pairs = pl.BlockSpec((1, C // PAIR, tm, PAIR), lambda b, t: (b, 0, t, 0))
    ins = [x] + ([v_first] if has_vres else []) + [vecs, wrkv] + list(loras) + [ones]
    in_specs = [act] + ([pairs] if has_vres else []) + [_const_spec(a.shape) for a in ins[(2 if has_vres else 1):]]
    out = jax.ShapeDtypeStruct((B, C // PAIR, T, PAIR), F32)
    return pl.pallas_call(
        functools.partial(_rwkv_pre_kernel, has_vres, tm),
        grid=(B, T // tm),
        in_specs=in_specs,
        out_specs=[pairs] * 7,
        out_shape=[out] * 7,
        scratch_shapes=[pltpu.VMEM((8, C), F32)],
        compiler_params=pltpu.CompilerParams(
            dimension_semantics=("parallel", "arbitrary"), vmem_limit_bytes=VMEM_LIMIT),
        name="rwkv_pre",
    )(*ins)


def _chunk_cumsum(x):
    pos = lax.broadcasted_iota(jnp.int32, x.shape, 0) & (CHUNK - 1)
    s = 1
    while s < CHUNK:
        x = x + jnp.where(pos >= s, pltpu.roll(x, s, axis=0), 0.0)
        s *= 2
    return x


def _each(f, *lists):
    return [f(*xs) for xs in zip(*lists)]


_WKV_PRE_STAGES = 8
_T_AT, _T_RT, _T_KT, _T_BT, _T_V, _T_EC = range(6)


def _wkv_phase0(n_blocks, never, r_ref, lw_ref, k_ref, v_ref, av_ref, kk_ref, rk, ones, tilde_ref, bv_ref):
    rows = r_ref.shape[2] // n_blocks
    for i in range(n_blocks):
        after = yield
        sl = slice(i * rows, (i + 1) * rows)
        r, lw, k, v, kk = r_ref[0, 0, sl], lw_ref[0, 0, sl], k_ref[0, 0, sl], v_ref[0, 0, sl], kk_ref[0, 0, sl]
        if after is not None:
            lw = lw + jnp.where(never, jnp.concatenate([after] * (rows // after.shape[0]), axis=0), 0.0)
        c = _chunk_cumsum(lw)
        ec = jnp.exp(c)
        enc = jnp.exp(-c)
        tilde_ref[_T_AT, sl] = -kk * jnp.exp(c - lw)
        tilde_ref[_T_RT, sl] = r * ec
        tilde_ref[_T_KT, sl] = k * enc
        tilde_ref[_T_BT, sl] = kk * av_ref[0, 0, sl] * enc
        tilde_ref[_T_V, sl] = v
        tilde_ref[_T_EC, sl] = ec
        bv_ref[sl] = _group_sum(r * k * rk, ones) * v
    yield


def _wkv_chunks_pre(n_chunks, tilde_ref):
    L = CHUNK
    bf = lambda t: t.astype(BF16)
    lane = lax.broadcasted_iota(jnp.int32, (L, PAIR), 1)
    head0 = lane < HEAD
    row = lax.broadcasted_iota(jnp.int32, (L, 2 * L), 0)
    col = lax.broadcasted_iota(jnp.int32, (L, 2 * L), 1)
    src = jnp.where(col >= L, col - L, col)
    strict = src < row
    incl = src <= row
    left = col < L
    eye = jnp.where(src == row, 1.0, 0.0)
    cis = list(range(n_chunks))
    mid = n_chunks // 2 - 1

    def plane(which):
        vals = [tilde_ref[which, ci * L:(ci + 1) * L, :] for ci in cis]
        return lambda ci: vals[ci]

    at, rt, kt, bt, v = plane(_T_AT), plane(_T_RT), plane(_T_KT), plane(_T_BT), plane(_T_V)
    p_last_vals = [tilde_ref[_T_EC, (ci + 1) * L - 1:(ci + 1) * L, :] for ci in cis]
    p_last = lambda ci: p_last_vals[ci]

    def bd_rows(x):
        return jnp.concatenate([jnp.where(head0, x, 0.0), jnp.where(head0, 0.0, x)], axis=0)

    def bd_cols(x):
        return jnp.concatenate([jnp.where(left, x, 0.0), jnp.where(left, 0.0, x)], axis=0)

    def bd_rows2(x):
        return jnp.concatenate([bd_rows(x[:, :PAIR]), bd_rows(x[:, PAIR:])], axis=1)

    sc = [_dot_nt(bf(jnp.concatenate([at(ci), rt(ci)], axis=0)),
                  bf(jnp.concatenate([bd_rows(bt(ci)), bd_rows(kt(ci))], axis=0))) for ci in cis]
    yield sc[mid][:L, :PAIR], sc[-1][:L, :PAIR]
    a_ab = [jnp.where(strict, s[:L, :2 * L], 0.0) for s in sc]
    akv = [_dot(bf(jnp.where(strict, sc[ci][:L, 2 * L:], 0.0)), bf(bd_rows(v(ci)))) for ci in cis]
    p = _each(lambda a: _dot(bf(a), bf(bd_cols(a))), a_ab)
    t = [eye + a for a in a_ab]
    yield p[mid], p[-1]
    s = 2
    while 2 * s < L:
        o = _each(lambda pp, tt: _dot(bf(pp), bf(jnp.concatenate([bd_cols(tt), bd_cols(pp)], axis=1))), p, t)
        t = _each(lambda tt, oo: tt + oo[:, :2 * L], t, o)
        p = [oo[:, 2 * L:] for oo in o]
        s *= 2
        yield p[mid], p[-1]
    t = _each(lambda pp, tt: tt + _dot(bf(pp), bf(bd_cols(tt))), p, t)
    yield t[mid], t[-1]
    z = [_dot(bf(t[ci]), bf(bd_rows2(jnp.concatenate([at(ci), akv[ci]], axis=1)))) for ci in cis]
    yield z[mid][:, :PAIR], z[-1][:, :PAIR]
    zeros2 = jnp.zeros((2 * L, PAIR), F32)
    o2 = [_dot(
        bf(jnp.concatenate([jnp.where(incl, sc[ci][L:, :2 * L], 0.0), jnp.where(incl, sc[ci][L:, 2 * L:], 0.0)], axis=1)),
        bf(jnp.concatenate([bd_rows2(z[ci]), jnp.concatenate([zeros2, bd_rows(v(ci))], axis=1)], axis=0)))
        for ci in cis]
    rh = [rt(ci) + o2[ci][:, :PAIR] for ci in cis]
    yh = [oo[:, PAIR:] for oo in o2]
    yield yh[mid], yh[-1]
    zeros1 = jnp.zeros((L, PAIR), F32)
    o3 = [_dot(
        bf(jnp.concatenate([bt(ci) * p_last(ci), kt(ci) * p_last(ci)], axis=0).T),
        bf(jnp.concatenate([z[ci], jnp.concatenate([zeros1, v(ci)], axis=1)], axis=0))) for ci in cis]
    r128 = lax.broadcasted_iota(jnp.int32, (PAIR, PAIR), 0)
    c128 = lax.broadcasted_iota(jnp.int32, (PAIR, PAIR), 1)
    same_head = (r128 < HEAD) == (c128 < HEAD)
    m = [jnp.where(same_head, o3[ci][:, :PAIR], 0.0) + jnp.where(r128 == c128, p_last(ci), 0.0) for ci in cis]
    g = [jnp.where(same_head, oo[:, PAIR:], 0.0) for oo in o3]
    return rh, yh, m, g


def _wkv_chain(n_chunks, first_tile, mrh_ref, g_ref, yh_ref, h_ref, ys):
    L = CHUNK
    h = jnp.where(first_tile, 0.0, h_ref[...])
    for ci in range(n_chunks):
        o = _dot(mrh_ref[ci], h.astype(BF16))
        h = o[:PAIR] + g_ref[ci]
        ys.append(o[PAIR:] + yh_ref[ci * L:(ci + 1) * L, :])
        yield
    h_ref[...] = h


def _wkv_kernel(n_chunks, tiles_per_seq,
                r_ref, lw_ref, k_ref, v_ref, av_ref, kk_ref, pvec_in_ref, pvec_out_ref, ones_ref,
                y_ref, h_ref, tilde_ref, bv_ref, mrh_ref, g_ref, yh_ref):
    s = pl.program_id(0)

    @pl.when(s == 0)
    def _():
        for ref in (h_ref, tilde_ref, bv_ref, mrh_ref, g_ref, yh_ref):
            ref[...] = jnp.zeros_like(ref)

    p0 = s % 2
    p1 = 1 - p0
    b_in = s % 3
    b_out = (s + 1) % 3
    ones = ones_ref[...]

    ys = []
    first_tile = (s - 2) % tiles_per_seq == 0
    chain = _wkv_chain(n_chunks, first_tile, mrh_ref.at[p0], g_ref.at[p0], yh_ref.at[p0], h_ref, ys)
    pre = _wkv_chunks_pre(n_chunks, tilde_ref.at[p1])
    phase0 = _wkv_phase0(2 * _WKV_PRE_STAGES, s < 0, r_ref, lw_ref, k_ref, v_ref, av_ref, kk_ref,
                         pvec_in_ref[0:1, :], ones, tilde_ref.at[p0], bv_ref.at[b_in])
    next(phase0)
    chain_steps_per_stage = -(-n_chunks // _WKV_PRE_STAGES)
    blocks_left = 2 * _WKV_PRE_STAGES
    while True:
        try:
            deps = next(pre)
        except StopIteration as stop:
            rh, yh, m, g = stop.value
            break
        for _ in range(chain_steps_per_stage):
            next(chain, None)
        for dep in deps:
            if blocks_left:
                phase0.send(dep)
                blocks_left -= 1
    for _ in chain:
        pass
    assert blocks_left == 0

    y = jnp.concatenate(ys, axis=0)
    inv_n = 1.0 / HEAD
    d = y - _group_sum(y, ones) * inv_n
    var = _group_sum(d * d, ones) * inv_n
    y_ref[0, 0] = d * lax.rsqrt(var + GN_EPS) * pvec_out_ref[1:2, :] + pvec_out_ref[2:3, :] + bv_ref[b_out]

    for ci in range(n_chunks):
        mrh_ref[p1, ci] = jnp.concatenate([m[ci], rh[ci]], axis=0).astype(BF16)
        g_ref[p1, ci] = g[ci]
    yh_ref[p1] = jnp.concatenate(yh, axis=0)


def _wkv(r, lw, k, v, av, kk, pvec, ones, tt):
    B, n_pairs, T, _ = r.shape
    n_tiles = T // tt
    n_chunks = tt // CHUNK
    steps = B * n_pairs * n_tiles

    def tile_index(i):
        return i // (n_pairs * n_tiles), (i // n_tiles) % n_pairs, i % n_tiles

    t_in = lambda s: jnp.minimum(s, steps - 1)
    t_out = lambda s: jnp.clip(s - 2, 0, steps - 1)
    act_in = pl.BlockSpec((1, 1, tt, PAIR), lambda s: tile_index(t_in(s)) + (0,))
    act_out = pl.BlockSpec((1, 1, tt, PAIR), lambda s: tile_index(t_out(s)) + (0,))
    return pl.pallas_call(
        functools.partial(_wkv_kernel, n_chunks, n_tiles),
        grid=(steps + 2,),
        in_specs=[act_in] * 6 + [pl.BlockSpec((8, PAIR), lambda s: (0, tile_index(t_in(s))[1])),
                                 pl.BlockSpec((8, PAIR), lambda s: (0, tile_index(t_out(s))[1])),
                                 pl.BlockSpec(ones.shape, lambda s: (0, 0))],
        out_specs=act_out,
        out_shape=jax.ShapeDtypeStruct((B, n_pairs, T, PAIR), F32),
        scratch_shapes=[pltpu.VMEM((PAIR, PAIR), F32),
                        pltpu.VMEM((2, 6, tt, PAIR), F32),
                        pltpu.VMEM((3, tt, PAIR), F32),
                        pltpu.VMEM((2, n_chunks, PAIR + CHUNK, PAIR), BF16),
                        pltpu.VMEM((2, n_chunks, PAIR, PAIR), F32),
                        pltpu.VMEM((2, tt, PAIR), F32)],
        compiler_params=pltpu.CompilerParams(
            dimension_semantics=("arbitrary",), vmem_limit_bytes=VMEM_LIMIT),
        name="wkv",
    )(r, lw, k, v, av, kk, pvec, pvec, ones)


def _post_mlp_kernel(has_gate, has_kv, has_q, *refs):
    refs = list(refs)
    x_ref, y_ref = refs[:2]
    del refs[:2]
    g_ref = refs.pop(0) if has_gate else None
    wo_ref, ln_ref, w1_ref, w2_ref = refs[:4]
    del refs[:4]
    if has_kv or has_q:
        ones_ref = refs.pop(0)
    if has_kv:
        kv_ln_ref, wkv_ref, kg_ref = refs[:3]
        del refs[:3]
    if has_q:
        q_ln_ref, wq_ref, qg_ref = refs[:3]
        del refs[:3]
    o_ref = refs.pop(0)
    if has_kv:
        k_ref, v_ref = refs[:2]
        del refs[:2]
    if has_q:
        q_ref = refs.pop(0)

    y = _load_pairs(y_ref)
    if has_gate:
        y = y * _load_pairs(g_ref)
    xn = x_ref[...] + _dot(y.astype(BF16), wo_ref[...])
    hid = _dot(_rms(xn, ln_ref[...]).astype(BF16), w1_ref[...])
    hid = jnp.square(jnp.maximum(hid, 0.0)).astype(BF16)
    out = xn + _dot(hid, w2_ref[...])
    o_ref[...] = out

    if has_kv:
        kv = _dot(_rms(out, kv_ln_ref[...]).astype(BF16), wkv_ref[...])
        half = kv.shape[1] // 2
        k = kv[:, :half]
        ms = _group_sum(k * k, ones_ref[...]) * (1.0 / HEAD)
        k_ref[...] = (k * lax.rsqrt(ms + RMS_EPS) * kg_ref[...]).astype(BF16)
        v_ref[...] = kv[:, half:].astype(BF16)
    if has_q:
        q = _dot(_rms(out, q_ln_ref[...]).astype(BF16), wq_ref[...])
        ms = _group_sum(q * q, ones_ref[...]) * (1.0 / HEAD)
        q_ref[...] = (q * lax.rsqrt(ms + RMS_EPS) * qg_ref[...] * (HEAD ** -0.5 * LOG2E)).astype(BF16)


def _post_mlp(x, y, g, wo, ln, w1, w2, tm, ones=None, kv=None, q=None):
    M, C = x.shape
    tiles_per_row = y.shape[2] // tm
    act = pl.BlockSpec((tm, C), lambda i: (i, 0))
    pairs = pl.BlockSpec((1, C // PAIR, tm, PAIR), lambda i: (i // tiles_per_row, 0, i % tiles_per_row, 0))
    has_gate, has_kv, has_q = g is not None, kv is not None, q is not None
    acts = [x, y] + ([g] if has_gate else [])
    consts = [wo, ln, w1, w2] + ([ones] if has_kv or has_q else []) + list(kv or ()) + list(q or ())
    out_shape = [jax.ShapeDtypeStruct((M, C), F32)]
    out_specs = [act]
    if has_kv:
        n = kv[1].shape[1] // 2
        out_shape += [jax.ShapeDtypeStruct((M, n), BF16)] * 2
        out_specs += [pl.BlockSpec((tm, n), lambda i: (i, 0))] * 2
    if has_q:
        n = q[1].shape[1]
        out_shape.append(jax.ShapeDtypeStruct((M, n), BF16))
        out_specs.append(pl.BlockSpec((tm, n), lambda i: (i, 0)))
    return pl.pallas_call(
        functools.partial(_post_mlp_kernel, has_gate, has_kv, has_q),
        grid=(M // tm,),
        in_specs=[act] + [pairs] * (len(acts) - 1) + [_weight_spec(a) for a in consts],
        out_specs=out_specs,
        out_shape=out_shape,
        compiler_params=pltpu.CompilerParams(
            dimension_semantics=("parallel",), vmem_limit_bytes=VMEM_LIMIT),
        name="post_mlp",
    )(*acts, *map(_weight_arg, consts))


def _swa_bias_init(n_q_heads, slopes_ref, bias_ref):
    W = WINDOW
    qi = lax.broadcasted_iota(jnp.int32, (W, 2 * W), 0)
    kj = lax.broadcasted_iota(jnp.int32, (W, 2 * W), 1)
    dist = qi + W - kj
    in_window = (dist >= 0) & (dist < WINDOW)
    in_window_cur = in_window & (kj >= W)
    distf = dist.astype(F32)
    for head in range(n_q_heads):
        alibi = -(slopes_ref[head] * LOG2E) * distf
        bias_ref[0, head] = jnp.where(in_window, alibi, NEG_BIG)
        bias_ref[1, head] = jnp.where(in_window_cur, alibi, NEG_BIG)


def _swa_tile(n_q_heads, first_in_row, sinks_ref, q_ref, k_refs, v_refs, bias_ref, attn_ref):
    W = WINDOW
    group = n_q_heads // N_KV
    pairs = group // 2
    lane = lax.broadcasted_iota(jnp.int32, (W, PAIR), 1)
    head0 = lane < HEAD
    lane2 = lax.broadcasted_iota(jnp.int32, (2 * W, PAIR), 1)
    head0_kv = lane2 < HEAD
    zero = jnp.zeros((), BF16)

    for qb in range(q_ref.shape[0] // W):
        rows = slice(qb * W, (qb + 1) * W)
        plane = jnp.where(first_in_row, 1, 0) if qb == 0 else 0
        for h in range(N_KV):
            ks = slice(h * PAIR, (h + 1) * PAIR)
            k2 = jnp.concatenate([k_refs[qb][:, ks], k_refs[qb + 1][:, ks]], axis=0)
            v2 = jnp.concatenate([v_refs[qb][:, ks], v_refs[qb + 1][:, ks]], axis=0)
            q_tiles = [q_ref[rows, (h * pairs + j) * PAIR:(h * pairs + j + 1) * PAIR] for j in range(pairs)]
            lhs = jnp.concatenate([jnp.where(head0, t, zero) for t in q_tiles]
                                  + [jnp.where(head0, zero, t) for t in q_tiles], axis=0)
            s_all = _dot_nt(lhs, k2)
            yield
            probs, denoms = [], []
            for i in range(group):
                head = h * group + 2 * (i % pairs) + i // pairs
                s = s_all[i * W:(i + 1) * W, :] + bias_ref[plane, head]
                sink = sinks_ref[head] * LOG2E
                mx = jnp.maximum(jnp.max(s, axis=-1, keepdims=True), sink)
                p = jnp.exp2(s - mx)
                denoms.append(jnp.sum(p, axis=-1, keepdims=True) + jnp.exp2(sink - mx))
                probs.append(p.astype(BF16))
            p_first = jnp.concatenate(probs[:pairs], axis=0)
            p_second = jnp.concatenate(probs[pairs:], axis=0)
            v_stack = jnp.concatenate([jnp.where(head0_kv, v2, zero), jnp.where(head0_kv, zero, v2)], axis=0)
            o = _dot(jnp.concatenate([p_first, p_second], axis=1), v_stack)
            for j in range(pairs):
                den = jnp.where(head0, denoms[j], denoms[pairs + j])
                col = (h * pairs + j) * PAIR
                attn_ref[rows, col:col + PAIR] = (o[j * W:(j + 1) * W, :] / den).astype(attn_ref.dtype)


_MLP_SPLIT = 4


def _mlp_stages(x_ref, attn_ref, wo_ref, ln_ref, w1_ref, w2_ref, result):
    xn = x_ref[...] + _dot(attn_ref[...], wo_ref[...])
    h = _rms(xn, ln_ref[...]).astype(BF16)
    yield
    step = w1_ref.shape[1] // _MLP_SPLIT
    hids = []
    for c in range(_MLP_SPLIT):
        hid = _dot(h, w1_ref[:, c * step:(c + 1) * step])
        hids.append(jnp.square(jnp.maximum(hid, 0.0)).astype(BF16))
        yield
    acc = xn
    for c in range(_MLP_SPLIT):
        acc = acc + _dot(hids[c], w2_ref[c * step:(c + 1) * step, :])
        if c + 1 < _MLP_SPLIT:
            yield
    result.append(acc)


_MLP_STAGES = 2 * _MLP_SPLIT


def _swa_mlp_kernel(has_q, n_q_heads, blocks, tiles_per_row, slopes_ref, sinks_ref, *refs):
    refs = list(refs)
    x_ref, q_ref = refs[:2]
    k_refs = refs[2:3 + blocks]
    v_refs = refs[3 + blocks:4 + 2 * blocks]
    del refs[:4 + 2 * blocks]
    wo_ref, ln_ref, w1_ref, w2_ref = refs[:4]
    del refs[:4]
    if has_q:
        ones_ref, q_ln_ref, wq_ref, qg_ref = refs[:4]
        del refs[:4]
    o_ref = refs.pop(0)
    qn_ref = refs.pop(0) if has_q else None
    attn_ref, bias_ref = refs
    s = pl.program_id(0)

    @pl.when(s == 0)
    def _():
        attn_ref[...] = jnp.zeros_like(attn_ref)
        _swa_bias_init(n_q_heads, slopes_ref, bias_ref)

    cur = s % 2
    first_in_row = s % tiles_per_row == 0
    result = []
    swa = _swa_tile(n_q_heads, first_in_row, sinks_ref, q_ref, k_refs, v_refs, bias_ref, attn_ref.at[cur])
    mlp = _mlp_stages(x_ref, attn_ref.at[1 - cur], wo_ref, ln_ref, w1_ref, w2_ref, result)
    mlp_stages_per_piece = -(-_MLP_STAGES // (blocks * N_KV))
    for _ in swa:
        for _ in range(mlp_stages_per_piece):
            next(mlp, None)
    for _ in mlp:
        pass
    out = result[0]
    o_ref[...] = out
    if has_q:
        q = _dot(_rms(out, q_ln_ref[...]).astype(BF16), wq_ref[...])
        ms = _group_sum(q * q, ones_ref[...]) * (1.0 / HEAD)
        qn_ref[...] = (q * lax.rsqrt(ms + RMS_EPS) * qg_ref[...] * (HEAD ** -0.5 * LOG2E)).astype(BF16)


def _swa_mlp(x, q, k2, v2, slopes, sinks, wo, ln, w1, w2, tm, tokens_per_row, ones=None, q_next=None):
    M, C = x.shape
    W = WINDOW
    steps = M // tm
    blocks = tm // W
    assert tokens_per_row % tm == 0
    has_q = q_next is not None
    t_att = lambda s: jnp.minimum(s, steps - 1)
    t_mlp = lambda s: jnp.maximum(s - 1, 0)
    kv_spec = lambda off: pl.BlockSpec(
        (W, k2.shape[1]), lambda s, *_: (jnp.maximum(t_att(s) * blocks + off, 0), 0))
    act_mlp = pl.BlockSpec((tm, C), lambda s, *_: (t_mlp(s), 0))
    consts = [wo, ln, w1, w2] + ([ones] + list(q_next) if has_q else [])
    out_shape = [jax.ShapeDtypeStruct((M, C), F32)]
    out_specs = [act_mlp]
    if has_q:
        out_shape.append(jax.ShapeDtypeStruct((M, q_next[1].shape[1]), BF16))
        out_specs.append(pl.BlockSpec((tm, q_next[1].shape[1]), lambda s, *_: (t_mlp(s), 0)))
    return pl.pallas_call(
        functools.partial(_swa_mlp_kernel, has_q, C // HEAD, blocks, tokens_per_row // tm),
        grid_spec=pltpu.PrefetchScalarGridSpec(
            num_scalar_prefetch=2,
            grid=(steps + 1,),
            in_specs=[act_mlp, pl.BlockSpec((tm, C), lambda s, *_: (t_att(s), 0)),
                      *[kv_spec(off) for off in range(-1, blocks)] * 2]
                     + [_weight_spec(a) for a in consts],
            out_specs=out_specs,
            scratch_shapes=[pltpu.VMEM((2, tm, C), BF16),
                            pltpu.VMEM((2, C // HEAD, W, 2 * W), F32)],
        ),
        out_shape=out_shape,
        compiler_params=pltpu.CompilerParams(
            dimension_semantics=("arbitrary",), vmem_limit_bytes=VMEM_LIMIT),
        name="swa_mlp",
    )(slopes, sinks, x, q, *[k2] * (blocks + 1), *[v2] * (blocks + 1), *map(_weight_arg, consts))


def _pad_cols(w, n):
    return jnp.pad(w, ((0, 0), (0, n - w.shape[1])))


def _pad_rows(w, n):
    return jnp.pad(w, ((0, n - w.shape[0]), (0, 0)))


def _round_up(n, m):
    return (n + m - 1) // m * m


def _lora(w_in, w_out):
    n = _round_up(w_in.shape[1], 128)
    return _pad_cols(w_in, n).astype(BF16), _pad_rows(w_out, n).astype(BF16)


def _dup_heads(w):
    c, n = w.shape
    w = w.reshape(c, n // HEAD, 1, HEAD)
    return jnp.broadcast_to(w, (c, n // HEAD, 2, HEAD)).reshape(c, 2 * n)


def _block_ones(n):
    i = jnp.arange(n) // HEAD
    return (i[:, None] == i[None, :]).astype(BF16)


def kernel(x, ln_mix, ln_mlp, mlp_w1, mlp_w2, a_mu, a_w_rkv, a_w0, a_w1, a_w2, a_a0, a_a1, a_a2, a_g1, a_g2, a_k_k, a_k_a, a_r_k, a_gn_g, a_gn_b, a_wo, a_v0, a_v1, a_v2, kv_norm, w_kv, k_gain, b_wq, b_q_gain, b_sinks, b_wo):
    B, T, C = x.shape
    M = B * T
    n_a = a_mu.shape[0]
    n_b = b_wq.shape[0]
    n_heads = C // HEAD
    tm_mlp = 512
    tm_pre = 512
    tt = 1024
    ones256 = _block_ones(256)
    ones128 = _block_ones(PAIR)
    row = lambda v: v.reshape(1, -1).astype(F32)
    slopes = jnp.exp2(-8.0 * jnp.arange(1, n_heads + 1, dtype=F32) / n_heads)
    w1_all = mlp_w1.astype(BF16)
    w2_all = mlp_w2.astype(BF16)

    def q_params(i):
        j = i - n_a
        return row(ln_mix[i]), b_wq[j].astype(BF16), jnp.tile(row(b_q_gain[j]), (1, n_heads))

    def mlp(i, x, y, g, wo):
        kv = q = None
        if i == n_a - 1:
            kv = (row(kv_norm), _dup_heads(w_kv).astype(BF16), jnp.tile(row(k_gain), (1, 2 * N_KV)))
        if n_a - 1 <= i < n_a + n_b - 1:
            q = q_params(i + 1)
        return _post_mlp(x.reshape(M, C), y, g, wo.astype(BF16), row(ln_mlp[i]), _Layer(w1_all, i), _Layer(w2_all, i),
                         tm_mlp, ones256, kv, q)

    v_first = None
    k2 = v2 = q = None
    for i in range(n_a + n_b):
        if i < n_a:
            j = i
            zeros = jnp.zeros((1, C), F32)
            vecs = jnp.concatenate(
                [a_mu[j], row(ln_mix[i]), row(a_w0[j]), row(a_a0[j]),
                 row(a_v0[j - 1]) if j > 0 else zeros, row(a_k_k[j]), row(a_k_a[j])]
                + [zeros] * 4, axis=0)
            loras = list(_lora(a_w1[j], a_w2[j]) + _lora(a_a1[j], a_a2[j]) + _lora(a_g1[j], a_g2[j]))
            if j > 0:
                loras += list(_lora(a_v1[j - 1], a_v2[j - 1]))
            r, lw, k, v, av, kk, g = _rwkv_pre(
                x, v_first if j > 0 else None, vecs, a_w_rkv[j].astype(BF16), loras, ones256, tm_pre)
            if j == 0:
                v_first = v
            pvec = jnp.concatenate([row(a_r_k[j]), row(a_gn_g[j]), row(a_gn_b[j])]
                                   + [jnp.zeros((1, C), F32)] * 5, axis=0)
            y = _wkv(r, lw, k, v, av, kk, pvec, ones128, tt)
            outs = mlp(i, x, y, g, a_wo[j])
        else:
            outs = _swa_mlp(x.reshape(M, C), q, k2, v2, slopes, b_sinks[i - n_a].astype(F32),
                            b_wo[i - n_a].astype(BF16), row(ln_mlp[i]), _Layer(w1_all, i), _Layer(w2_all, i),
                            tm_mlp, T, ones256, q_params(i + 1) if i + 1 < n_a + n_b else None)
        outs = list(outs)
        x = outs.pop(0).reshape(B, T, C)
        if i == n_a - 1:
            k2 = outs.pop(0)
            v2 = outs.pop(0)
        if outs:
            q = outs.pop(0)
    return x
```

```python
import functools
import math

import jax
import jax.numpy as jnp
from jax import lax
from jax.experimental import pallas as pl
from jax.experimental.pallas import tpu as pltpu

F32 = jnp.float32
BF16 = jnp.bfloat16

HEAD = 64
PAIR = 2 * HEAD
N_KV = 2
WINDOW = 128
CHUNK = 64
GN_EPS = 64e-5
RMS_EPS = 1e-6
NEG_BIG = -1e30
LOG2E = math.log2(math.e)
VMEM_LIMIT = 60 * 1024 * 1024

NT_DIMS = (((1,), (1,)), ((), ()))


def _dot(a, b):
    return jnp.dot(a, b, preferred_element_type=F32)


def _dot_nt(a, b):
    return lax.dot_general(a, b, NT_DIMS, preferred_element_type=F32)


def _rms(x, g):
    return x * lax.rsqrt(jnp.mean(x * x, axis=-1, keepdims=True) + RMS_EPS) * g


def _group_sum(x, ones):
    w = ones.shape[0]
    parts = [_dot(x[:, j:j + w].astype(BF16), ones) for j in range(0, x.shape[1], w)]
    return parts[0] if len(parts) == 1 else jnp.concatenate(parts, axis=1)


def _sigmoid(x):
    return 0.5 * jnp.tanh(0.5 * x) + 0.5


def _store_pairs(ref, x):
    for p in range(x.shape[1] // PAIR):
        ref[0, p] = x[:, p * PAIR:(p + 1) * PAIR]


def _load_pairs(ref):
    return jnp.concatenate([ref[0, p] for p in range(ref.shape[1])], axis=1)


def _const_spec(shape):
    nd = len(shape)
    return pl.BlockSpec(shape, lambda *_: (0,) * nd, pipeline_mode=pl.Buffered(1))


class _Layer:
    def __init__(self, stacked, layer):
        self.stacked, self.layer, self.shape = stacked, layer, stacked.shape[1:]


def _weight_spec(w):
    if isinstance(w, _Layer):
        nd = len(w.shape)
        return pl.BlockSpec((None,) + w.shape, lambda *_: (w.layer,) + (0,) * nd, pipeline_mode=pl.Buffered(1))
    return _const_spec(w.shape)


def _weight_arg(w):
    return w.stacked if isinstance(w, _Layer) else w


_V_MU, _V_LN, _V_W0, _V_A0, _V_V0, _V_KK, _V_KA = 0, 6, 7, 8, 9, 10, 11


def _rwkv_pre_kernel(has_vres, tm, *refs):
    if has_vres:
        (x_ref, vf_ref, vec_ref, wrkv_ref, w1_ref, w2_ref, a1_ref, a2_ref, g1_ref, g2_ref,
         v1_ref, v2_ref, ones_ref,
         r_ref, lw_ref, k_ref, v_ref, av_ref, kk_ref, g_ref, carry_ref) = refs
    else:
        (x_ref, vec_ref, wrkv_ref, w1_ref, w2_ref, a1_ref, a2_ref, g1_ref, g2_ref, ones_ref,
         r_ref, lw_ref, k_ref, v_ref, av_ref, kk_ref, g_ref, carry_ref) = refs

    @pl.when(pl.program_id(1) == 0)
    def _():
        carry_ref[...] = jnp.zeros_like(carry_ref)

    def vec(i):
        return vec_ref[i:i + 1, :]

    h = _rms(x_ref[0], vec(_V_LN))
    row = lax.broadcasted_iota(jnp.int32, h.shape, 0)
    prev = jnp.where(row == 0, carry_ref[0:1, :], pltpu.roll(h, 1, axis=0))
    carry_ref[0:1, :] = h[tm - 1:tm, :]
    xx = prev - h

    def mix(i):
        return (h + xx * vec(_V_MU + i)).astype(BF16)

    r = _dot(mix(0), wrkv_ref[0])
    k = _dot(mix(1), wrkv_ref[1])
    xv = mix(2)
    v = _dot(xv, wrkv_ref[2])

    wl = vec(_V_W0) + _dot(jnp.tanh(_dot(mix(3), w1_ref[...])).astype(BF16), w2_ref[...])
    _store_pairs(lw_ref, -math.exp(-0.5) * _sigmoid(wl))

    if has_vres:
        gate = _sigmoid(vec(_V_V0) + _dot(_dot(xv, v1_ref[...]).astype(BF16), v2_ref[...]))
        v = v + (_load_pairs(vf_ref) - v) * gate
    a = _sigmoid(vec(_V_A0) + _dot(_dot(mix(4), a1_ref[...]).astype(BF16), a2_ref[...]))
    _store_pairs(g_ref, _dot(_sigmoid(_dot(mix(5), g1_ref[...])).astype(BF16), g2_ref[...]))

    kk = k * vec(_V_KK)
    _store_pairs(kk_ref, kk * jnp.minimum(lax.rsqrt(_group_sum(kk * kk, ones_ref[...])), 1e12))
    _store_pairs(k_ref, k * (1.0 + (a - 1.0) * vec(_V_KA)))
    _store_pairs(r_ref, r)
    _store_pairs(v_ref, v)
    _store_pairs(av_ref, a)


def _rwkv_pre(x, v_first, vecs, wrkv, loras, ones, tm):
    B, T, C = x.shape
    has_vres = v_first is not None
    act = pl.BlockSpec((1, tm, C), lambda b, t: (b, t, 0))
    pairs = pl.BlockSpec((1, C // PAIR, tm, PAIR), lambda b, t: (b, 0, t, 0))
    ins = [x] + ([v_first] if has_vres else []) + [vecs, wrkv] + list(loras) + [ones]
    in_specs = [act] + ([pairs] if has_vres else []) + [_const_spec(a.shape) for a in ins[(2 if has_vres else 1):]]
    out = jax.ShapeDtypeStruct((B, C // PAIR, T, PAIR), F32)
    return pl.pallas_call(
        functools.partial(_rwkv_pre_kernel, has_vres, tm),
        grid=(B, T // tm),
        in_specs=in_specs,
        out_specs=[pairs] * 7,
        out_shape=[out] * 7,
        scratch_shapes=[pltpu.VMEM((8, C), F32)],
        compiler_params=pltpu.CompilerParams(
            dimension_semantics=("parallel", "arbitrary"), vmem_limit_bytes=VMEM_LIMIT),
        name="rwkv_pre",
    )(*ins)


def _chunk_cumsum(x):
    pos = lax.broadcasted_iota(jnp.int32, x.shape, 0) & (CHUNK - 1)
    s = 1
    while s < CHUNK:
        x = x + jnp.where(pos >= s, pltpu.roll(x, s, axis=0), 0.0)
        s *= 2
    return x


def _each(f, *lists):
    return [f(*xs) for xs in zip(*lists)]


_WKV_PRE_STAGES = 8
_T_AT, _T_RT, _T_KT, _T_BT, _T_V, _T_EC = range(6)


def _wkv_phase0(n_blocks, never, r_ref, lw_ref, k_ref, v_ref, av_ref, kk_ref, rk, ones, tilde_ref, bv_ref):
    rows = r_ref.shape[2] // n_blocks
    for i in range(n_blocks):
        after = yield
        sl = slice(i * rows, (i + 1) * rows)
        r, lw, k, v, kk = r_ref[0, 0, sl], lw_ref[0, 0, sl], k_ref[0, 0, sl], v_ref[0, 0, sl], kk_ref[0, 0, sl]
        if after is not None:
            lw = lw + jnp.where(never, jnp.concatenate([after] * (rows // after.shape[0]), axis=0), 0.0)
        c = _chunk_cumsum(lw)
        ec = jnp.exp(c)
        enc = jnp.exp(-c)
        tilde_ref[_T_AT, sl] = -kk * jnp.exp(c - lw)
        tilde_ref[_T_RT, sl] = r * ec
        tilde_ref[_T_KT, sl] = k * enc
        tilde_ref[_T_BT, sl] = kk * av_ref[0, 0, sl] * enc
        tilde_ref[_T_V, sl] = v
        tilde_ref[_T_EC, sl] = ec
        bv_ref[sl] = _group_sum(r * k * rk, ones) * v
    yield


def _wkv_chunks_pre(first, n_chunks, tilde_ref):
    L = CHUNK
    bf = lambda t: t.astype(BF16)
    lane = lax.broadcasted_iota(jnp.int32, (L, PAIR), 1)
    head0 = lane < HEAD
    row = lax.broadcasted_iota(jnp.int32, (L, 2 * L), 0)
    col = lax.broadcasted_iota(jnp.int32, (L, 2 * L), 1)
    src = jnp.where(col >= L, col - L, col)
    strict = src < row
    incl = src <= row
    left = col < L
    eye = jnp.where(src == row, 1.0, 0.0)
    cis = list(range(n_chunks))
    mid = n_chunks // 2 - 1

    def plane(which):
        vals = [tilde_ref[which, (first + ci) * L:(first + ci + 1) * L, :] for ci in cis]
        return lambda ci: vals[ci]

    at, rt, kt, bt, v = plane(_T_AT), plane(_T_RT), plane(_T_KT), plane(_T_BT), plane(_T_V)
    p_last_vals = [tilde_ref[_T_EC, (first + ci + 1) * L - 1:(first + ci + 1) * L, :] for ci in cis]
    p_last = lambda ci: p_last_vals[ci]

    def bd_rows(x):
        return jnp.concatenate([jnp.where(head0, x, 0.0), jnp.where(head0, 0.0, x)], axis=0)

    def bd_cols(x):
        return jnp.concatenate([jnp.where(left, x, 0.0), jnp.where(left, 0.0, x)], axis=0)

    def bd_rows2(x):
        return jnp.concatenate([bd_rows(x[:, :PAIR]), bd_rows(x[:, PAIR:])], axis=1)

    sc = [_dot_nt(bf(jnp.concatenate([at(ci), rt(ci)], axis=0)),
                  bf(jnp.concatenate([bd_rows(bt(ci)), bd_rows(kt(ci))], axis=0))) for ci in cis]
    yield sc[mid][:L, :PAIR], sc[-1][:L, :PAIR]
    a_ab = [jnp.where(strict, s[:L, :2 * L], 0.0) for s in sc]
    akv = [_dot(bf(jnp.where(strict, sc[ci][:L, 2 * L:], 0.0)), bf(bd_rows(v(ci)))) for ci in cis]
    p = _each(lambda a: _dot(bf(a), bf(bd_cols(a))), a_ab)
    t = [eye + a for a in a_ab]
    yield p[mid], p[-1]
    s = 2
    while 2 * s < L:
        o = _each(lambda pp, tt: _dot(bf(pp), bf(jnp.concatenate([bd_cols(tt), bd_cols(pp)], axis=1))), p, t)
        t = _each(lambda tt, oo: tt + oo[:, :2 * L], t, o)
        p = [oo[:, 2 * L:] for oo in o]
        s *= 2
        yield p[mid], p[-1]
    t = _each(lambda pp, tt: tt + _dot(bf(pp), bf(bd_cols(tt))), p, t)
    yield t[mid], t[-1]
    z = [_dot(bf(t[ci]), bf(bd_rows2(jnp.concatenate([at(ci), akv[ci]], axis=1)))) for ci in cis]
    yield z[mid][:, :PAIR], z[-1][:, :PAIR]
    zeros2 = jnp.zeros((2 * L, PAIR), F32)
    o2 = [_dot(
        bf(jnp.concatenate([jnp.where(incl, sc[ci][L:, :2 * L], 0.0), jnp.where(incl, sc[ci][L:, 2 * L:], 0.0)], axis=1)),
        bf(jnp.concatenate([bd_rows2(z[ci]), jnp.concatenate([zeros2, bd_rows(v(ci))], axis=1)], axis=0)))
        for ci in cis]
    rh = [rt(ci) + o2[ci][:, :PAIR] for ci in cis]
    yh = [oo[:, PAIR:] for oo in o2]
    yield yh[mid], yh[-1]
    zeros1 = jnp.zeros((L, PAIR), F32)
    o3 = [_dot(
        bf(jnp.concatenate([bt(ci) * p_last(ci), kt(ci) * p_last(ci)], axis=0).T),
        bf(jnp.concatenate([z[ci], jnp.concatenate([zeros1, v(ci)], axis=1)], axis=0))) for ci in cis]
    r128 = lax.broadcasted_iota(jnp.int32, (PAIR, PAIR), 0)
    c128 = lax.broadcasted_iota(jnp.int32, (PAIR, PAIR), 1)
    same_head = (r128 < HEAD) == (c128 < HEAD)
    m = [jnp.where(same_head, o3[ci][:, :PAIR], 0.0) + jnp.where(r128 == c128, p_last(ci), 0.0) for ci in cis]
    g = [jnp.where(same_head, oo[:, PAIR:], 0.0) for oo in o3]
    return rh, yh, m, g


def _wkv_chain(n_chunks, first_tile, mrh_ref, g_ref, yh_ref, h_ref, ys):
    L = CHUNK
    h = jnp.where(first_tile, 0.0, h_ref[...])
    for ci in range(n_chunks):
        o = _dot(mrh_ref[ci], h.astype(BF16))
        h = o[:PAIR] + g_ref[ci]
        ys.append(o[PAIR:] + yh_ref[ci * L:(ci + 1) * L, :])
        yield
    h_ref[...] = h


def _wkv_kernel(n_chunks, tiles_per_seq,
                r_ref, lw_ref, k_ref, v_ref, av_ref, kk_ref, pvec_in_ref, pvec_out_ref, ones_ref,
                y_ref, h_ref, tilde_ref, bv_ref, mrh_ref, g_ref, yh_ref):
    s = pl.program_id(0)

    @pl.when(s == 0)
    def _():
        for ref in (h_ref, tilde_ref, bv_ref, mrh_ref, g_ref, yh_ref):
            ref[...] = jnp.zeros_like(ref)

    p0 = s % 2
    p1 = 1 - p0
    b_in = s % 3
    b_out = (s + 1) % 3
    ones = ones_ref[...]

    ys = []
    first_tile = (s - 2) % tiles_per_seq == 0
    chain = _wkv_chain(n_chunks, first_tile, mrh_ref.at[p0], g_ref.at[p0], yh_ref.at[p0], h_ref, ys)
    half = n_chunks // 2
    pre = [_wkv_chunks_pre(0, half, tilde_ref.at[p1]), _wkv_chunks_pre(half, n_chunks - half, tilde_ref.at[p1])]
    done = [None, None]
    phase0 = _wkv_phase0(2 * _WKV_PRE_STAGES, s < 0, r_ref, lw_ref, k_ref, v_ref, av_ref, kk_ref,
                         pvec_in_ref[0:1, :], ones, tilde_ref.at[p0], bv_ref.at[b_in])
    next(phase0)
    blocks_left = [2 * _WKV_PRE_STAGES]

    def advance(i):
        try:
            deps = next(pre[i])
        except StopIteration as stop:
            done[i] = stop.value
            return
        next(chain, None)
        if blocks_left[0]:
            phase0.send(deps[-1])
            blocks_left[0] -= 1

    advance(0)
    while done[0] is None or done[1] is None:
        for i in (0, 1):
            if done[i] is None:
                advance(i)
    for _ in chain:
        pass
    assert blocks_left[0] == 0
    rh, yh, m, g = (done[0][j] + done[1][j] for j in range(4))

    y = jnp.concatenate(ys, axis=0)
    inv_n = 1.0 / HEAD
    d = y - _group_sum(y, ones) * inv_n
    var = _group_sum(d * d, ones) * inv_n
    y_ref[0, 0] = d * lax.rsqrt(var + GN_EPS) * pvec_out_ref[1:2, :] + pvec_out_ref[2:3, :] + bv_ref[b_out]

    for ci in range(n_chunks):
        mrh_ref[p1, ci] = jnp.concatenate([m[ci], rh[ci]], axis=0).astype(BF16)
        g_ref[p1, ci] = g[ci]
    yh_ref[p1] = jnp.concatenate(yh, axis=0)


def _wkv(r, lw, k, v, av, kk, pvec, ones, tt):
    B, n_pairs, T, _ = r.shape
    n_tiles = T // tt
    n_chunks = tt // CHUNK
    steps = B * n_pairs * n_tiles

    def tile_index(i):
        return i // (n_pairs * n_tiles), (i // n_tiles) % n_pairs, i % n_tiles

    t_in = lambda s: jnp.minimum(s, steps - 1)
    t_out = lambda s: jnp.clip(s - 2, 0, steps - 1)
    act_in = pl.BlockSpec((1, 1, tt, PAIR), lambda s: tile_index(t_in(s)) + (0,))
    act_out = pl.BlockSpec((1, 1, tt, PAIR), lambda s: tile_index(t_out(s)) + (0,))
    return pl.pallas_call(
        functools.partial(_wkv_kernel, n_chunks, n_tiles),
        grid=(steps + 2,),
        in_specs=[act_in] * 6 + [pl.BlockSpec((8, PAIR), lambda s: (0, tile_index(t_in(s))[1])),
                                 pl.BlockSpec((8, PAIR), lambda s: (0, tile_index(t_out(s))[1])),
                                 pl.BlockSpec(ones.shape, lambda s: (0, 0))],
        out_specs=act_out,
        out_shape=jax.ShapeDtypeStruct((B, n_pairs, T, PAIR), F32),
        scratch_shapes=[pltpu.VMEM((PAIR, PAIR), F32),
                        pltpu.VMEM((2, 6, tt, PAIR), F32),
                        pltpu.VMEM((3, tt, PAIR), F32),
                        pltpu.VMEM((2, n_chunks, PAIR + CHUNK, PAIR), BF16),
                        pltpu.VMEM((2, n_chunks, PAIR, PAIR), F32),
                        pltpu.VMEM((2, tt, PAIR), F32)],
        compiler_params=pltpu.CompilerParams(
            dimension_semantics=("arbitrary",), vmem_limit_bytes=VMEM_LIMIT),
        name="wkv",
    )(r, lw, k, v, av, kk, pvec, pvec, ones)


def _post_mlp_kernel(has_gate, has_kv, has_q, *refs):
    refs = list(refs)
    x_ref, y_ref = refs[:2]
    del refs[:2]
    g_ref = refs.pop(0) if has_gate else None
    wo_ref, ln_ref, w1_ref, w2_ref = refs[:4]
    del refs[:4]
    if has_kv or has_q:
        ones_ref = refs.pop(0)
    if has_kv:
        kv_ln_ref, wkv_ref, kg_ref = refs[:3]
        del refs[:3]
    if has_q:
        q_ln_ref, wq_ref, qg_ref = refs[:3]
        del refs[:3]
    o_ref = refs.pop(0)
    if has_kv:
        k_ref, v_ref = refs[:2]
        del refs[:2]
    if has_q:
        q_ref = refs.pop(0)

    y = _load_pairs(y_ref)
    if has_gate:
        y = y * _load_pairs(g_ref)
    xn = x_ref[...] + _dot(y.astype(BF16), wo_ref[...])
    hid = _dot(_rms(xn, ln_ref[...]).astype(BF16), w1_ref[...])
    hid = jnp.square(jnp.maximum(hid, 0.0)).astype(BF16)
    out = xn + _dot(hid, w2_ref[...])
    o_ref[...] = out

    if has_kv:
        kv = _dot(_rms(out, kv_ln_ref[...]).astype(BF16), wkv_ref[...])
        half = kv.shape[1] // 2
        k = kv[:, :half]
        ms = _group_sum(k * k, ones_ref[...]) * (1.0 / HEAD)
        k_ref[...] = (k * lax.rsqrt(ms + RMS_EPS) * kg_ref[...]).astype(BF16)
        v_ref[...] = kv[:, half:].astype(BF16)
    if has_q:
        q = _dot(_rms(out, q_ln_ref[...]).astype(BF16), wq_ref[...])
        ms = _group_sum(q * q, ones_ref[...]) * (1.0 / HEAD)
        q_ref[...] = (q * lax.rsqrt(ms + RMS_EPS) * qg_ref[...] * (HEAD ** -0.5 * LOG2E)).astype(BF16)


def _post_mlp(x, y, g, wo, ln, w1, w2, tm, ones=None, kv=None, q=None):
    M, C = x.shape
    tiles_per_row = y.shape[2] // tm
    act = pl.BlockSpec((tm, C), lambda i: (i, 0))
    pairs = pl.BlockSpec((1, C // PAIR, tm, PAIR), lambda i: (i // tiles_per_row, 0, i % tiles_per_row, 0))
    has_gate, has_kv, has_q = g is not None, kv is not None, q is not None
    acts = [x, y] + ([g] if has_gate else [])
    consts = [wo, ln, w1, w2] + ([ones] if has_kv or has_q else []) + list(kv or ()) + list(q or ())
    out_shape = [jax.ShapeDtypeStruct((M, C), F32)]
    out_specs = [act]
    if has_kv:
        n = kv[1].shape[1] // 2
        out_shape += [jax.ShapeDtypeStruct((M, n), BF16)] * 2
        out_specs += [pl.BlockSpec((tm, n), lambda i: (i, 0))] * 2
    if has_q:
        n = q[1].shape[1]
        out_shape.append(jax.ShapeDtypeStruct((M, n), BF16))
        out_specs.append(pl.BlockSpec((tm, n), lambda i: (i, 0)))
    return pl.pallas_call(
        functools.partial(_post_mlp_kernel, has_gate, has_kv, has_q),
        grid=(M // tm,),
        in_specs=[act] + [pairs] * (len(acts) - 1) + [_weight_spec(a) for a in consts],
        out_specs=out_specs,
        out_shape=out_shape,
        compiler_params=pltpu.CompilerParams(
            dimension_semantics=("parallel",), vmem_limit_bytes=VMEM_LIMIT),
        name="post_mlp",
    )(*acts, *map(_weight_arg, consts))


def _swa_bias_init(n_q_heads, slopes_ref, bias_ref):
    W = WINDOW
    qi = lax.broadcasted_iota(jnp.int32, (W, 2 * W), 0)
    kj = lax.broadcasted_iota(jnp.int32, (W, 2 * W), 1)
    dist = qi + W - kj
    in_window = (dist >= 0) & (dist < WINDOW)
    in_window_cur = in_window & (kj >= W)
    distf = dist.astype(F32)
    for head in range(n_q_heads):
        alibi = -(slopes_ref[head] * LOG2E) * distf
        bias_ref[0, head] = jnp.where(in_window, alibi, NEG_BIG)
        bias_ref[1, head] = jnp.where(in_window_cur, alibi, NEG_BIG)


def _swa_tile(n_q_heads, first_in_row, sinks_ref, q_ref, k_refs, v_refs, bias_ref, attn_ref):
    W = WINDOW
    group = n_q_heads // N_KV
    pairs = group // 2
    lane = lax.broadcasted_iota(jnp.int32, (W, PAIR), 1)
    head0 = lane < HEAD
    lane2 = lax.broadcasted_iota(jnp.int32, (2 * W, PAIR), 1)
    head0_kv = lane2 < HEAD
    zero = jnp.zeros((), BF16)

    for qb in range(q_ref.shape[0] // W):
        rows = slice(qb * W, (qb + 1) * W)
        plane = jnp.where(first_in_row, 1, 0) if qb == 0 else 0
        for h in range(N_KV):
            ks = slice(h * PAIR, (h + 1) * PAIR)
            k2 = jnp.concatenate([k_refs[qb][:, ks], k_refs[qb + 1][:, ks]], axis=0)
            v2 = jnp.concatenate([v_refs[qb][:, ks], v_refs[qb + 1][:, ks]], axis=0)
            q_tiles = [q_ref[rows, (h * pairs + j) * PAIR:(h * pairs + j + 1) * PAIR] for j in range(pairs)]
            lhs = jnp.concatenate([jnp.where(head0, t, zero) for t in q_tiles]
                                  + [jnp.where(head0, zero, t) for t in q_tiles], axis=0)
            s_all = _dot_nt(lhs, k2)
            yield
            probs, denoms = [], []
            for i in range(group):
                head = h * group + 2 * (i % pairs) + i // pairs
                s = s_all[i * W:(i + 1) * W, :] + bias_ref[plane, head]
                sink = sinks_ref[head] * LOG2E
                mx = jnp.maximum(jnp.max(s, axis=-1, keepdims=True), sink)
                p = jnp.exp2(s - mx)
                denoms.append(jnp.sum(p, axis=-1, keepdims=True) + jnp.exp2(sink - mx))
                probs.append(p.astype(BF16))
            p_first = jnp.concatenate(probs[:pairs], axis=0)
            p_second = jnp.concatenate(probs[pairs:], axis=0)
            v_stack = jnp.concatenate([jnp.where(head0_kv, v2, zero), jnp.where(head0_kv, zero, v2)], axis=0)
            o = _dot(jnp.concatenate([p_first, p_second], axis=1), v_stack)
            for j in range(pairs):
                den = jnp.where(head0, denoms[j], denoms[pairs + j])
                col = (h * pairs + j) * PAIR
                attn_ref[rows, col:col + PAIR] = (o[j * W:(j + 1) * W, :] / den).astype(attn_ref.dtype)


_MLP_SPLIT = 4


def _mlp_stages(x_ref, attn_ref, wo_ref, ln_ref, w1_ref, w2_ref, result):
    xn = x_ref[...] + _dot(attn_ref[...], wo_ref[...])
    h = _rms(xn, ln_ref[...]).astype(BF16)
    yield
    step = w1_ref.shape[1] // _MLP_SPLIT
    hids = []
    for c in range(_MLP_SPLIT):
        hid = _dot(h, w1_ref[:, c * step:(c + 1) * step])
        hids.append(jnp.square(jnp.maximum(hid, 0.0)).astype(BF16))
        yield
    acc = xn
    for c in range(_MLP_SPLIT):
        acc = acc + _dot(hids[c], w2_ref[c * step:(c + 1) * step, :])
        if c + 1 < _MLP_SPLIT:
            yield
    result.append(acc)


_MLP_STAGES = 2 * _MLP_SPLIT


def _swa_mlp_kernel(has_q, n_q_heads, blocks, tiles_per_row, slopes_ref, sinks_ref, *refs):
    refs = list(refs)
    x_ref, q_ref = refs[:2]
    k_refs = refs[2:3 + blocks]
    v_refs = refs[3 + blocks:4 + 2 * blocks]
    del refs[:4 + 2 * blocks]
    wo_ref, ln_ref, w1_ref, w2_ref = refs[:4]
    del refs[:4]
    if has_q:
        ones_ref, q_ln_ref, wq_ref, qg_ref = refs[:4]
        del refs[:4]
    o_ref = refs.pop(0)
    qn_ref = refs.pop(0) if has_q else None
    attn_ref, bias_ref = refs
    s = pl.program_id(0)

    @pl.when(s == 0)
    def _():
        attn_ref[...] = jnp.zeros_like(attn_ref)
        _swa_bias_init(n_q_heads, slopes_ref, bias_ref)

    cur = s % 2
    first_in_row = s % tiles_per_row == 0
    result = []
    swa = _swa_tile(n_q_heads, first_in_row, sinks_ref, q_ref, k_refs, v_refs, bias_ref, attn_ref.at[cur])
    mlp = _mlp_stages(x_ref, attn_ref.at[1 - cur], wo_ref, ln_ref, w1_ref, w2_ref, result)
    mlp_stages_per_piece = -(-_MLP_STAGES // (blocks * N_KV))
    for _ in swa:
        for _ in range(mlp_stages_per_piece):
            next(mlp, None)
    for _ in mlp:
        pass
    out = result[0]
    o_ref[...] = out
    if has_q:
        q = _dot(_rms(out, q_ln_ref[...]).astype(BF16), wq_ref[...])
        ms = _group_sum(q * q, ones_ref[...]) * (1.0 / HEAD)
        qn_ref[...] = (q * lax.rsqrt(ms + RMS_EPS) * qg_ref[...] * (HEAD ** -0.5 * LOG2E)).astype(BF16)


def _swa_mlp(x, q, k2, v2, slopes, sinks, wo, ln, w1, w2, tm, tokens_per_row, ones=None, q_next=None):
    M, C = x.shape
    W = WINDOW
    steps = M // tm
    blocks = tm // W
    assert tokens_per_row % tm == 0
    has_q = q_next is not None
    t_att = lambda s: jnp.minimum(s, steps - 1)
    t_mlp = lambda s: jnp.maximum(s - 1, 0)
    kv_spec = lambda off: pl.BlockSpec(
        (W, k2.shape[1]), lambda s, *_: (jnp.maximum(t_att(s) * blocks + off, 0), 0))
    act_mlp = pl.BlockSpec((tm, C), lambda s, *_: (t_mlp(s), 0))
    consts = [wo, ln, w1, w2] + ([ones] + list(q_next) if has_q else [])
    out_shape = [jax.ShapeDtypeStruct((M, C), F32)]
    out_specs = [act_mlp]
    if has_q:
        out_shape.append(jax.ShapeDtypeStruct((M, q_next[1].shape[1]), BF16))
        out_specs.append(pl.BlockSpec((tm, q_next[1].shape[1]), lambda s, *_: (t_mlp(s), 0)))
    return pl.pallas_call(
        functools.partial(_swa_mlp_kernel, has_q, C // HEAD, blocks, tokens_per_row // tm),
        grid_spec=pltpu.PrefetchScalarGridSpec(
            num_scalar_prefetch=2,
            grid=(steps + 1,),
            in_specs=[act_mlp, pl.BlockSpec((tm, C), lambda s, *_: (t_att(s), 0)),
                      *[kv_spec(off) for off in range(-1, blocks)] * 2]
                     + [_weight_spec(a) for a in consts],
            out_specs=out_specs,
            scratch_shapes=[pltpu.VMEM((2, tm, C), BF16),
                            pltpu.VMEM((2, C // HEAD, W, 2 * W), F32)],
        ),
        out_shape=out_shape,
        compiler_params=pltpu.CompilerParams(
            dimension_semantics=("arbitrary",), vmem_limit_bytes=VMEM_LIMIT),
        name="swa_mlp",
    )(slopes, sinks, x, q, *[k2] * (blocks + 1), *[v2] * (blocks + 1), *map(_weight_arg, consts))


def _pad_cols(w, n):
    return jnp.pad(w, ((0, 0), (0, n - w.shape[1])))


def _pad_rows(w, n):
    return jnp.pad(w, ((0, n - w.shape[0]), (0, 0)))


def _round_up(n, m):
    return (n + m - 1) // m * m


def _lora(w_in, w_out):
    n = _round_up(w_in.shape[1], 128)
    return _pad_cols(w_in, n).astype(BF16), _pad_rows(w_out, n).astype(BF16)


def _dup_heads(w):
    c, n = w.shape
    w = w.reshape(c, n // HEAD, 1, HEAD)
    return jnp.broadcast_to(w, (c, n // HEAD, 2, HEAD)).reshape(c, 2 * n)


def _block_ones(n):
    i = jnp.arange(n) // HEAD
    return (i[:, None] == i[None, :]).astype(BF16)


def kernel(x, ln_mix, ln_mlp, mlp_w1, mlp_w2, a_mu, a_w_rkv, a_w0, a_w1, a_w2, a_a0, a_a1, a_a2, a_g1, a_g2, a_k_k, a_k_a, a_r_k, a_gn_g, a_gn_b, a_wo, a_v0, a_v1, a_v2, kv_norm, w_kv, k_gain, b_wq, b_q_gain, b_sinks, b_wo):
    B, T, C = x.shape
    M = B * T
    n_a = a_mu.shape[0]
    n_b = b_wq.shape[0]
    n_heads = C // HEAD
    tm_mlp = 512
    tm_pre = 512
    tt = 1024
    ones256 = _block_ones(256)
    ones128 = _block_ones(PAIR)
    row = lambda v: v.reshape(1, -1).astype(F32)
    slopes = jnp.exp2(-8.0 * jnp.arange(1, n_heads + 1, dtype=F32) / n_heads)
    w1_all = mlp_w1.astype(BF16)
    w2_all = mlp_w2.astype(BF16)

    def q_params(i):
        j = i - n_a
        return row(ln_mix[i]), b_wq[j].astype(BF16), jnp.tile(row(b_q_gain[j]), (1, n_heads))

    def mlp(i, x, y, g, wo):
        kv = q = None
        if i == n_a - 1:
            kv = (row(kv_norm), _dup_heads(w_kv).astype(BF16), jnp.tile(row(k_gain), (1, 2 * N_KV)))
        if n_a - 1 <= i < n_a + n_b - 1:
            q = q_params(i + 1)
        return _post_mlp(x.reshape(M, C), y, g, wo.astype(BF16), row(ln_mlp[i]), _Layer(w1_all, i), _Layer(w2_all, i),
                         tm_mlp, ones256, kv, q)

    v_first = None
    k2 = v2 = q = None
    for i in range(n_a + n_b):
        if i < n_a:
            j = i
            zeros = jnp.zeros((1, C), F32)
            vecs = jnp.concatenate(
                [a_mu[j], row(ln_mix[i]), row(a_w0[j]), row(a_a0[j]),
                 row(a_v0[j - 1]) if j > 0 else zeros, row(a_k_k[j]), row(a_k_a[j])]
                + [zeros] * 4, axis=0)
            loras = list(_lora(a_w1[j], a_w2[j]) + _lora(a_a1[j], a_a2[j]) + _lora(a_g1[j], a_g2[j]))
            if j > 0:
                loras += list(_lora(a_v1[j - 1], a_v2[j - 1]))
            r, lw, k, v, av, kk, g = _rwkv_pre(
                x, v_first if j > 0 else None, vecs, a_w_rkv[j].astype(BF16), loras, ones256, tm_pre)
            if j == 0:
                v_first = v
            pvec = jnp.concatenate([row(a_r_k[j]), row(a_gn_g[j]), row(a_gn_b[j])]
                                   + [jnp.zeros((1, C), F32)] * 5, axis=0)
            y = _wkv(r, lw, k, v, av, kk, pvec, ones128, tt)
            outs = mlp(i, x, y, g, a_wo[j])
        else:
            outs = _swa_mlp(x.reshape(M, C), q, k2, v2, slopes, b_sinks[i - n_a].astype(F32),
                            b_wo[i - n_a].astype(BF16), row(ln_mlp[i]), _Layer(w1_all, i), _Layer(w2_all, i),
                            tm_mlp, T, ones256, q_params(i + 1) if i + 1 < n_a + n_b else None)
        outs = list(outs)
        x = outs.pop(0).reshape(B, T, C)
        if i == n_a - 1:
            k2 = outs.pop(0)
            v2 = outs.pop(0)
        if outs:
            q = outs.pop(0)
    return x
```

```python
import functools
import math

import jax
import jax.numpy as jnp
from jax import lax
from jax.experimental import pallas as pl
from jax.experimental.pallas import tpu as pltpu

F32 = jnp.float32
BF16 = jnp.bfloat16

HEAD = 64
PAIR = 2 * HEAD
N_KV = 2
WINDOW = 128
CHUNK = 64
GN_EPS = 64e-5
RMS_EPS = 1e-6
NEG_BIG = -1e30
LOG2E = math.log2(math.e)
VMEM_LIMIT = 60 * 1024 * 1024

NT_DIMS = (((1,), (1,)), ((), ()))


def _dot(a, b):
    return jnp.dot(a, b, preferred_element_type=F32)


def _dot_nt(a, b):
    return lax.dot_general(a, b, NT_DIMS, preferred_element_type=F32)


def _rms(x, g):
    return x * lax.rsqrt(jnp.mean(x * x, axis=-1, keepdims=True) + RMS_EPS) * g


def _group_sum(x, ones):
    w = ones.shape[0]
    parts = [_dot(x[:, j:j + w].astype(BF16), ones) for j in range(0, x.shape[1], w)]
    return parts[0] if len(parts) == 1 else jnp.concatenate(parts, axis=1)


def _sigmoid(x):
    return 0.5 * jnp.tanh(0.5 * x) + 0.5


def _store_pairs(ref, x):
    for p in range(x.shape[1] // PAIR):
        ref[0, p] = x[:, p * PAIR:(p + 1) * PAIR]


def _load_pairs(ref):
    return jnp.concatenate([ref[0, p] for p in range(ref.shape[1])], axis=1)


def _const_spec(shape):
    nd = len(shape)
    return pl.BlockSpec(shape, lambda *_: (0,) * nd, pipeline_mode=pl.Buffered(1))


class _Layer:
    def __init__(self, stacked, layer):
        self.stacked, self.layer, self.shape = stacked, layer, stacked.shape[1:]


def _weight_spec(w):
    if isinstance(w, _Layer):
        nd = len(w.shape)
        return pl.BlockSpec((None,) + w.shape, lambda *_: (w.layer,) + (0,) * nd, pipeline_mode=pl.Buffered(1))
    return _const_spec(w.shape)


def _weight_arg(w):
    return w.stacked if isinstance(w, _Layer) else w


_V_MU, _V_LN, _V_W0, _V_A0, _V_V0, _V_KK, _V_KA = 0, 6, 7, 8, 9, 10, 11


def _rwkv_pre_kernel(has_vres, tm, *refs):
    if has_vres:
        (x_ref, vf_ref, vec_ref, wrkv_ref, w1_ref, w2_ref, a1_ref, a2_ref, g1_ref, g2_ref,
         v1_ref, v2_ref, ones_ref,
         r_ref, lw_ref, k_ref, v_ref, av_ref, kk_ref, g_ref, carry_ref) = refs
    else:
        (x_ref, vec_ref, wrkv_ref, w1_ref, w2_ref, a1_ref, a2_ref, g1_ref, g2_ref, ones_ref,
         r_ref, lw_ref, k_ref, v_ref, av_ref, kk_ref, g_ref, carry_ref) = refs

    @pl.when(pl.program_id(1) == 0)
    def _():
        carry_ref[...] = jnp.zeros_like(carry_ref)

    def vec(i):
        return vec_ref[i:i + 1, :]

    h = _rms(x_ref[0], vec(_V_LN))
    row = lax.broadcasted_iota(jnp.int32, h.shape, 0)
    prev = jnp.where(row == 0, carry_ref[0:1, :], pltpu.roll(h, 1, axis=0))
    carry_ref[0:1, :] = h[tm - 1:tm, :]
    xx = prev - h

    def mix(i):
        return (h + xx * vec(_V_MU + i)).astype(BF16)

    r = _dot(mix(0), wrkv_ref[0])
    k = _dot(mix(1), wrkv_ref[1])
    xv = mix(2)
    v = _dot(xv, wrkv_ref[2])

    wl = vec(_V_W0) + _dot(jnp.tanh(_dot(mix(3), w1_ref[...])).astype(BF16), w2_ref[...])
    _store_pairs(lw_ref, -math.exp(-0.5) * _sigmoid(wl))

    if has_vres:
        gate = _sigmoid(vec(_V_V0) + _dot(_dot(xv, v1_ref[...]).astype(BF16), v2_ref[...]))
        v = v + (_load_pairs(vf_ref) - v) * gate
    a = _sigmoid(vec(_V_A0) + _dot(_dot(mix(4), a1_ref[...]).astype(BF16), a2_ref[...]))
    _store_pairs(g_ref, _dot(_sigmoid(_dot(mix(5), g1_ref[...])).astype(BF16), g2_ref[...]))

    kk = k * vec(_V_KK)
    _store_pairs(kk_ref, kk * jnp.minimum(lax.rsqrt(_group_sum(kk * kk, ones_ref[...])), 1e12))
    _store_pairs(k_ref, k * (1.0 + (a - 1.0) * vec(_V_KA)))
    _store_pairs(r_ref, r)
    _store_pairs(v_ref, v)
    _store_pairs(av_ref, a)


def _rwkv_pre(x, v_first, vecs, wrkv, loras, ones, tm):
    B, T, C = x.shape
    has_vres = v_first is not None
    act = pl.BlockSpec((1, tm, C), lambda b, t: (b, t, 0))
    pairs = pl.BlockSpec((1, C // PAIR, tm, PAIR), lambda b, t: (b, 0, t, 0))
    ins = [x] + ([v_first] if has_vres else []) + [vecs, wrkv] + list(loras) + [ones]
    in_specs = [act] + ([pairs] if has_vres else []) + [_const_spec(a.shape) for a in ins[(2 if has_vres else 1):]]
    out = jax.ShapeDtypeStruct((B, C // PAIR, T, PAIR), F32)
    return pl.pallas_call(
        functools.partial(_rwkv_pre_kernel, has_vres, tm),
        grid=(B, T // tm),
        in_specs=in_specs,
        out_specs=[pairs] * 7,
        out_shape=[out] * 7,
        scratch_shapes=[pltpu.VMEM((8, C), F32)],
        compiler_params=pltpu.CompilerParams(
            dimension_semantics=("parallel", "arbitrary"), vmem_limit_bytes=VMEM_LIMIT),
        name="rwkv_pre",
    )(*ins)


def _chunk_cumsum(x):
    pos = lax.broadcasted_iota(jnp.int32, x.shape, 0) & (CHUNK - 1)
    s = 1
    while s < CHUNK:
        x = x + jnp.where(pos >= s, pltpu.roll(x, s, axis=0), 0.0)
        s *= 2
    return x


def _each(f, *lists):
    return [f(*xs) for xs in zip(*lists)]


_WKV_PRE_STAGES = 8
_T_AT, _T_RT, _T_KT, _T_BT, _T_V, _T_EC = range(6)


def _wkv_phase0(n_blocks, never, r_ref, lw_ref, k_ref, v_ref, av_ref, kk_ref, rk, ones, tilde_ref, bv_ref):
    rows = r_ref.shape[2] // n_blocks
    for i in range(n_blocks):
        after = yield
        sl = slice(i * rows, (i + 1) * rows)
        r, lw, k, v, kk = r_ref[0, 0, sl], lw_ref[0, 0, sl], k_ref[0, 0, sl], v_ref[0, 0, sl], kk_ref[0, 0, sl]
        if after is not None:
            lw = lw + jnp.where(never, jnp.concatenate([after] * (rows // after.shape[0]), axis=0), 0.0)
        c = _chunk_cumsum(lw)
        ec = jnp.exp(c)
        enc = jnp.exp(-c)
        tilde_ref[_T_AT, sl] = -kk * jnp.exp(c - lw)
        tilde_ref[_T_RT, sl] = r * ec
        tilde_ref[_T_KT, sl] = k * enc
        tilde_ref[_T_BT, sl] = kk * av_ref[0, 0, sl] * enc
        tilde_ref[_T_V, sl] = v
        tilde_ref[_T_EC, sl] = ec
        bv_ref[sl] = _group_sum(r * k * rk, ones) * v
    yield


def _wkv_chunks_pre(first, n_chunks, tilde_ref):
    L = CHUNK
    bf = lambda t: t.astype(BF16)
    lane = lax.broadcasted_iota(jnp.int32, (L, PAIR), 1)
    head0 = lane < HEAD
    row = lax.broadcasted_iota(jnp.int32, (L, 2 * L), 0)
    col = lax.broadcasted_iota(jnp.int32, (L, 2 * L), 1)
    src = jnp.where(col >= L, col - L, col)
    strict = src < row
    incl = src <= row
    left = col < L
    eye = jnp.where(src == row, 1.0, 0.0)
    cis = list(range(n_chunks))
    mid = n_chunks // 2 - 1

    def plane(which):
        vals = [tilde_ref[which, (first + ci) * L:(first + ci + 1) * L, :] for ci in cis]
        return lambda ci: vals[ci]

    at, rt, kt, bt, v = plane(_T_AT), plane(_T_RT), plane(_T_KT), plane(_T_BT), plane(_T_V)
    p_last_vals = [tilde_ref[_T_EC, (first + ci + 1) * L - 1:(first + ci + 1) * L, :] for ci in cis]
    p_last = lambda ci: p_last_vals[ci]

    def bd_rows(x):
        return jnp.concatenate([jnp.where(head0, x, 0.0), jnp.where(head0, 0.0, x)], axis=0)

    def bd_cols(x):
        return jnp.concatenate([jnp.where(left, x, 0.0), jnp.where(left, 0.0, x)], axis=0)

    def bd_rows2(x):
        return jnp.concatenate([bd_rows(x[:, :PAIR]), bd_rows(x[:, PAIR:])], axis=1)

    sc = [_dot_nt(bf(jnp.concatenate([at(ci), rt(ci)], axis=0)),
                  bf(jnp.concatenate([bd_rows(bt(ci)), bd_rows(kt(ci))], axis=0))) for ci in cis]
    yield sc[mid][:L, :PAIR], sc[-1][:L, :PAIR]
    a_ab = [jnp.where(strict, s[:L, :2 * L], 0.0) for s in sc]
    akv = [_dot(bf(jnp.where(strict, sc[ci][:L, 2 * L:], 0.0)), bf(bd_rows(v(ci)))) for ci in cis]
    p = _each(lambda a: _dot(bf(a), bf(bd_cols(a))), a_ab)
    t = [eye + a for a in a_ab]
    yield p[mid], p[-1]
    s = 2
    while 2 * s < L:
        o = _each(lambda pp, tt: _dot(bf(pp), bf(jnp.concatenate([bd_cols(tt), bd_cols(pp)], axis=1))), p, t)
        t = _each(lambda tt, oo: tt + oo[:, :2 * L], t, o)
        p = [oo[:, 2 * L:] for oo in o]
        s *= 2
        yield p[mid], p[-1]
    t = _each(lambda pp, tt: tt + _dot(bf(pp), bf(bd_cols(tt))), p, t)
    yield t[mid], t[-1]
    z = [_dot(bf(t[ci]), bf(bd_rows2(jnp.concatenate([at(ci), akv[ci]], axis=1)))) for ci in cis]
    yield z[mid][:, :PAIR], z[-1][:, :PAIR]
    zeros2 = jnp.zeros((2 * L, PAIR), F32)
    o2 = [_dot(
        bf(jnp.concatenate([jnp.where(incl, sc[ci][L:, :2 * L], 0.0), jnp.where(incl, sc[ci][L:, 2 * L:], 0.0)], axis=1)),
        bf(jnp.concatenate([bd_rows2(z[ci]), jnp.concatenate([zeros2, bd_rows(v(ci))], axis=1)], axis=0)))
        for ci in cis]
    rh = [rt(ci) + o2[ci][:, :PAIR] for ci in cis]
    yh = [oo[:, PAIR:] for oo in o2]
    yield yh[mid], yh[-1]
    zeros1 = jnp.zeros((L, PAIR), F32)
    o3 = [_dot(
        bf(jnp.concatenate([bt(ci) * p_last(ci), kt(ci) * p_last(ci)], axis=0).T),
        bf(jnp.concatenate([z[ci], jnp.concatenate([zeros1, v(ci)], axis=1)], axis=0))) for ci in cis]
    r128 = lax.broadcasted_iota(jnp.int32, (PAIR, PAIR), 0)
    c128 = lax.broadcasted_iota(jnp.int32, (PAIR, PAIR), 1)
    same_head = (r128 < HEAD) == (c128 < HEAD)
    m = [jnp.where(same_head, o3[ci][:, :PAIR], 0.0) + jnp.where(r128 == c128, p_last(ci), 0.0) for ci in cis]
    g = [jnp.where(same_head, oo[:, PAIR:], 0.0) for oo in o3]
    return rh, yh, m, g


def _wkv_chain(n_chunks, first_tile, mrh_ref, g_ref, yh_ref, h_ref, ys):
    L = CHUNK
    h = jnp.where(first_tile, 0.0, h_ref[...])
    for ci in range(n_chunks):
        o = _dot(mrh_ref[ci], h.astype(BF16))
        h = o[:PAIR] + g_ref[ci]
        ys.append(o[PAIR:] + yh_ref[ci * L:(ci + 1) * L, :])
        yield
    h_ref[...] = h


def _wkv_kernel(n_chunks, tiles_per_seq,
                r_ref, lw_ref, k_ref, v_ref, av_ref, kk_ref, pvec_in_ref, pvec_out_ref, ones_ref,
                y_ref, h_ref, tilde_ref, bv_ref, mrh_ref, g_ref, yh_ref):
    s = pl.program_id(0)

    @pl.when(s == 0)
    def _():
        for ref in (h_ref, tilde_ref, bv_ref, mrh_ref, g_ref, yh_ref):
            ref[...] = jnp.zeros_like(ref)

    b_in = s % 3
    b_out = (s + 1) % 3

    def body(p0, p1):
        ones = ones_ref[...]
        ys = []
        first_tile = (s - 2) % tiles_per_seq == 0
        chain = _wkv_chain(n_chunks, first_tile, mrh_ref.at[p0], g_ref.at[p0], yh_ref.at[p0], h_ref, ys)
        half = n_chunks // 2
        pre = [_wkv_chunks_pre(0, half, tilde_ref.at[p1]),
               _wkv_chunks_pre(half, n_chunks - half, tilde_ref.at[p1])]
        done = [None, None]
        phase0 = _wkv_phase0(2 * _WKV_PRE_STAGES, s < 0, r_ref, lw_ref, k_ref, v_ref, av_ref, kk_ref,
                             pvec_in_ref[0:1, :], ones, tilde_ref.at[p0], bv_ref.at[b_in])
        next(phase0)
        blocks_left = [2 * _WKV_PRE_STAGES]

        def advance(i):
            try:
                deps = next(pre[i])
            except StopIteration as stop:
                done[i] = stop.value
                return
            next(chain, None)
            if blocks_left[0]:
                phase0.send(deps[-1])
                blocks_left[0] -= 1

        advance(0)
        while done[0] is None or done[1] is None:
            for i in (0, 1):
                if done[i] is None:
                    advance(i)
        for _ in chain:
            pass
        assert blocks_left[0] == 0
        rh, yh, m, g = (done[0][j] + done[1][j] for j in range(4))

        y = jnp.concatenate(ys, axis=0)
        inv_n = 1.0 / HEAD
        d = y - _group_sum(y, ones) * inv_n
        var = _group_sum(d * d, ones) * inv_n
        y_ref[0, 0] = d * lax.rsqrt(var + GN_EPS) * pvec_out_ref[1:2, :] + pvec_out_ref[2:3, :] + bv_ref[b_out]

        for ci in range(n_chunks):
            mrh_ref[p1, ci] = jnp.concatenate([m[ci], rh[ci]], axis=0).astype(BF16)
            g_ref[p1, ci] = g[ci]
        yh_ref[p1] = jnp.concatenate(yh, axis=0)

    for parity in (0, 1):
        pl.when(s % 2 == parity)(functools.partial(body, parity, 1 - parity))


def _wkv(r, lw, k, v, av, kk, pvec, ones, tt):
    B, n_pairs, T, _ = r.shape
    n_tiles = T // tt
    n_chunks = tt // CHUNK
    steps = B * n_pairs * n_tiles

    def tile_index(i):
        return i // (n_pairs * n_tiles), (i // n_tiles) % n_pairs, i % n_tiles

    t_in = lambda s: jnp.minimum(s, steps - 1)
    t_out = lambda s: jnp.clip(s - 2, 0, steps - 1)
    act_in = pl.BlockSpec((1, 1, tt, PAIR), lambda s: tile_index(t_in(s)) + (0,))
    act_out = pl.BlockSpec((1, 1, tt, PAIR), lambda s: tile_index(t_out(s)) + (0,))
    return pl.pallas_call(
        functools.partial(_wkv_kernel, n_chunks, n_tiles),
        grid=(steps + 2,),
        in_specs=[act_in] * 6 + [pl.BlockSpec((8, PAIR), lambda s: (0, tile_index(t_in(s))[1])),
                                 pl.BlockSpec((8, PAIR), lambda s: (0, tile_index(t_out(s))[1])),
                                 pl.BlockSpec(ones.shape, lambda s: (0, 0))],
        out_specs=act_out,
        out_shape=jax.ShapeDtypeStruct((B, n_pairs, T, PAIR), F32),
        scratch_shapes=[pltpu.VMEM((PAIR, PAIR), F32),
                        pltpu.VMEM((2, 6, tt, PAIR), F32),
                        pltpu.VMEM((3, tt, PAIR), F32),
                        pltpu.VMEM((2, n_chunks, PAIR + CHUNK, PAIR), BF16),
                        pltpu.VMEM((2, n_chunks, PAIR, PAIR), F32),
                        pltpu.VMEM((2, tt, PAIR), F32)],
        compiler_params=pltpu.CompilerParams(
            dimension_semantics=("arbitrary",), vmem_limit_bytes=VMEM_LIMIT),
        name="wkv",
    )(r, lw, k, v, av, kk, pvec, pvec, ones)


def _post_mlp_kernel(has_gate, has_kv, has_q, *refs):
    refs = list(refs)
    x_ref, y_ref = refs[:2]
    del refs[:2]
    g_ref = refs.pop(0) if has_gate else None
    wo_ref, ln_ref, w1_ref, w2_ref = refs[:4]
    del refs[:4]
    if has_kv or has_q:
        ones_ref = refs.pop(0)
    if has_kv:
        kv_ln_ref, wkv_ref, kg_ref = refs[:3]
        del refs[:3]
    if has_q:
        q_ln_ref, wq_ref, qg_ref = refs[:3]
        del refs[:3]
    o_ref = refs.pop(0)
    if has_kv:
        k_ref, v_ref = refs[:2]
        del refs[:2]
    if has_q:
        q_ref = refs.pop(0)

    y = _load_pairs(y_ref)
    if has_gate:
        y = y * _load_pairs(g_ref)
    xn = x_ref[...] + _dot(y.astype(BF16), wo_ref[...])
    hid = _dot(_rms(xn, ln_ref[...]).astype(BF16), w1_ref[...])
    hid = jnp.square(jnp.maximum(hid, 0.0)).astype(BF16)
    out = xn + _dot(hid, w2_ref[...])
    o_ref[...] = out

    if has_kv:
        kv = _dot(_rms(out, kv_ln_ref[...]).astype(BF16), wkv_ref[...])
        half = kv.shape[1] // 2
        k = kv[:, :half]
        ms = _group_sum(k * k, ones_ref[...]) * (1.0 / HEAD)
        k_ref[...] = (k * lax.rsqrt(ms + RMS_EPS) * kg_ref[...]).astype(BF16)
        v_ref[...] = kv[:, half:].astype(BF16)
    if has_q:
        q = _dot(_rms(out, q_ln_ref[...]).astype(BF16), wq_ref[...])
        ms = _group_sum(q * q, ones_ref[...]) * (1.0 / HEAD)
        q_ref[...] = (q * lax.rsqrt(ms + RMS_EPS) * qg_ref[...] * (HEAD ** -0.5 * LOG2E)).astype(BF16)


def _post_mlp(x, y, g, wo, ln, w1, w2, tm, ones=None, kv=None, q=None):
    M, C = x.shape
    tiles_per_row = y.shape[2] // tm
    act = pl.BlockSpec((tm, C), lambda i: (i, 0))
    pairs = pl.BlockSpec((1, C // PAIR, tm, PAIR), lambda i: (i // tiles_per_row, 0, i % tiles_per_row, 0))
    has_gate, has_kv, has_q = g is not None, kv is not None, q is not None
    acts = [x, y] + ([g] if has_gate else [])
    consts = [wo, ln, w1, w2] + ([ones] if has_kv or has_q else []) + list(kv or ()) + list(q or ())
    out_shape = [jax.ShapeDtypeStruct((M, C), F32)]
    out_specs = [act]
    if has_kv:
        n = kv[1].shape[1] // 2
        out_shape += [jax.ShapeDtypeStruct((M, n), BF16)] * 2
        out_specs += [pl.BlockSpec((tm, n), lambda i: (i, 0))] * 2
    if has_q:
        n = q[1].shape[1]
        out_shape.append(jax.ShapeDtypeStruct((M, n), BF16))
        out_specs.append(pl.BlockSpec((tm, n), lambda i: (i, 0)))
    return pl.pallas_call(
        functools.partial(_post_mlp_kernel, has_gate, has_kv, has_q),
        grid=(M // tm,),
        in_specs=[act] + [pairs] * (len(acts) - 1) + [_weight_spec(a) for a in consts],
        out_specs=out_specs,
        out_shape=out_shape,
        compiler_params=pltpu.CompilerParams(
            dimension_semantics=("parallel",), vmem_limit_bytes=VMEM_LIMIT),
        name="post_mlp",
    )(*acts, *map(_weight_arg, consts))


def _swa_bias_init(n_q_heads, slopes_ref, bias_ref):
    W = WINDOW
    qi = lax.broadcasted_iota(jnp.int32, (W, 2 * W), 0)
    kj = lax.broadcasted_iota(jnp.int32, (W, 2 * W), 1)
    dist = qi + W - kj
    in_window = (dist >= 0) & (dist < WINDOW)
    in_window_cur = in_window & (kj >= W)
    distf = dist.astype(F32)
    for head in range(n_q_heads):
        alibi = -(slopes_ref[head] * LOG2E) * distf
        bias_ref[0, head] = jnp.where(in_window, alibi, NEG_BIG)
        bias_ref[1, head] = jnp.where(in_window_cur, alibi, NEG_BIG)


def _swa_tile(n_q_heads, first_in_row, sinks_ref, q_ref, k_refs, v_refs, bias_ref, attn_ref):
    W = WINDOW
    group = n_q_heads // N_KV
    pairs = group // 2
    lane = lax.broadcasted_iota(jnp.int32, (W, PAIR), 1)
    head0 = lane < HEAD
    lane2 = lax.broadcasted_iota(jnp.int32, (2 * W, PAIR), 1)
    head0_kv = lane2 < HEAD
    zero = jnp.zeros((), BF16)

    for qb in range(q_ref.shape[0] // W):
        rows = slice(qb * W, (qb + 1) * W)
        plane = jnp.where(first_in_row, 1, 0) if qb == 0 else 0
        for h in range(N_KV):
            ks = slice(h * PAIR, (h + 1) * PAIR)
            k2 = jnp.concatenate([k_refs[qb][:, ks], k_refs[qb + 1][:, ks]], axis=0)
            v2 = jnp.concatenate([v_refs[qb][:, ks], v_refs[qb + 1][:, ks]], axis=0)
            q_tiles = [q_ref[rows, (h * pairs + j) * PAIR:(h * pairs + j + 1) * PAIR] for j in range(pairs)]
            lhs = jnp.concatenate([jnp.where(head0, t, zero) for t in q_tiles]
                                  + [jnp.where(head0, zero, t) for t in q_tiles], axis=0)
            s_all = _dot_nt(lhs, k2)
            yield
            probs, denoms = [], []
            for i in range(group):
                head = h * group + 2 * (i % pairs) + i // pairs
                s = s_all[i * W:(i + 1) * W, :] + bias_ref[plane, head]
                sink = sinks_ref[head] * LOG2E
                mx = jnp.maximum(jnp.max(s, axis=-1, keepdims=True), sink)
                p = jnp.exp2(s - mx)
                denoms.append(jnp.sum(p, axis=-1, keepdims=True) + jnp.exp2(sink - mx))
                probs.append(p.astype(BF16))
            p_first = jnp.concatenate(probs[:pairs], axis=0)
            p_second = jnp.concatenate(probs[pairs:], axis=0)
            v_stack = jnp.concatenate([jnp.where(head0_kv, v2, zero), jnp.where(head0_kv, zero, v2)], axis=0)
            o = _dot(jnp.concatenate([p_first, p_second], axis=1), v_stack)
            for j in range(pairs):
                den = jnp.where(head0, denoms[j], denoms[pairs + j])
                col = (h * pairs + j) * PAIR
                attn_ref[rows, col:col + PAIR] = (o[j * W:(j + 1) * W, :] / den).astype(attn_ref.dtype)


_MLP_SPLIT = 4


def _mlp_stages(x_ref, attn_ref, wo_ref, ln_ref, w1_ref, w2_ref, result):
    xn = x_ref[...] + _dot(attn_ref[...], wo_ref[...])
    h = _rms(xn, ln_ref[...]).astype(BF16)
    yield
    step = w1_ref.shape[1] // _MLP_SPLIT
    hids = []
    for c in range(_MLP_SPLIT):
        hid = _dot(h, w1_ref[:, c * step:(c + 1) * step])
        hids.append(jnp.square(jnp.maximum(hid, 0.0)).astype(BF16))
        yield
    acc = xn
    for c in range(_MLP_SPLIT):
        acc = acc + _dot(hids[c], w2_ref[c * step:(c + 1) * step, :])
        if c + 1 < _MLP_SPLIT:
            yield
    result.append(acc)


_MLP_STAGES = 2 * _MLP_SPLIT


def _swa_mlp_kernel(has_q, n_q_heads, blocks, tiles_per_row, slopes_ref, sinks_ref, *refs):
    refs = list(refs)
    x_ref, q_ref = refs[:2]
    k_refs = refs[2:3 + blocks]
    v_refs = refs[3 + blocks:4 + 2 * blocks]
    del refs[:4 + 2 * blocks]
    wo_ref, ln_ref, w1_ref, w2_ref = refs[:4]
    del refs[:4]
    if has_q:
        ones_ref, q_ln_ref, wq_ref, qg_ref = refs[:4]
        del refs[:4]
    o_ref = refs.pop(0)
    qn_ref = refs.pop(0) if has_q else None
    attn_ref, bias_ref = refs
    s = pl.program_id(0)

    @pl.when(s == 0)
    def _():
        attn_ref[...] = jnp.zeros_like(attn_ref)
        _swa_bias_init(n_q_heads, slopes_ref, bias_ref)

    cur = s % 2
    first_in_row = s % tiles_per_row == 0
    result = []
    swa = _swa_tile(n_q_heads, first_in_row, sinks_ref, q_ref, k_refs, v_refs, bias_ref, attn_ref.at[cur])
    mlp = _mlp_stages(x_ref, attn_ref.at[1 - cur], wo_ref, ln_ref, w1_ref, w2_ref, result)
    mlp_stages_per_piece = -(-_MLP_STAGES // (blocks * N_KV))
    for _ in swa:
        for _ in range(mlp_stages_per_piece):
            next(mlp, None)
    for _ in mlp:
        pass
    out = result[0]
    o_ref[...] = out
    if has_q:
        q = _dot(_rms(out, q_ln_ref[...]).astype(BF16), wq_ref[...])
        ms = _group_sum(q * q, ones_ref[...]) * (1.0 / HEAD)
        qn_ref[...] = (q * lax.rsqrt(ms + RMS_EPS) * qg_ref[...] * (HEAD ** -0.5 * LOG2E)).astype(BF16)


def _swa_mlp(x, q, k2, v2, slopes, sinks, wo, ln, w1, w2, tm, tokens_per_row, ones=None, q_next=None):
    M, C = x.shape
    W = WINDOW
    steps = M // tm
    blocks = tm // W
    assert tokens_per_row % tm == 0
    has_q = q_next is not None
    t_att = lambda s: jnp.minimum(s, steps - 1)
    t_mlp = lambda s: jnp.maximum(s - 1, 0)
    kv_spec = lambda off: pl.BlockSpec(
        (W, k2.shape[1]), lambda s, *_: (jnp.maximum(t_att(s) * blocks + off, 0), 0))
    act_mlp = pl.BlockSpec((tm, C), lambda s, *_: (t_mlp(s), 0))
    consts = [wo, ln, w1, w2] + ([ones] + list(q_next) if has_q else [])
    out_shape = [jax.ShapeDtypeStruct((M, C), F32)]
    out_specs = [act_mlp]
    if has_q:
        out_shape.append(jax.ShapeDtypeStruct((M, q_next[1].shape[1]), BF16))
        out_specs.append(pl.BlockSpec((tm, q_next[1].shape[1]), lambda s, *_: (t_mlp(s), 0)))
    return pl.pallas_call(
        functools.partial(_swa_mlp_kernel, has_q, C // HEAD, blocks, tokens_per_row // tm),
        grid_spec=pltpu.PrefetchScalarGridSpec(
            num_scalar_prefetch=2,
            grid=(steps + 1,),
            in_specs=[act_mlp, pl.BlockSpec((tm, C), lambda s, *_: (t_att(s), 0)),
                      *[kv_spec(off) for off in range(-1, blocks)] * 2]
                     + [_weight_spec(a) for a in consts],
            out_specs=out_specs,
            scratch_shapes=[pltpu.VMEM((2, tm, C), BF16),
                            pltpu.VMEM((2, C // HEAD, W, 2 * W), F32)],
        ),
        out_shape=out_shape,
        compiler_params=pltpu.CompilerParams(
            dimension_semantics=("arbitrary",), vmem_limit_bytes=VMEM_LIMIT),
        name="swa_mlp",
    )(slopes, sinks, x, q, *[k2] * (blocks + 1), *[v2] * (blocks + 1), *map(_weight_arg, consts))


def _pad_cols(w, n):
    return jnp.pad(w, ((0, 0), (0, n - w.shape[1])))


def _pad_rows(w, n):
    return jnp.pad(w, ((0, n - w.shape[0]), (0, 0)))


def _round_up(n, m):
    return (n + m - 1) // m * m


def _lora(w_in, w_out):
    n = _round_up(w_in.shape[1], 128)
    return _pad_cols(w_in, n).astype(BF16), _pad_rows(w_out, n).astype(BF16)


def _dup_heads(w):
    c, n = w.shape
    w = w.reshape(c, n // HEAD, 1, HEAD)
    return jnp.broadcast_to(w, (c, n // HEAD, 2, HEAD)).reshape(c, 2 * n)


def _block_ones(n):
    i = jnp.arange(n) // HEAD
    return (i[:, None] == i[None, :]).astype(BF16)


def kernel(x, ln_mix, ln_mlp, mlp_w1, mlp_w2, a_mu, a_w_rkv, a_w0, a_w1, a_w2, a_a0, a_a1, a_a2, a_g1, a_g2, a_k_k, a_k_a, a_r_k, a_gn_g, a_gn_b, a_wo, a_v0, a_v1, a_v2, kv_norm, w_kv, k_gain, b_wq, b_q_gain, b_sinks, b_wo):
    B, T, C = x.shape
    M = B * T
    n_a = a_mu.shape[0]
    n_b = b_wq.shape[0]
    n_heads = C // HEAD
    tm_mlp = 512
    tm_pre = 512
    tt = 1024
    ones256 = _block_ones(256)
    ones128 = _block_ones(PAIR)
    row = lambda v: v.reshape(1, -1).astype(F32)
    slopes = jnp.exp2(-8.0 * jnp.arange(1, n_heads + 1, dtype=F32) / n_heads)
    w1_all = mlp_w1.astype(BF16)
    w2_all = mlp_w2.astype(BF16)

    def q_params(i):
        j = i - n_a
        return row(ln_mix[i]), b_wq[j].astype(BF16), jnp.tile(row(b_q_gain[j]), (1, n_heads))

    def mlp(i, x, y, g, wo):
        kv = q = None
        if i == n_a - 1:
            kv = (row(kv_norm), _dup_heads(w_kv).astype(BF16), jnp.tile(row(k_gain), (1, 2 * N_KV)))
        if n_a - 1 <= i < n_a + n_b - 1:
            q = q_params(i + 1)
        return _post_mlp(x.reshape(M, C), y, g, wo.astype(BF16), row(ln_mlp[i]), _Layer(w1_all, i), _Layer(w2_all, i),
                         tm_mlp, ones256, kv, q)

    v_first = None
    k2 = v2 = q = None
    for i in range(n_a + n_b):
        if i < n_a:
            j = i
            zeros = jnp.zeros((1, C), F32)
            vecs = jnp.concatenate(
                [a_mu[j], row(ln_mix[i]), row(a_w0[j]), row(a_a0[j]),
                 row(a_v0[j - 1]) if j > 0 else zeros, row(a_k_k[j]), row(a_k_a[j])]
                + [zeros] * 4, axis=0)
            loras = list(_lora(a_w1[j], a_w2[j]) + _lora(a_a1[j], a_a2[j]) + _lora(a_g1[j], a_g2[j]))
            if j > 0:
                loras += list(_lora(a_v1[j - 1], a_v2[j - 1]))
            r, lw, k, v, av, kk, g = _rwkv_pre(
                x, v_first if j > 0 else None, vecs, a_w_rkv[j].astype(BF16), loras, ones256, tm_pre)
            if j == 0:
                v_first = v
            pvec = jnp.concatenate([row(a_r_k[j]), row(a_gn_g[j]), row(a_gn_b[j])]
                                   + [jnp.zeros((1, C), F32)] * 5, axis=0)
            y = _wkv(r, lw, k, v, av, kk, pvec, ones128, tt)
            outs = mlp(i, x, y, g, a_wo[j])
        else:
            outs = _swa_mlp(x.reshape(M, C), q, k2, v2, slopes, b_sinks[i - n_a].astype(F32),
                            b_wo[i - n_a].astype(BF16), row(ln_mlp[i]), _Layer(w1_all, i), _Layer(w2_all, i),
                            tm_mlp, T, ones256, q_params(i + 1) if i + 1 < n_a + n_b else None)
        outs = list(outs)
        x = outs.pop(0).reshape(B, T, C)
        if i == n_a - 1:
            k2 = outs.pop(0)
            v2 = outs.pop(0)
        if outs:
            q = outs.pop(0)
    return x
```

```python
import functools
import math

import jax
import jax.numpy as jnp
from jax import lax
from jax.experimental import pallas as pl
from jax.experimental.pallas import tpu as pltpu

F32 = jnp.float32
BF16 = jnp.bfloat16

HEAD = 64
PAIR = 2 * HEAD
N_KV = 2
WINDOW = 128
CHUNK = 64
GN_EPS = 64e-5
RMS_EPS = 1e-6
NEG_BIG = -1e30
LOG2E = math.log2(math.e)
VMEM_LIMIT = 60 * 1024 * 1024

NT_DIMS = (((1,), (1,)), ((), ()))


def _dot(a, b):
    return jnp.dot(a, b, preferred_element_type=F32)


def _dot_nt(a, b):
    return lax.dot_general(a, b, NT_DIMS, preferred_element_type=F32)


def _rms(x, g):
    return x * lax.rsqrt(jnp.mean(x * x, axis=-1, keepdims=True) + RMS_EPS) * g


def _group_sum(x, ones):
    w = ones.shape[0]
    parts = [_dot(x[:, j:j + w].astype(BF16), ones) for j in range(0, x.shape[1], w)]
    return parts[0] if len(parts) == 1 else jnp.concatenate(parts, axis=1)


def _sigmoid(x):
    return 0.5 * jnp.tanh(0.5 * x) + 0.5


def _store_pairs(ref, x):
    for p in range(x.shape[1] // PAIR):
        ref[0, p] = x[:, p * PAIR:(p + 1) * PAIR]


def _load_pairs(ref):
    return jnp.concatenate([ref[0, p] for p in range(ref.shape[1])], axis=1)


def _const_spec(shape):
    nd = len(shape)
    return pl.BlockSpec(shape, lambda *_: (0,) * nd, pipeline_mode=pl.Buffered(1))


class _Layer:
    def __init__(self, stacked, layer):
        self.stacked, self.layer, self.shape = stacked, layer, stacked.shape[1:]


def _weight_spec(w):
    if isinstance(w, _Layer):
        nd = len(w.shape)
        return pl.BlockSpec((None,) + w.shape, lambda *_: (w.layer,) + (0,) * nd, pipeline_mode=pl.Buffered(1))
    return _const_spec(w.shape)


def _weight_arg(w):
    return w.stacked if isinstance(w, _Layer) else w


_V_MU, _V_LN, _V_W0, _V_A0, _V_V0, _V_KK, _V_KA = 0, 6, 7, 8, 9, 10, 11


def _rwkv_pre_kernel(has_vres, tm, *refs):
    if has_vres:
        (x_ref, vf_ref, vec_ref, wrkv_ref, w1_ref, w2_ref, a1_ref, a2_ref, g1_ref, g2_ref,
         v1_ref, v2_ref, ones_ref,
         r_ref, lw_ref, k_ref, v_ref, av_ref, kk_ref, g_ref, carry_ref) = refs
    else:
        (x_ref, vec_ref, wrkv_ref, w1_ref, w2_ref, a1_ref, a2_ref, g1_ref, g2_ref, ones_ref,
         r_ref, lw_ref, k_ref, v_ref, av_ref, kk_ref, g_ref, carry_ref) = refs

    @pl.when(pl.program_id(1) == 0)
    def _():
        carry_ref[...] = jnp.zeros_like(carry_ref)

    def vec(i):
        return vec_ref[i:i + 1, :]

    h = _rms(x_ref[0], vec(_V_LN))
    row = lax.broadcasted_iota(jnp.int32, h.shape, 0)
    prev = jnp.where(row == 0, carry_ref[0:1, :], pltpu.roll(h, 1, axis=0))
    carry_ref[0:1, :] = h[tm - 1:tm, :]
    xx = prev - h

    def mix(i):
        return (h + xx * vec(_V_MU + i)).astype(BF16)

    r = _dot(mix(0), wrkv_ref[0])
    k = _dot(mix(1), wrkv_ref[1])
    xv = mix(2)
    v = _dot(xv, wrkv_ref[2])

    wl = vec(_V_W0) + _dot(jnp.tanh(_dot(mix(3), w1_ref[...])).astype(BF16), w2_ref[...])
    _store_pairs(lw_ref, -math.exp(-0.5) * _sigmoid(wl))

    if has_vres:
        gate = _sigmoid(vec(_V_V0) + _dot(_dot(xv, v1_ref[...]).astype(BF16), v2_ref[...]))
        v = v + (_load_pairs(vf_ref) - v) * gate
    a = _sigmoid(vec(_V_A0) + _dot(_dot(mix(4), a1_ref[...]).astype(BF16), a2_ref[...]))
    _store_pairs(g_ref, _dot(_sigmoid(_dot(mix(5), g1_ref[...])).astype(BF16), g2_ref[...]))

    kk = k * vec(_V_KK)
    _store_pairs(kk_ref, kk * jnp.minimum(lax.rsqrt(_group_sum(kk * kk, ones_ref[...])), 1e12))
    _store_pairs(k_ref, k * (1.0 + (a - 1.0) * vec(_V_KA)))
    _store_pairs(r_ref, r)
    _store_pairs(v_ref, v)
    _store_pairs(av_ref, a)


def _rwkv_pre(x, v_first, vecs, wrkv, loras, ones, tm):
    B, T, C = x.shape
    has_vres = v_first is not None
    act = pl.BlockSpec((1, tm, C), lambda b, t: (b, t, 0))
    pairs = pl.BlockSpec((1, C // PAIR, tm, PAIR), lambda b, t: (b, 0, t, 0))
    ins = [x] + ([v_first] if has_vres else []) + [vecs, wrkv] + list(loras) + [ones]
    in_specs = [act] + ([pairs] if has_vres else []) + [_const_spec(a.shape) for a in ins[(2 if has_vres else 1):]]
    out = jax.ShapeDtypeStruct((B, C // PAIR, T, PAIR), F32)
    return pl.pallas_call(
        functools.partial(_rwkv_pre_kernel, has_vres, tm),
        grid=(B, T // tm),
        in_specs=in_specs,
        out_specs=[pairs] * 7,
        out_shape=[out] * 7,
        scratch_shapes=[pltpu.VMEM((8, C), F32)],
        compiler_params=pltpu.CompilerParams(
            dimension_semantics=("parallel", "arbitrary"), vmem_limit_bytes=VMEM_LIMIT),
        name="rwkv_pre",
    )(*ins)


def _chunk_cumsum(x):
    pos = lax.broadcasted_iota(jnp.int32, x.shape, 0) & (CHUNK - 1)
    s = 1
    while s < CHUNK:
        x = x + jnp.where(pos >= s, pltpu.roll(x, s, axis=0), 0.0)
        s *= 2
    return x


def _each(f, *lists):
    return [f(*xs) for xs in zip(*lists)]


_WKV_PRE_STAGES = 8
_T_AT, _T_RT, _T_KT, _T_BT, _T_V, _T_EC = range(6)


def _wkv_phase0(n_blocks, r_ref, lw_ref, k_ref, v_ref, av_ref, kk_ref, rk, ones, tilde_ref, bv_ref):
    rows = r_ref.shape[2] // n_blocks
    for i in range(n_blocks):
        sl = slice(i * rows, (i + 1) * rows)
        r, lw, k, v, kk = r_ref[0, 0, sl], lw_ref[0, 0, sl], k_ref[0, 0, sl], v_ref[0, 0, sl], kk_ref[0, 0, sl]
        c = _chunk_cumsum(lw)
        ec = jnp.exp(c)
        enc = jnp.exp(-c)
        tilde_ref[_T_AT, sl] = -kk * jnp.exp(c - lw)
        tilde_ref[_T_RT, sl] = r * ec
        tilde_ref[_T_KT, sl] = k * enc
        tilde_ref[_T_BT, sl] = kk * av_ref[0, 0, sl] * enc
        tilde_ref[_T_V, sl] = v
        tilde_ref[_T_EC, sl] = ec
        bv_ref[sl] = _group_sum(r * k * rk, ones) * v
        yield


def _wkv_chunks_pre(first, n_chunks, tilde_ref):
    L = CHUNK
    bf = lambda t: t.astype(BF16)
    lane = lax.broadcasted_iota(jnp.int32, (L, PAIR), 1)
    head0 = lane < HEAD
    row = lax.broadcasted_iota(jnp.int32, (L, 2 * L), 0)
    col = lax.broadcasted_iota(jnp.int32, (L, 2 * L), 1)
    src = jnp.where(col >= L, col - L, col)
    strict = src < row
    incl = src <= row
    left = col < L
    eye = jnp.where(src == row, 1.0, 0.0)
    cis = list(range(n_chunks))

    def plane(which):
        vals = [tilde_ref[which, (first + ci) * L:(first + ci + 1) * L, :] for ci in cis]
        return lambda ci: vals[ci]

    at, rt, kt, bt, v = plane(_T_AT), plane(_T_RT), plane(_T_KT), plane(_T_BT), plane(_T_V)
    p_last_vals = [tilde_ref[_T_EC, (first + ci + 1) * L - 1:(first + ci + 1) * L, :] for ci in cis]
    p_last = lambda ci: p_last_vals[ci]

    def bd_rows(x):
        return jnp.concatenate([jnp.where(head0, x, 0.0), jnp.where(head0, 0.0, x)], axis=0)

    def bd_cols(x):
        return jnp.concatenate([jnp.where(left, x, 0.0), jnp.where(left, 0.0, x)], axis=0)

    def bd_rows2(x):
        return jnp.concatenate([bd_rows(x[:, :PAIR]), bd_rows(x[:, PAIR:])], axis=1)

    sc = [_dot_nt(bf(jnp.concatenate([at(ci), rt(ci)], axis=0)),
                  bf(jnp.concatenate([bd_rows(bt(ci)), bd_rows(kt(ci))], axis=0))) for ci in cis]
    yield
    a_ab = [jnp.where(strict, s[:L, :2 * L], 0.0) for s in sc]
    akv = [_dot(bf(jnp.where(strict, sc[ci][:L, 2 * L:], 0.0)), bf(bd_rows(v(ci)))) for ci in cis]
    p = _each(lambda a: _dot(bf(a), bf(bd_cols(a))), a_ab)
    t = [eye + a for a in a_ab]
    yield
    s = 2
    while 2 * s < L:
        o = _each(lambda pp, tt: _dot(bf(pp), bf(jnp.concatenate([bd_cols(tt), bd_cols(pp)], axis=1))), p, t)
        t = _each(lambda tt, oo: tt + oo[:, :2 * L], t, o)
        p = [oo[:, 2 * L:] for oo in o]
        s *= 2
        yield
    t = _each(lambda pp, tt: tt + _dot(bf(pp), bf(bd_cols(tt))), p, t)
    yield
    z = [_dot(bf(t[ci]), bf(bd_rows2(jnp.concatenate([at(ci), akv[ci]], axis=1)))) for ci in cis]
    yield
    zeros2 = jnp.zeros((2 * L, PAIR), F32)
    o2 = [_dot(
        bf(jnp.concatenate([jnp.where(incl, sc[ci][L:, :2 * L], 0.0), jnp.where(incl, sc[ci][L:, 2 * L:], 0.0)], axis=1)),
        bf(jnp.concatenate([bd_rows2(z[ci]), jnp.concatenate([zeros2, bd_rows(v(ci))], axis=1)], axis=0)))
        for ci in cis]
    rh = [rt(ci) + o2[ci][:, :PAIR] for ci in cis]
    yh = [oo[:, PAIR:] for oo in o2]
    yield
    zeros1 = jnp.zeros((L, PAIR), F32)
    o3 = [_dot(
        bf(jnp.concatenate([bt(ci) * p_last(ci), kt(ci) * p_last(ci)], axis=0).T),
        bf(jnp.concatenate([z[ci], jnp.concatenate([zeros1, v(ci)], axis=1)], axis=0))) for ci in cis]
    r128 = lax.broadcasted_iota(jnp.int32, (PAIR, PAIR), 0)
    c128 = lax.broadcasted_iota(jnp.int32, (PAIR, PAIR), 1)
    same_head = (r128 < HEAD) == (c128 < HEAD)
    m = [jnp.where(same_head, o3[ci][:, :PAIR], 0.0) + jnp.where(r128 == c128, p_last(ci), 0.0) for ci in cis]
    g = [jnp.where(same_head, oo[:, PAIR:], 0.0) for oo in o3]
    return rh, yh, m, g


def _wkv_chain(n_chunks, first_tile, mrh_ref, g_ref, yh_ref, h_ref, ys):
    L = CHUNK
    h = jnp.where(first_tile, 0.0, h_ref[...])
    for ci in range(n_chunks):
        o = _dot(mrh_ref[ci], h.astype(BF16))
        h = o[:PAIR] + g_ref[ci]
        ys.append(o[PAIR:] + yh_ref[ci * L:(ci + 1) * L, :])
        yield
    h_ref[...] = h


def _wkv_kernel(n_chunks, tiles_per_seq,
                r_ref, lw_ref, k_ref, v_ref, av_ref, kk_ref, pvec_in_ref, pvec_out_ref, ones_ref,
                y_ref, h_ref, tilde_ref, bv_ref, mrh_ref, g_ref, yh_ref):
    s = pl.program_id(0)

    @pl.when(s == 0)
    def _():
        for ref in (h_ref, tilde_ref, bv_ref, mrh_ref, g_ref, yh_ref):
            ref[...] = jnp.zeros_like(ref)

    b_in = s % 3
    b_out = (s + 1) % 3

    def body(p0, p1):
        ones = ones_ref[...]
        ys = []
        first_tile = (s - 2) % tiles_per_seq == 0
        chain = _wkv_chain(n_chunks, first_tile, mrh_ref.at[p0], g_ref.at[p0], yh_ref.at[p0], h_ref, ys)
        half = n_chunks // 2
        pre = [_wkv_chunks_pre(0, half, tilde_ref.at[p1]),
               _wkv_chunks_pre(half, n_chunks - half, tilde_ref.at[p1])]
        done = [None, None]
        phase0 = _wkv_phase0(2 * _WKV_PRE_STAGES, r_ref, lw_ref, k_ref, v_ref, av_ref, kk_ref,
                             pvec_in_ref[0:1, :], ones, tilde_ref.at[p0], bv_ref.at[b_in])

        def advance(i):
            try:
                next(pre[i])
            except StopIteration as stop:
                done[i] = stop.value
                return
            next(chain, None)
            next(phase0, None)

        advance(0)
        while done[0] is None or done[1] is None:
            for i in (0, 1):
                if done[i] is None:
                    advance(i)
        for _ in chain:
            pass
        for _ in phase0:
            pass
        rh, yh, m, g = (done[0][j] + done[1][j] for j in range(4))

        y = jnp.concatenate(ys, axis=0)
        inv_n = 1.0 / HEAD
        d = y - _group_sum(y, ones) * inv_n
        var = _group_sum(d * d, ones) * inv_n
        y_ref[0, 0] = d * lax.rsqrt(var + GN_EPS) * pvec_out_ref[1:2, :] + pvec_out_ref[2:3, :] + bv_ref[b_out]

        for ci in range(n_chunks):
            mrh_ref[p1, ci] = jnp.concatenate([m[ci], rh[ci]], axis=0).astype(BF16)
            g_ref[p1, ci] = g[ci]
        yh_ref[p1] = jnp.concatenate(yh, axis=0)

    for parity in (0, 1):
        pl.when(s % 2 == parity)(functools.partial(body, parity, 1 - parity))


def _wkv(r, lw, k, v, av, kk, pvec, ones, tt):
    B, n_pairs, T, _ = r.shape
    n_tiles = T // tt
    n_chunks = tt // CHUNK
    steps = B * n_pairs * n_tiles

    def tile_index(i):
        return i // (n_pairs * n_tiles), (i // n_tiles) % n_pairs, i % n_tiles

    t_in = lambda s: jnp.minimum(s, steps - 1)
    t_out = lambda s: jnp.clip(s - 2, 0, steps - 1)
    act_in = pl.BlockSpec((1, 1, tt, PAIR), lambda s: tile_index(t_in(s)) + (0,))
    act_out = pl.BlockSpec((1, 1, tt, PAIR), lambda s: tile_index(t_out(s)) + (0,))
    return pl.pallas_call(
        functools.partial(_wkv_kernel, n_chunks, n_tiles),
        grid=(steps + 2,),
        in_specs=[act_in] * 6 + [pl.BlockSpec((8, PAIR), lambda s: (0, tile_index(t_in(s))[1])),
                                 pl.BlockSpec((8, PAIR), lambda s: (0, tile_index(t_out(s))[1])),
                                 pl.BlockSpec(ones.shape, lambda s: (0, 0))],
        out_specs=act_out,
        out_shape=jax.ShapeDtypeStruct((B, n_pairs, T, PAIR), F32),
        scratch_shapes=[pltpu.VMEM((PAIR, PAIR), F32),
                        pltpu.VMEM((2, 6, tt, PAIR), F32),
                        pltpu.VMEM((3, tt, PAIR), F32),
                        pltpu.VMEM((2, n_chunks, PAIR + CHUNK, PAIR), BF16),
                        pltpu.VMEM((2, n_chunks, PAIR, PAIR), F32),
                        pltpu.VMEM((2, tt, PAIR), F32)],
        compiler_params=pltpu.CompilerParams(
            dimension_semantics=("arbitrary",), vmem_limit_bytes=VMEM_LIMIT),
        name="wkv",
    )(r, lw, k, v, av, kk, pvec, pvec, ones)


def _post_mlp_kernel(has_gate, has_kv, has_q, *refs):
    refs = list(refs)
    x_ref, y_ref = refs[:2]
    del refs[:2]
    g_ref = refs.pop(0) if has_gate else None
    wo_ref, ln_ref, w1_ref, w2_ref = refs[:4]
    del refs[:4]
    if has_kv or has_q:
        ones_ref = refs.pop(0)
    if has_kv:
        kv_ln_ref, wkv_ref, kg_ref = refs[:3]
        del refs[:3]
    if has_q:
        q_ln_ref, wq_ref, qg_ref = refs[:3]
        del refs[:3]
    o_ref = refs.pop(0)
    if has_kv:
        k_ref, v_ref = refs[:2]
        del refs[:2]
    if has_q:
        q_ref = refs.pop(0)

    y = _load_pairs(y_ref)
    if has_gate:
        y = y * _load_pairs(g_ref)
    xn = x_ref[...] + _dot(y.astype(BF16), wo_ref[...])
    hid = _dot(_rms(xn, ln_ref[...]).astype(BF16), w1_ref[...])
    hid = jnp.square(jnp.maximum(hid, 0.0)).astype(BF16)
    out = xn + _dot(hid, w2_ref[...])
    o_ref[...] = out

    if has_kv:
        kv = _dot(_rms(out, kv_ln_ref[...]).astype(BF16), wkv_ref[...])
        half = kv.shape[1] // 2
        k = kv[:, :half]
        ms = _group_sum(k * k, ones_ref[...]) * (1.0 / HEAD)
        k_ref[...] = (k * lax.rsqrt(ms + RMS_EPS) * kg_ref[...]).astype(BF16)
        v_ref[...] = kv[:, half:].astype(BF16)
    if has_q:
        q = _dot(_rms(out, q_ln_ref[...]).astype(BF16), wq_ref[...])
        ms = _group_sum(q * q, ones_ref[...]) * (1.0 / HEAD)
        q_ref[...] = (q * lax.rsqrt(ms + RMS_EPS) * qg_ref[...] * (HEAD ** -0.5 * LOG2E)).astype(BF16)


def _post_mlp(x, y, g, wo, ln, w1, w2, tm, ones=None, kv=None, q=None):
    M, C = x.shape
    tiles_per_row = y.shape[2] // tm
    act = pl.BlockSpec((tm, C), lambda i: (i, 0))
    pairs = pl.BlockSpec((1, C // PAIR, tm, PAIR), lambda i: (i // tiles_per_row, 0, i % tiles_per_row, 0))
    has_gate, has_kv, has_q = g is not None, kv is not None, q is not None
    acts = [x, y] + ([g] if has_gate else [])
    consts = [wo, ln, w1, w2] + ([ones] if has_kv or has_q else []) + list(kv or ()) + list(q or ())
    out_shape = [jax.ShapeDtypeStruct((M, C), F32)]
    out_specs = [act]
    if has_kv:
        n = kv[1].shape[1] // 2
        out_shape += [jax.ShapeDtypeStruct((M, n), BF16)] * 2
        out_specs += [pl.BlockSpec((tm, n), lambda i: (i, 0))] * 2
    if has_q:
        n = q[1].shape[1]
        out_shape.append(jax.ShapeDtypeStruct((M, n), BF16))
        out_specs.append(pl.BlockSpec((tm, n), lambda i: (i, 0)))
    return pl.pallas_call(
        functools.partial(_post_mlp_kernel, has_gate, has_kv, has_q),
        grid=(M // tm,),
        in_specs=[act] + [pairs] * (len(acts) - 1) + [_weight_spec(a) for a in consts],
        out_specs=out_specs,
        out_shape=out_shape,
        compiler_params=pltpu.CompilerParams(
            dimension_semantics=("parallel",), vmem_limit_bytes=VMEM_LIMIT),
        name="post_mlp",
    )(*acts, *map(_weight_arg, consts))


def _swa_bias_init(n_q_heads, slopes_ref, bias_ref):
    W = WINDOW
    qi = lax.broadcasted_iota(jnp.int32, (W, 2 * W), 0)
    kj = lax.broadcasted_iota(jnp.int32, (W, 2 * W), 1)
    dist = qi + W - kj
    in_window = (dist >= 0) & (dist < WINDOW)
    in_window_cur = in_window & (kj >= W)
    distf = dist.astype(F32)
    for head in range(n_q_heads):
        alibi = -(slopes_ref[head] * LOG2E) * distf
        bias_ref[0, head] = jnp.where(in_window, alibi, NEG_BIG)
        bias_ref[1, head] = jnp.where(in_window_cur, alibi, NEG_BIG)


def _swa_tile(n_q_heads, first_in_row, sinks_ref, q_ref, k_refs, v_refs, bias_ref, attn_ref):
    W = WINDOW
    group = n_q_heads // N_KV
    pairs = group // 2
    lane = lax.broadcasted_iota(jnp.int32, (W, PAIR), 1)
    head0 = lane < HEAD
    lane2 = lax.broadcasted_iota(jnp.int32, (2 * W, PAIR), 1)
    head0_kv = lane2 < HEAD
    zero = jnp.zeros((), BF16)

    for qb in range(q_ref.shape[0] // W):
        rows = slice(qb * W, (qb + 1) * W)
        plane = jnp.where(first_in_row, 1, 0) if qb == 0 else 0
        for h in range(N_KV):
            ks = slice(h * PAIR, (h + 1) * PAIR)
            k2 = jnp.concatenate([k_refs[qb][:, ks], k_refs[qb + 1][:, ks]], axis=0)
            v2 = jnp.concatenate([v_refs[qb][:, ks], v_refs[qb + 1][:, ks]], axis=0)
            q_tiles = [q_ref[rows, (h * pairs + j) * PAIR:(h * pairs + j + 1) * PAIR] for j in range(pairs)]
            lhs = jnp.concatenate([jnp.where(head0, t, zero) for t in q_tiles]
                                  + [jnp.where(head0, zero, t) for t in q_tiles], axis=0)
            s_all = _dot_nt(lhs, k2)
            yield
            probs, denoms = [], []
            for i in range(group):
                head = h * group + 2 * (i % pairs) + i // pairs
                s = s_all[i * W:(i + 1) * W, :] + bias_ref[plane, head]
                sink = sinks_ref[head] * LOG2E
                mx = jnp.maximum(jnp.max(s, axis=-1, keepdims=True), sink)
                p = jnp.exp2(s - mx)
                denoms.append(jnp.sum(p, axis=-1, keepdims=True) + jnp.exp2(sink - mx))
                probs.append(p.astype(BF16))
            p_first = jnp.concatenate(probs[:pairs], axis=0)
            p_second = jnp.concatenate(probs[pairs:], axis=0)
            v_stack = jnp.concatenate([jnp.where(head0_kv, v2, zero), jnp.where(head0_kv, zero, v2)], axis=0)
            o = _dot(jnp.concatenate([p_first, p_second], axis=1), v_stack)
            for j in range(pairs):
                den = jnp.where(head0, denoms[j], denoms[pairs + j])
                col = (h * pairs + j) * PAIR
                attn_ref[rows, col:col + PAIR] = (o[j * W:(j + 1) * W, :] / den).astype(attn_ref.dtype)


_MLP_SPLIT = 4


def _mlp_stages(x_ref, attn_ref, wo_ref, ln_ref, w1_ref, w2_ref, result):
    xn = x_ref[...] + _dot(attn_ref[...], wo_ref[...])
    h = _rms(xn, ln_ref[...]).astype(BF16)
    yield
    step = w1_ref.shape[1] // _MLP_SPLIT
    hids = []
    for c in range(_MLP_SPLIT):
        hid = _dot(h, w1_ref[:, c * step:(c + 1) * step])
        hids.append(jnp.square(jnp.maximum(hid, 0.0)).astype(BF16))
        yield
    acc = xn
    for c in range(_MLP_SPLIT):
        acc = acc + _dot(hids[c], w2_ref[c * step:(c + 1) * step, :])
        if c + 1 < _MLP_SPLIT:
            yield
    result.append(acc)


_MLP_STAGES = 2 * _MLP_SPLIT


def _swa_mlp_kernel(has_q, n_q_heads, blocks, tiles_per_row, slopes_ref, sinks_ref, *refs):
    refs = list(refs)
    x_ref, q_ref = refs[:2]
    k_refs = refs[2:3 + blocks]
    v_refs = refs[3 + blocks:4 + 2 * blocks]
    del refs[:4 + 2 * blocks]
    wo_ref, ln_ref, w1_ref, w2_ref = refs[:4]
    del refs[:4]
    if has_q:
        ones_ref, q_ln_ref, wq_ref, qg_ref = refs[:4]
        del refs[:4]
    o_ref = refs.pop(0)
    qn_ref = refs.pop(0) if has_q else None
    attn_ref, bias_ref = refs
    s = pl.program_id(0)

    @pl.when(s == 0)
    def _():
        attn_ref[...] = jnp.zeros_like(attn_ref)
        _swa_bias_init(n_q_heads, slopes_ref, bias_ref)

    cur = s % 2
    first_in_row = s % tiles_per_row == 0
    result = []
    swa = _swa_tile(n_q_heads, first_in_row, sinks_ref, q_ref, k_refs, v_refs, bias_ref, attn_ref.at[cur])
    mlp = _mlp_stages(x_ref, attn_ref.at[1 - cur], wo_ref, ln_ref, w1_ref, w2_ref, result)
    mlp_stages_per_piece = -(-_MLP_STAGES // (blocks * N_KV))
    for _ in swa:
        for _ in range(mlp_stages_per_piece):
            next(mlp, None)
    for _ in mlp:
        pass
    out = result[0]
    o_ref[...] = out
    if has_q:
        q = _dot(_rms(out, q_ln_ref[...]).astype(BF16), wq_ref[...])
        ms = _group_sum(q * q, ones_ref[...]) * (1.0 / HEAD)
        qn_ref[...] = (q * lax.rsqrt(ms + RMS_EPS) * qg_ref[...] * (HEAD ** -0.5 * LOG2E)).astype(BF16)


def _swa_mlp(x, q, k2, v2, slopes, sinks, wo, ln, w1, w2, tm, tokens_per_row, ones=None, q_next=None):
    M, C = x.shape
    W = WINDOW
    steps = M // tm
    blocks = tm // W
    assert tokens_per_row % tm == 0
    has_q = q_next is not None
    t_att = lambda s: jnp.minimum(s, steps - 1)
    t_mlp = lambda s: jnp.maximum(s - 1, 0)
    kv_spec = lambda off: pl.BlockSpec(
        (W, k2.shape[1]), lambda s, *_: (jnp.maximum(t_att(s) * blocks + off, 0), 0))
    act_mlp = pl.BlockSpec((tm, C), lambda s, *_: (t_mlp(s), 0))
    consts = [wo, ln, w1, w2] + ([ones] + list(q_next) if has_q else [])
    out_shape = [jax.ShapeDtypeStruct((M, C), F32)]
    out_specs = [act_mlp]
    if has_q:
        out_shape.append(jax.ShapeDtypeStruct((M, q_next[1].shape[1]), BF16))
        out_specs.append(pl.BlockSpec((tm, q_next[1].shape[1]), lambda s, *_: (t_mlp(s), 0)))
    return pl.pallas_call(
        functools.partial(_swa_mlp_kernel, has_q, C // HEAD, blocks, tokens_per_row // tm),
        grid_spec=pltpu.PrefetchScalarGridSpec(
            num_scalar_prefetch=2,
            grid=(steps + 1,),
            in_specs=[act_mlp, pl.BlockSpec((tm, C), lambda s, *_: (t_att(s), 0)),
                      *[kv_spec(off) for off in range(-1, blocks)] * 2]
                     + [_weight_spec(a) for a in consts],
            out_specs=out_specs,
            scratch_shapes=[pltpu.VMEM((2, tm, C), BF16),
                            pltpu.VMEM((2, C // HEAD, W, 2 * W), F32)],
        ),
        out_shape=out_shape,
        compiler_params=pltpu.CompilerParams(
            dimension_semantics=("arbitrary",), vmem_limit_bytes=VMEM_LIMIT),
        name="swa_mlp",
    )(slopes, sinks, x, q, *[k2] * (blocks + 1), *[v2] * (blocks + 1), *map(_weight_arg, consts))


def _pad_cols(w, n):
    return jnp.pad(w, ((0, 0), (0, n - w.shape[1])))


def _pad_rows(w, n):
    return jnp.pad(w, ((0, n - w.shape[0]), (0, 0)))


def _round_up(n, m):
    return (n + m - 1) // m * m


def _lora(w_in, w_out):
    n = _round_up(w_in.shape[1], 128)
    return _pad_cols(w_in, n).astype(BF16), _pad_rows(w_out, n).astype(BF16)


def _dup_heads(w):
    c, n = w.shape
    w = w.reshape(c, n // HEAD, 1, HEAD)
    return jnp.broadcast_to(w, (c, n // HEAD, 2, HEAD)).reshape(c, 2 * n)


def _block_ones(n):
    i = jnp.arange(n) // HEAD
    return (i[:, None] == i[None, :]).astype(BF16)


def kernel(x, ln_mix, ln_mlp, mlp_w1, mlp_w2, a_mu, a_w_rkv, a_w0, a_w1, a_w2, a_a0, a_a1, a_a2, a_g1, a_g2, a_k_k, a_k_a, a_r_k, a_gn_g, a_gn_b, a_wo, a_v0, a_v1, a_v2, kv_norm, w_kv, k_gain, b_wq, b_q_gain, b_sinks, b_wo):
    B, T, C = x.shape
    M = B * T
    n_a = a_mu.shape[0]
    n_b = b_wq.shape[0]
    n_heads = C // HEAD
    tm_mlp = 512
    tm_pre = 512
    tt = 1024
    ones256 = _block_ones(256)
    ones128 = _block_ones(PAIR)
    row = lambda v: v.reshape(1, -1).astype(F32)
    slopes = jnp.exp2(-8.0 * jnp.arange(1, n_heads + 1, dtype=F32) / n_heads)
    w1_all = mlp_w1.astype(BF16)
    w2_all = mlp_w2.astype(BF16)

    def q_params(i):
        j = i - n_a
        return row(ln_mix[i]), b_wq[j].astype(BF16), jnp.tile(row(b_q_gain[j]), (1, n_heads))

    def mlp(i, x, y, g, wo):
        kv = q = None
        if i == n_a - 1:
            kv = (row(kv_norm), _dup_heads(w_kv).astype(BF16), jnp.tile(row(k_gain), (1, 2 * N_KV)))
        if n_a - 1 <= i < n_a + n_b - 1:
            q = q_params(i + 1)
        return _post_mlp(x.reshape(M, C), y, g, wo.astype(BF16), row(ln_mlp[i]), _Layer(w1_all, i), _Layer(w2_all, i),
                         tm_mlp, ones256, kv, q)

    v_first = None
    k2 = v2 = q = None
    for i in range(n_a + n_b):
        if i < n_a:
            j = i
            zeros = jnp.zeros((1, C), F32)
            vecs = jnp.concatenate(
                [a_mu[j], row(ln_mix[i]), row(a_w0[j]), row(a_a0[j]),
                 row(a_v0[j - 1]) if j > 0 else zeros, row(a_k_k[j]), row(a_k_a[j])]
                + [zeros] * 4, axis=0)
            loras = list(_lora(a_w1[j], a_w2[j]) + _lora(a_a1[j], a_a2[j]) + _lora(a_g1[j], a_g2[j]))
            if j > 0:
                loras += list(_lora(a_v1[j - 1], a_v2[j - 1]))
            r, lw, k, v, av, kk, g = _rwkv_pre(
                x, v_first if j > 0 else None, vecs, a_w_rkv[j].astype(BF16), loras, ones256, tm_pre)
            if j == 0:
                v_first = v
            pvec = jnp.concatenate([row(a_r_k[j]), row(a_gn_g[j]), row(a_gn_b[j])]
                                   + [jnp.zeros((1, C), F32)] * 5, axis=0)
            y = _wkv(r, lw, k, v, av, kk, pvec, ones128, tt)
            outs = mlp(i, x, y, g, a_wo[j])
        else:
            outs = _swa_mlp(x.reshape(M, C), q, k2, v2, slopes, b_sinks[i - n_a].astype(F32),
                            b_wo[i - n_a].astype(BF16), row(ln_mlp[i]), _Layer(w1_all, i), _Layer(w2_all, i),
                            tm_mlp, T, ones256, q_params(i + 1) if i + 1 < n_a + n_b else None)
        outs = list(outs)
        x = outs.pop(0).reshape(B, T, C)
        if i == n_a - 1:
            k2 = outs.pop(0)
            v2 = outs.pop(0)
        if outs:
            q = outs.pop(0)
    return x
```

```python
import functools
import math

import jax
import jax.numpy as jnp
from jax import lax
from jax.experimental import pallas as pl
from jax.experimental.pallas import tpu as pltpu

F32 = jnp.float32
BF16 = jnp.bfloat16

HEAD = 64
PAIR = 2 * HEAD
N_KV = 2
WINDOW = 128
CHUNK = 64
GN_EPS = 64e-5
RMS_EPS = 1e-6
NEG_BIG = -1e30
LOG2E = math.log2(math.e)
VMEM_LIMIT = 60 * 1024 * 1024

NT_DIMS = (((1,), (1,)), ((), ()))


def _dot(a, b):
    return jnp.dot(a, b, preferred_element_type=F32)


def _dot_nt(a, b):
    return lax.dot_general(a, b, NT_DIMS, preferred_element_type=F32)


def _rms(x, g):
    return x * lax.rsqrt(jnp.mean(x * x, axis=-1, keepdims=True) + RMS_EPS) * g


def _group_sum(x, ones):
    w = ones.shape[0]
    parts = [_dot(x[:, j:j + w].astype(BF16), ones) for j in range(0, x.shape[1], w)]
    return parts[0] if len(parts) == 1 else jnp.concatenate(parts, axis=1)


def _pair_group_sum_lanes(x):
    head0 = lax.broadcasted_iota(jnp.int32, x.shape, 1) < HEAD
    s0 = jnp.sum(jnp.where(head0, x, 0.0), axis=-1, keepdims=True)
    s1 = jnp.sum(jnp.where(head0, 0.0, x), axis=-1, keepdims=True)
    return jnp.where(head0, s0, s1)


def _sigmoid(x):
    return 0.5 * jnp.tanh(0.5 * x) + 0.5


def _store_pairs(ref, x):
    for p in range(x.shape[1] // PAIR):
        ref[0, p] = x[:, p * PAIR:(p + 1) * PAIR]


def _load_pairs(ref):
    return jnp.concatenate([ref[0, p] for p in range(ref.shape[1])], axis=1)


def _const_spec(shape):
    nd = len(shape)
    return pl.BlockSpec(shape, lambda *_: (0,) * nd, pipeline_mode=pl.Buffered(1))


class _Layer:
    def __init__(self, stacked, layer):
        self.stacked, self.layer, self.shape = stacked, layer, stacked.shape[1:]


def _weight_spec(w):
    if isinstance(w, _Layer):
        nd = len(w.shape)
        return pl.BlockSpec((None,) + w.shape, lambda *_: (w.layer,) + (0,) * nd, pipeline_mode=pl.Buffered(1))
    return _const_spec(w.shape)


def _weight_arg(w):
    return w.stacked if isinstance(w, _Layer) else w


_V_MU, _V_LN, _V_W0, _V_A0, _V_V0, _V_KK, _V_KA = 0, 6, 7, 8, 9, 10, 11


def _rwkv_pre_kernel(has_vres, tm, *refs):
    if has_vres:
        (x_ref, vf_ref, vec_ref, wrkv_ref, w1_ref, w2_ref, a1_ref, a2_ref, g1_ref, g2_ref,
         v1_ref, v2_ref, ones_ref,
         r_ref, lw_ref, k_ref, v_ref, av_ref, kk_ref, g_ref, carry_ref) = refs
    else:
        (x_ref, vec_ref, wrkv_ref, w1_ref, w2_ref, a1_ref, a2_ref, g1_ref, g2_ref, ones_ref,
         r_ref, lw_ref, k_ref, v_ref, av_ref, kk_ref, g_ref, carry_ref) = refs

    @pl.when(pl.program_id(1) == 0)
    def _():
        carry_ref[...] = jnp.zeros_like(carry_ref)

    def vec(i):
        return vec_ref[i:i + 1, :]

    h = _rms(x_ref[0], vec(_V_LN))
    row = lax.broadcasted_iota(jnp.int32, h.shape, 0)
    prev = jnp.where(row == 0, carry_ref[0:1, :], pltpu.roll(h, 1, axis=0))
    carry_ref[0:1, :] = h[tm - 1:tm, :]
    xx = prev - h

    def mix(i):
        return (h + xx * vec(_V_MU + i)).astype(BF16)

    r = _dot(mix(0), wrkv_ref[0])
    k = _dot(mix(1), wrkv_ref[1])
    xv = mix(2)
    v = _dot(xv, wrkv_ref[2])

    wl = vec(_V_W0) + _dot(jnp.tanh(_dot(mix(3), w1_ref[...])).astype(BF16), w2_ref[...])
    _store_pairs(lw_ref, -math.exp(-0.5) * _sigmoid(wl))

    if has_vres:
        gate = _sigmoid(vec(_V_V0) + _dot(_dot(xv, v1_ref[...]).astype(BF16), v2_ref[...]))
        v = v + (_load_pairs(vf_ref) - v) * gate
    a = _sigmoid(vec(_V_A0) + _dot(_dot(mix(4), a1_ref[...]).astype(BF16), a2_ref[...]))
    _store_pairs(g_ref, _dot(_sigmoid(_dot(mix(5), g1_ref[...])).astype(BF16), g2_ref[...]))

    kk = k * vec(_V_KK)
    _store_pairs(kk_ref, kk * jnp.minimum(lax.rsqrt(_group_sum(kk * kk, ones_ref[...])), 1e12))
    _store_pairs(k_ref, k * (1.0 + (a - 1.0) * vec(_V_KA)))
    _store_pairs(r_ref, r)
    _store_pairs(v_ref, v)
    _store_pairs(av_ref, a)


def _rwkv_pre(x, v_first, vecs, wrkv, loras, ones, tm):
    B, T, C = x.shape
    has_vres = v_first is not None
    act = pl.BlockSpec((1, tm, C), lambda b, t: (b, t, 0))
    pairs = pl.BlockSpec((1, C // PAIR, tm, PAIR), lambda b, t: (b, 0, t, 0))
    ins = [x] + ([v_first] if has_vres else []) + [vecs, wrkv] + list(loras) + [ones]
    in_specs = [act] + ([pairs] if has_vres else []) + [_const_spec(a.shape) for a in ins[(2 if has_vres else 1):]]
    out = jax.ShapeDtypeStruct((B, C // PAIR, T, PAIR), F32)
    return pl.pallas_call(
        functools.partial(_rwkv_pre_kernel, has_vres, tm),
        grid=(B, T // tm),
        in_specs=in_specs,
        out_specs=[pairs] * 7,
        out_shape=[out] * 7,
        scratch_shapes=[pltpu.VMEM((8, C), F32)],
        compiler_params=pltpu.CompilerParams(
            dimension_semantics=("parallel", "arbitrary"), vmem_limit_bytes=VMEM_LIMIT),
        name="rwkv_pre",
    )(*ins)


def _chunk_cumsum(x):
    pos = lax.broadcasted_iota(jnp.int32, x.shape, 0) & (CHUNK - 1)
    s = 1
    while s < CHUNK:
        x = x + jnp.where(pos >= s, pltpu.roll(x, s, axis=0), 0.0)
        s *= 2
    return x


def _each(f, *lists):
    return [f(*xs) for xs in zip(*lists)]


_WKV_PRE_STAGES = 8
_T_AT, _T_RT, _T_KT, _T_BT, _T_V, _T_EC = range(6)


def _wkv_phase0(n_blocks, never, r_ref, lw_ref, k_ref, v_ref, av_ref, kk_ref, rk, ones, tilde_ref, bv_ref):
    rows = r_ref.shape[2] // n_blocks
    for i in range(n_blocks):
        after = yield
        sl = slice(i * rows, (i + 1) * rows)
        r, lw, k, v, kk = r_ref[0, 0, sl], lw_ref[0, 0, sl], k_ref[0, 0, sl], v_ref[0, 0, sl], kk_ref[0, 0, sl]
        if after is not None:
            lw = lw + jnp.where(never, jnp.concatenate([after] * (rows // after.shape[0]), axis=0), 0.0)
        c = _chunk_cumsum(lw)
        ec = jnp.exp(c)
        enc = jnp.exp(-c)
        tilde_ref[_T_AT, sl] = -kk * jnp.exp(c - lw)
        tilde_ref[_T_RT, sl] = r * ec
        tilde_ref[_T_KT, sl] = k * enc
        tilde_ref[_T_BT, sl] = kk * av_ref[0, 0, sl] * enc
        tilde_ref[_T_V, sl] = v
        tilde_ref[_T_EC, sl] = ec
        bv_ref[sl] = _group_sum(r * k * rk, ones) * v
    yield


def _wkv_chunks_pre(first, n_chunks, tilde_ref):
    L = CHUNK
    bf = lambda t: t.astype(BF16)
    lane = lax.broadcasted_iota(jnp.int32, (L, PAIR), 1)
    head0 = lane < HEAD
    row = lax.broadcasted_iota(jnp.int32, (L, 2 * L), 0)
    col = lax.broadcasted_iota(jnp.int32, (L, 2 * L), 1)
    src = jnp.where(col >= L, col - L, col)
    strict = src < row
    incl = src <= row
    left = col < L
    eye = jnp.where(src == row, 1.0, 0.0)
    cis = list(range(n_chunks))
    mid = n_chunks // 2 - 1

    def plane(which):
        vals = [tilde_ref[which, (first + ci) * L:(first + ci + 1) * L, :] for ci in cis]
        return lambda ci: vals[ci]

    at, rt, kt, bt, v = plane(_T_AT), plane(_T_RT), plane(_T_KT), plane(_T_BT), plane(_T_V)
    p_last_vals = [tilde_ref[_T_EC, (first + ci + 1) * L - 1:(first + ci + 1) * L, :] for ci in cis]
    p_last = lambda ci: p_last_vals[ci]

    def bd_rows(x):
        return jnp.concatenate([jnp.where(head0, x, 0.0), jnp.where(head0, 0.0, x)], axis=0)

    def bd_cols(x):
        return jnp.concatenate([jnp.where(left, x, 0.0), jnp.where(left, 0.0, x)], axis=0)

    def bd_rows2(x):
        return jnp.concatenate([bd_rows(x[:, :PAIR]), bd_rows(x[:, PAIR:])], axis=1)

    sc = [_dot_nt(bf(jnp.concatenate([at(ci), rt(ci)], axis=0)),
                  bf(jnp.concatenate([bd_rows(bt(ci)), bd_rows(kt(ci))], axis=0))) for ci in cis]
    yield sc[mid][:L, :PAIR], sc[-1][:L, :PAIR]
    a_ab = [jnp.where(strict, s[:L, :2 * L], 0.0) for s in sc]
    akv = [_dot(bf(jnp.where(strict, sc[ci][:L, 2 * L:], 0.0)), bf(bd_rows(v(ci)))) for ci in cis]
    p = _each(lambda a: _dot(bf(a), bf(bd_cols(a))), a_ab)
    t = [eye + a for a in a_ab]
    yield p[mid], p[-1]
    s = 2
    while 2 * s < L:
        o = _each(lambda pp, tt: _dot(bf(pp), bf(jnp.concatenate([bd_cols(tt), bd_cols(pp)], axis=1))), p, t)
        t = _each(lambda tt, oo: tt + oo[:, :2 * L], t, o)
        p = [oo[:, 2 * L:] for oo in o]
        s *= 2
        yield p[mid], p[-1]
    t = _each(lambda pp, tt: tt + _dot(bf(pp), bf(bd_cols(tt))), p, t)
    yield t[mid], t[-1]
    z = [_dot(bf(t[ci]), bf(bd_rows2(jnp.concatenate([at(ci), akv[ci]], axis=1)))) for ci in cis]
    yield z[mid][:, :PAIR], z[-1][:, :PAIR]
    zeros2 = jnp.zeros((2 * L, PAIR), F32)
    o2 = [_dot(
        bf(jnp.concatenate([jnp.where(incl, sc[ci][L:, :2 * L], 0.0), jnp.where(incl, sc[ci][L:, 2 * L:], 0.0)], axis=1)),
        bf(jnp.concatenate([bd_rows2(z[ci]), jnp.concatenate([zeros2, bd_rows(v(ci))], axis=1)], axis=0)))
        for ci in cis]
    rh = [rt(ci) + o2[ci][:, :PAIR] for ci in cis]
    yh = [oo[:, PAIR:] for oo in o2]
    yield yh[mid], yh[-1]
    zeros1 = jnp.zeros((L, PAIR), F32)
    o3 = [_dot(
        bf(jnp.concatenate([bt(ci) * p_last(ci), kt(ci) * p_last(ci)], axis=0).T),
        bf(jnp.concatenate([z[ci], jnp.concatenate([zeros1, v(ci)], axis=1)], axis=0))) for ci in cis]
    r128 = lax.broadcasted_iota(jnp.int32, (PAIR, PAIR), 0)
    c128 = lax.broadcasted_iota(jnp.int32, (PAIR, PAIR), 1)
    same_head = (r128 < HEAD) == (c128 < HEAD)
    m = [jnp.where(same_head, o3[ci][:, :PAIR], 0.0) + jnp.where(r128 == c128, p_last(ci), 0.0) for ci in cis]
    g = [jnp.where(same_head, oo[:, PAIR:], 0.0) for oo in o3]
    return rh, yh, m, g


def _wkv_chain(n_chunks, first_tile, mrh_ref, g_ref, yh_ref, h_ref, ys):
    L = CHUNK
    h = jnp.where(first_tile, 0.0, h_ref[...])
    for ci in range(n_chunks):
        o = _dot(mrh_ref[ci], h.astype(BF16))
        h = o[:PAIR] + g_ref[ci]
        ys.append(o[PAIR:] + yh_ref[ci * L:(ci + 1) * L, :])
        yield
    h_ref[...] = h


def _wkv_kernel(n_chunks, tiles_per_seq,
                r_ref, lw_ref, k_ref, v_ref, av_ref, kk_ref, pvec_in_ref, pvec_out_ref, ones_ref,
                y_ref, h_ref, tilde_ref, bv_ref, mrh_ref, g_ref, yh_ref):
    s = pl.program_id(0)

    @pl.when(s == 0)
    def _():
        for ref in (h_ref, tilde_ref, bv_ref, mrh_ref, g_ref, yh_ref):
            ref[...] = jnp.zeros_like(ref)

    b_in = s % 3
    b_out = (s + 1) % 3

    def body(p0, p1):
        ones = ones_ref[...]
        ys = []
        first_tile = (s - 2) % tiles_per_seq == 0
        chain = _wkv_chain(n_chunks, first_tile, mrh_ref.at[p0], g_ref.at[p0], yh_ref.at[p0], h_ref, ys)
        half = n_chunks // 2
        pre = [_wkv_chunks_pre(0, half, tilde_ref.at[p1]),
               _wkv_chunks_pre(half, n_chunks - half, tilde_ref.at[p1])]
        done = [None, None]
        phase0 = _wkv_phase0(2 * _WKV_PRE_STAGES, s < 0, r_ref, lw_ref, k_ref, v_ref, av_ref, kk_ref,
                             pvec_in_ref[0:1, :], ones, tilde_ref.at[p0], bv_ref.at[b_in])
        next(phase0)
        blocks_left = [2 * _WKV_PRE_STAGES]

        def advance(i):
            try:
                deps = next(pre[i])
            except StopIteration as stop:
                done[i] = stop.value
                return
            next(chain, None)
            if blocks_left[0]:
                phase0.send(deps[-1])
                blocks_left[0] -= 1

        advance(0)
        while done[0] is None or done[1] is None:
            for i in (0, 1):
                if done[i] is None:
                    advance(i)
        for _ in chain:
            pass
        assert blocks_left[0] == 0
        rh, yh, m, g = (done[0][j] + done[1][j] for j in range(4))

        y = jnp.concatenate(ys, axis=0)
        inv_n = 1.0 / HEAD
        d = y - _pair_group_sum_lanes(y) * inv_n
        var = _pair_group_sum_lanes(d * d) * inv_n
        y_ref[0, 0] = d * lax.rsqrt(var + GN_EPS) * pvec_out_ref[1:2, :] + pvec_out_ref[2:3, :] + bv_ref[b_out]

        for ci in range(n_chunks):
            mrh_ref[p1, ci] = jnp.concatenate([m[ci], rh[ci]], axis=0).astype(BF16)
            g_ref[p1, ci] = g[ci]
        yh_ref[p1] = jnp.concatenate(yh, axis=0)

    for parity in (0, 1):
        pl.when(s % 2 == parity)(functools.partial(body, parity, 1 - parity))


def _wkv(r, lw, k, v, av, kk, pvec, ones, tt):
    B, n_pairs, T, _ = r.shape
    n_tiles = T // tt
    n_chunks = tt // CHUNK
    steps = B * n_pairs * n_tiles

    def tile_index(i):
        return i // (n_pairs * n_tiles), (i // n_tiles) % n_pairs, i % n_tiles

    t_in = lambda s: jnp.minimum(s, steps - 1)
    t_out = lambda s: jnp.clip(s - 2, 0, steps - 1)
    act_in = pl.BlockSpec((1, 1, tt, PAIR), lambda s: tile_index(t_in(s)) + (0,))
    act_out = pl.BlockSpec((1, 1, tt, PAIR), lambda s: tile_index(t_out(s)) + (0,))
    return pl.pallas_call(
        functools.partial(_wkv_kernel, n_chunks, n_tiles),
        grid=(steps + 2,),
        in_specs=[act_in] * 6 + [pl.BlockSpec((8, PAIR), lambda s: (0, tile_index(t_in(s))[1])),
                                 pl.BlockSpec((8, PAIR), lambda s: (0, tile_index(t_out(s))[1])),
                                 pl.BlockSpec(ones.shape, lambda s: (0, 0))],
        out_specs=act_out,
        out_shape=jax.ShapeDtypeStruct((B, n_pairs, T, PAIR), F32),
        scratch_shapes=[pltpu.VMEM((PAIR, PAIR), F32),
                        pltpu.VMEM((2, 6, tt, PAIR), F32),
                        pltpu.VMEM((3, tt, PAIR), F32),
                        pltpu.VMEM((2, n_chunks, PAIR + CHUNK, PAIR), BF16),
                        pltpu.VMEM((2, n_chunks, PAIR, PAIR), F32),
                        pltpu.VMEM((2, tt, PAIR), F32)],
        compiler_params=pltpu.CompilerParams(
            dimension_semantics=("arbitrary",), vmem_limit_bytes=VMEM_LIMIT),
        name="wkv",
    )(r, lw, k, v, av, kk, pvec, pvec, ones)


def _post_mlp_kernel(has_gate, has_kv, has_q, *refs):
    refs = list(refs)
    x_ref, y_ref = refs[:2]
    del refs[:2]
    g_ref = refs.pop(0) if has_gate else None
    wo_ref, ln_ref, w1_ref, w2_ref = refs[:4]
    del refs[:4]
    if has_kv or has_q:
        ones_ref = refs.pop(0)
    if has_kv:
        kv_ln_ref, wkv_ref, kg_ref = refs[:3]
        del refs[:3]
    if has_q:
        q_ln_ref, wq_ref, qg_ref = refs[:3]
        del refs[:3]
    o_ref = refs.pop(0)
    if has_kv:
        k_ref, v_ref = refs[:2]
        del refs[:2]
    if has_q:
        q_ref = refs.pop(0)

    y = _load_pairs(y_ref)
    if has_gate:
        y = y * _load_pairs(g_ref)
    xn = x_ref[...] + _dot(y.astype(BF16), wo_ref[...])
    hid = _dot(_rms(xn, ln_ref[...]).astype(BF16), w1_ref[...])
    hid = jnp.square(jnp.maximum(hid, 0.0)).astype(BF16)
    out = xn + _dot(hid, w2_ref[...])
    o_ref[...] = out

    if has_kv:
        kv = _dot(_rms(out, kv_ln_ref[...]).astype(BF16), wkv_ref[...])
        half = kv.shape[1] // 2
        k = kv[:, :half]
        ms = _group_sum(k * k, ones_ref[...]) * (1.0 / HEAD)
        k_ref[...] = (k * lax.rsqrt(ms + RMS_EPS) * kg_ref[...]).astype(BF16)
        v_ref[...] = kv[:, half:].astype(BF16)
    if has_q:
        q = _dot(_rms(out, q_ln_ref[...]).astype(BF16), wq_ref[...])
        ms = _group_sum(q * q, ones_ref[...]) * (1.0 / HEAD)
        q_ref[...] = (q * lax.rsqrt(ms + RMS_EPS) * qg_ref[...] * (HEAD ** -0.5 * LOG2E)).astype(BF16)


def _post_mlp(x, y, g, wo, ln, w1, w2, tm, ones=None, kv=None, q=None):
    M, C = x.shape
    tiles_per_row = y.shape[2] // tm
    act = pl.BlockSpec((tm, C), lambda i: (i, 0))
    pairs = pl.BlockSpec((1, C // PAIR, tm, PAIR), lambda i: (i // tiles_per_row, 0, i % tiles_per_row, 0))
    has_gate, has_kv, has_q = g is not None, kv is not None, q is not None
    acts = [x, y] + ([g] if has_gate else [])
    consts = [wo, ln, w1, w2] + ([ones] if has_kv or has_q else []) + list(kv or ()) + list(q or ())
    out_shape = [jax.ShapeDtypeStruct((M, C), F32)]
    out_specs = [act]
    if has_kv:
        n = kv[1].shape[1] // 2
        out_shape += [jax.ShapeDtypeStruct((M, n), BF16)] * 2
        out_specs += [pl.BlockSpec((tm, n), lambda i: (i, 0))] * 2
    if has_q:
        n = q[1].shape[1]
        out_shape.append(jax.ShapeDtypeStruct((M, n), BF16))
        out_specs.append(pl.BlockSpec((tm, n), lambda i: (i, 0)))
    return pl.pallas_call(
        functools.partial(_post_mlp_kernel, has_gate, has_kv, has_q),
        grid=(M // tm,),
        in_specs=[act] + [pairs] * (len(acts) - 1) + [_weight_spec(a) for a in consts],
        out_specs=out_specs,
        out_shape=out_shape,
        compiler_params=pltpu.CompilerParams(
            dimension_semantics=("parallel",), vmem_limit_bytes=VMEM_LIMIT),
        name="post_mlp",
    )(*acts, *map(_weight_arg, consts))


def _swa_bias_init(n_q_heads, slopes_ref, bias_ref):
    W = WINDOW
    qi = lax.broadcasted_iota(jnp.int32, (W, 2 * W), 0)
    kj = lax.broadcasted_iota(jnp.int32, (W, 2 * W), 1)
    dist = qi + W - kj
    in_window = (dist >= 0) & (dist < WINDOW)
    in_window_cur = in_window & (kj >= W)
    distf = dist.astype(F32)
    for head in range(n_q_heads):
        alibi = -(slopes_ref[head] * LOG2E) * distf
        bias_ref[0, head] = jnp.where(in_window, alibi, NEG_BIG)
        bias_ref[1, head] = jnp.where(in_window_cur, alibi, NEG_BIG)


def _swa_tile(n_q_heads, first_in_row, sinks_ref, q_ref, k_refs, v_refs, bias_ref, attn_ref):
    W = WINDOW
    group = n_q_heads // N_KV
    pairs = group // 2
    lane = lax.broadcasted_iota(jnp.int32, (W, PAIR), 1)
    head0 = lane < HEAD
    lane2 = lax.broadcasted_iota(jnp.int32, (2 * W, PAIR), 1)
    head0_kv = lane2 < HEAD
    zero = jnp.zeros((), BF16)

    for qb in range(q_ref.shape[0] // W):
        rows = slice(qb * W, (qb + 1) * W)
        plane = jnp.where(first_in_row, 1, 0) if qb == 0 else 0
        for h in range(N_KV):
            ks = slice(h * PAIR, (h + 1) * PAIR)
            k2 = jnp.concatenate([k_refs[qb][:, ks], k_refs[qb + 1][:, ks]], axis=0)
            v2 = jnp.concatenate([v_refs[qb][:, ks], v_refs[qb + 1][:, ks]], axis=0)
            q_tiles = [q_ref[rows, (h * pairs + j) * PAIR:(h * pairs + j + 1) * PAIR] for j in range(pairs)]
            lhs = jnp.concatenate([jnp.where(head0, t, zero) for t in q_tiles]
                                  + [jnp.where(head0, zero, t) for t in q_tiles], axis=0)
            s_all = _dot_nt(lhs, k2)
            yield
            probs, denoms = [], []
            for i in range(group):
                head = h * group + 2 * (i % pairs) + i // pairs
                s = s_all[i * W:(i + 1) * W, :] + bias_ref[plane, head]
                sink = sinks_ref[head] * LOG2E
                mx = jnp.maximum(jnp.max(s, axis=-1, keepdims=True), sink)
                p = jnp.exp2(s - mx)
                denoms.append(jnp.sum(p, axis=-1, keepdims=True) + jnp.exp2(sink - mx))
                probs.append(p.astype(BF16))
            p_first = jnp.concatenate(probs[:pairs], axis=0)
            p_second = jnp.concatenate(probs[pairs:], axis=0)
            v_stack = jnp.concatenate([jnp.where(head0_kv, v2, zero), jnp.where(head0_kv, zero, v2)], axis=0)
            o = _dot(jnp.concatenate([p_first, p_second], axis=1), v_stack)
            for j in range(pairs):
                den = jnp.where(head0, denoms[j], denoms[pairs + j])
                col = (h * pairs + j) * PAIR
                attn_ref[rows, col:col + PAIR] = (o[j * W:(j + 1) * W, :] / den).astype(attn_ref.dtype)


_MLP_SPLIT = 4


def _mlp_stages(x_ref, attn_ref, wo_ref, ln_ref, w1_ref, w2_ref, result):
    xn = x_ref[...] + _dot(attn_ref[...], wo_ref[...])
    h = _rms(xn, ln_ref[...]).astype(BF16)
    yield
    step = w1_ref.shape[1] // _MLP_SPLIT
    hids = []
    for c in range(_MLP_SPLIT):
        hid = _dot(h, w1_ref[:, c * step:(c + 1) * step])
        hids.append(jnp.square(jnp.maximum(hid, 0.0)).astype(BF16))
        yield
    acc = xn
    for c in range(_MLP_SPLIT):
        acc = acc + _dot(hids[c], w2_ref[c * step:(c + 1) * step, :])
        if c + 1 < _MLP_SPLIT:
            yield
    result.append(acc)


_MLP_STAGES = 2 * _MLP_SPLIT


def _swa_mlp_kernel(has_q, n_q_heads, blocks, tiles_per_row, slopes_ref, sinks_ref, *refs):
    refs = list(refs)
    x_ref, q_ref = refs[:2]
    k_refs = refs[2:3 + blocks]
    v_refs = refs[3 + blocks:4 + 2 * blocks]
    del refs[:4 + 2 * blocks]
    wo_ref, ln_ref, w1_ref, w2_ref = refs[:4]
    del refs[:4]
    if has_q:
        ones_ref, q_ln_ref, wq_ref, qg_ref = refs[:4]
        del refs[:4]
    o_ref = refs.pop(0)
    qn_ref = refs.pop(0) if has_q else None
    attn_ref, bias_ref = refs
    s = pl.program_id(0)

    @pl.when(s == 0)
    def _():
        attn_ref[...] = jnp.zeros_like(attn_ref)
        _swa_bias_init(n_q_heads, slopes_ref, bias_ref)

    cur = s % 2
    first_in_row = s % tiles_per_row == 0
    result = []
    swa = _swa_tile(n_q_heads, first_in_row, sinks_ref, q_ref, k_refs, v_refs, bias_ref, attn_ref.at[cur])
    mlp = _mlp_stages(x_ref, attn_ref.at[1 - cur], wo_ref, ln_ref, w1_ref, w2_ref, result)
    mlp_stages_per_piece = -(-_MLP_STAGES // (blocks * N_KV))
    for _ in swa:
        for _ in range(mlp_stages_per_piece):
            next(mlp, None)
    for _ in mlp:
        pass
    out = result[0]
    o_ref[...] = out
    if has_q:
        q = _dot(_rms(out, q_ln_ref[...]).astype(BF16), wq_ref[...])
        ms = _group_sum(q * q, ones_ref[...]) * (1.0 / HEAD)
        qn_ref[...] = (q * lax.rsqrt(ms + RMS_EPS) * qg_ref[...] * (HEAD ** -0.5 * LOG2E)).astype(BF16)


def _swa_mlp(x, q, k2, v2, slopes, sinks, wo, ln, w1, w2, tm, tokens_per_row, ones=None, q_next=None):
    M, C = x.shape
    W = WINDOW
    steps = M // tm
    blocks = tm // W
    assert tokens_per_row % tm == 0
    has_q = q_next is not None
    t_att = lambda s: jnp.minimum(s, steps - 1)
    t_mlp = lambda s: jnp.maximum(s - 1, 0)
    kv_spec = lambda off: pl.BlockSpec(
        (W, k2.shape[1]), lambda s, *_: (jnp.maximum(t_att(s) * blocks + off, 0), 0))
    act_mlp = pl.BlockSpec((tm, C), lambda s, *_: (t_mlp(s), 0))
    consts = [wo, ln, w1, w2] + ([ones] + list(q_next) if has_q else [])
    out_shape = [jax.ShapeDtypeStruct((M, C), F32)]
    out_specs = [act_mlp]
    if has_q:
        out_shape.append(jax.ShapeDtypeStruct((M, q_next[1].shape[1]), BF16))
        out_specs.append(pl.BlockSpec((tm, q_next[1].shape[1]), lambda s, *_: (t_mlp(s), 0)))
    return pl.pallas_call(
        functools.partial(_swa_mlp_kernel, has_q, C // HEAD, blocks, tokens_per_row // tm),
        grid_spec=pltpu.PrefetchScalarGridSpec(
            num_scalar_prefetch=2,
            grid=(steps + 1,),
            in_specs=[act_mlp, pl.BlockSpec((tm, C), lambda s, *_: (t_att(s), 0)),
                      *[kv_spec(off) for off in range(-1, blocks)] * 2]
                     + [_weight_spec(a) for a in consts],
            out_specs=out_specs,
            scratch_shapes=[pltpu.VMEM((2, tm, C), BF16),
                            pltpu.VMEM((2, C // HEAD, W, 2 * W), F32)],
        ),
        out_shape=out_shape,
        compiler_params=pltpu.CompilerParams(
            dimension_semantics=("arbitrary",), vmem_limit_bytes=VMEM_LIMIT),
        name="swa_mlp",
    )(slopes, sinks, x, q, *[k2] * (blocks + 1), *[v2] * (blocks + 1), *map(_weight_arg, consts))


def _pad_cols(w, n):
    return jnp.pad(w, ((0, 0), (0, n - w.shape[1])))


def _pad_rows(w, n):
    return jnp.pad(w, ((0, n - w.shape[0]), (0, 0)))


def _round_up(n, m):
    return (n + m - 1) // m * m


def _lora(w_in, w_out):
    n = _round_up(w_in.shape[1], 128)
    return _pad_cols(w_in, n).astype(BF16), _pad_rows(w_out, n).astype(BF16)


def _dup_heads(w):
    c, n = w.shape
    w = w.reshape(c, n // HEAD, 1, HEAD)
    return jnp.broadcast_to(w, (c, n // HEAD, 2, HEAD)).reshape(c, 2 * n)


def _block_ones(n):
    i = jnp.arange(n) // HEAD
    return (i[:, None] == i[None, :]).astype(BF16)


def kernel(x, ln_mix, ln_mlp, mlp_w1, mlp_w2, a_mu, a_w_rkv, a_w0, a_w1, a_w2, a_a0, a_a1, a_a2, a_g1, a_g2, a_k_k, a_k_a, a_r_k, a_gn_g, a_gn_b, a_wo, a_v0, a_v1, a_v2, kv_norm, w_kv, k_gain, b_wq, b_q_gain, b_sinks, b_wo):
    B, T, C = x.shape
    M = B * T
    n_a = a_mu.shape[0]
    n_b = b_wq.shape[0]
    n_heads = C // HEAD
    tm_mlp = 512
    tm_pre = 512
    tt = 1024
    ones256 = _block_ones(256)
    ones128 = _block_ones(PAIR)
    row = lambda v: v.reshape(1, -1).astype(F32)
    slopes = jnp.exp2(-8.0 * jnp.arange(1, n_heads + 1, dtype=F32) / n_heads)
    w1_all = mlp_w1.astype(BF16)
    w2_all = mlp_w2.astype(BF16)

    def q_params(i):
        j = i - n_a
        return row(ln_mix[i]), b_wq[j].astype(BF16), jnp.tile(row(b_q_gain[j]), (1, n_heads))

    def mlp(i, x, y, g, wo):
        kv = q = None
        if i == n_a - 1:
            kv = (row(kv_norm), _dup_heads(w_kv).astype(BF16), jnp.tile(row(k_gain), (1, 2 * N_KV)))
        if n_a - 1 <= i < n_a + n_b - 1:
            q = q_params(i + 1)
        return _post_mlp(x.reshape(M, C), y, g, wo.astype(BF16), row(ln_mlp[i]), _Layer(w1_all, i), _Layer(w2_all, i),
                         tm_mlp, ones256, kv, q)

    v_first = None
    k2 = v2 = q = None
    for i in range(n_a + n_b):
        if i < n_a:
            j = i
            zeros = jnp.zeros((1, C), F32)
            vecs = jnp.concatenate(
                [a_mu[j], row(ln_mix[i]), row(a_w0[j]), row(a_a0[j]),
                 row(a_v0[j - 1]) if j > 0 else zeros, row(a_k_k[j]), row(a_k_a[j])]
                + [zeros] * 4, axis=0)
            loras = list(_lora(a_w1[j], a_w2[j]) + _lora(a_a1[j], a_a2[j]) + _lora(a_g1[j], a_g2[j]))
            if j > 0:
                loras += list(_lora(a_v1[j - 1], a_v2[j - 1]))
            r, lw, k, v, av, kk, g = _rwkv_pre(
                x, v_first if j > 0 else None, vecs, a_w_rkv[j].astype(BF16), loras, ones256, tm_pre)
            if j == 0:
                v_first = v
            pvec = jnp.concatenate([row(a_r_k[j]), row(a_gn_g[j]), row(a_gn_b[j])]
                                   + [jnp.zeros((1, C), F32)] * 5, axis=0)
            y = _wkv(r, lw, k, v, av, kk, pvec, ones128, tt)
            outs = mlp(i, x, y, g, a_wo[j])
        else:
            outs = _swa_mlp(x.reshape(M, C), q, k2, v2, slopes, b_sinks[i - n_a].astype(F32),
                            b_wo[i - n_a].astype(BF16), row(ln_mlp[i]), _Layer(w1_all, i), _Layer(w2_all, i),
                            tm_mlp, T, ones256, q_params(i + 1) if i + 1 < n_a + n_b else None)
        outs = list(outs)
        x = outs.pop(0).reshape(B, T, C)
        if i == n_a - 1:
            k2 = outs.pop(0)
            v2 = outs.pop(0)
        if outs:
            q = outs.pop(0)
    return x
```

```python
import functools
import math

import jax
import jax.numpy as jnp
from jax import lax
from jax.experimental import pallas as pl
from jax.experimental.pallas import tpu as pltpu

F32 = jnp.float32
BF16 = jnp.bfloat16

HEAD = 64
PAIR = 2 * HEAD
N_KV = 2
WINDOW = 128
CHUNK = 64
GN_EPS = 64e-5
RMS_EPS = 1e-6
NEG_BIG = -1e30
LOG2E = math.log2(math.e)
VMEM_LIMIT = 60 * 1024 * 1024

NT_DIMS = (((1,), (1,)), ((), ()))


def _dot(a, b):
    return jnp.dot(a, b, preferred_element_type=F32)


def _dot_nt(a, b):
    return lax.dot_general(a, b, NT_DIMS, preferred_element_type=F32)


def _rms(x, g):
    return x * lax.rsqrt(jnp.mean(x * x, axis=-1, keepdims=True) + RMS_EPS) * g


def _group_sum(x, ones):
    w = ones.shape[0]
    parts = [_dot(x[:, j:j + w].astype(BF16), ones) for j in range(0, x.shape[1], w)]
    return parts[0] if len(parts) == 1 else jnp.concatenate(parts, axis=1)


def _pair_group_sum_lanes(x):
    head0 = lax.broadcasted_iota(jnp.int32, x.shape, 1) < HEAD
    s0 = jnp.sum(jnp.where(head0, x, 0.0), axis=-1, keepdims=True)
    s1 = jnp.sum(jnp.where(head0, 0.0, x), axis=-1, keepdims=True)
    return jnp.where(head0, s0, s1)


def _sigmoid(x):
    return 0.5 * jnp.tanh(0.5 * x) + 0.5


def _store_pairs(ref, x):
    for p in range(x.shape[1] // PAIR):
        ref[0, p] = x[:, p * PAIR:(p + 1) * PAIR]


def _load_pairs(ref):
    return jnp.concatenate([ref[0, p] for p in range(ref.shape[1])], axis=1)


def _const_spec(shape):
    nd = len(shape)
    return pl.BlockSpec(shape, lambda *_: (0,) * nd, pipeline_mode=pl.Buffered(1))


class _Layer:
    def __init__(self, stacked, layer):
        self.stacked, self.layer, self.shape = stacked, layer, stacked.shape[1:]


def _weight_spec(w):
    if isinstance(w, _Layer):
        nd = len(w.shape)
        return pl.BlockSpec((None,) + w.shape, lambda *_: (w.layer,) + (0,) * nd, pipeline_mode=pl.Buffered(1))
    return _const_spec(w.shape)


def _weight_arg(w):
    return w.stacked if isinstance(w, _Layer) else w


_V_MU, _V_LN, _V_W0, _V_A0, _V_V0, _V_KK, _V_KA = 0, 6, 7, 8, 9, 10, 11


def _rwkv_pre_kernel(has_vres, tm, *refs):
    if has_vres:
        (x_ref, vf_ref, vec_ref, wrkv_ref, w1_ref, w2_ref, a1_ref, a2_ref, g1_ref, g2_ref,
         v1_ref, v2_ref, ones_ref,
         r_ref, lw_ref, k_ref, v_ref, av_ref, kk_ref, g_ref, carry_ref) = refs
    else:
        (x_ref, vec_ref, wrkv_ref, w1_ref, w2_ref, a1_ref, a2_ref, g1_ref, g2_ref, ones_ref,
         r_ref, lw_ref, k_ref, v_ref, av_ref, kk_ref, g_ref, carry_ref) = refs

    @pl.when(pl.program_id(1) == 0)
    def _():
        carry_ref[...] = jnp.zeros_like(carry_ref)

    def vec(i):
        return vec_ref[i:i + 1, :]

    h = _rms(x_ref[0], vec(_V_LN))
    row = lax.broadcasted_iota(jnp.int32, h.shape, 0)
    prev = jnp.where(row == 0, carry_ref[0:1, :], pltpu.roll(h, 1, axis=0))
    carry_ref[0:1, :] = h[tm - 1:tm, :]
    xx = prev - h

    def mix(i):
        return (h + xx * vec(_V_MU + i)).astype(BF16)

    r = _dot(mix(0), wrkv_ref[0])
    k = _dot(mix(1), wrkv_ref[1])
    xv = mix(2)
    v = _dot(xv, wrkv_ref[2])

    wl = vec(_V_W0) + _dot(jnp.tanh(_dot(mix(3), w1_ref[...])).astype(BF16), w2_ref[...])
    _store_pairs(lw_ref, -math.exp(-0.5) * _sigmoid(wl))

    if has_vres:
        gate = _sigmoid(vec(_V_V0) + _dot(_dot(xv, v1_ref[...]).astype(BF16), v2_ref[...]))
        v = v + (_load_pairs(vf_ref) - v) * gate
    a = _sigmoid(vec(_V_A0) + _dot(_dot(mix(4), a1_ref[...]).astype(BF16), a2_ref[...]))
    _store_pairs(g_ref, _dot(_sigmoid(_dot(mix(5), g1_ref[...])).astype(BF16), g2_ref[...]))

    kk = k * vec(_V_KK)
    _store_pairs(kk_ref, kk * jnp.minimum(lax.rsqrt(_group_sum(kk * kk, ones_ref[...])), 1e12))
    _store_pairs(k_ref, k * (1.0 + (a - 1.0) * vec(_V_KA)))
    _store_pairs(r_ref, r)
    _store_pairs(v_ref, v)
    _store_pairs(av_ref, a)


def _rwkv_pre(x, v_first, vecs, wrkv, loras, ones, tm):
    B, T, C = x.shape
    has_vres = v_first is not None
    act = pl.BlockSpec((1, tm, C), lambda b, t: (b, t, 0))
    pairs = pl.BlockSpec((1, C // PAIR, tm, PAIR), lambda b, t: (b, 0, t, 0))
    ins = [x] + ([v_first] if has_vres else []) + [vecs, wrkv] + list(loras) + [ones]
    in_specs = [act] + ([pairs] if has_vres else []) + [_const_spec(a.shape) for a in ins[(2 if has_vres else 1):]]
    out = jax.ShapeDtypeStruct((B, C // PAIR, T, PAIR), F32)
    return pl.pallas_call(
        functools.partial(_rwkv_pre_kernel, has_vres, tm),
        grid=(B, T // tm),
        in_specs=in_specs,
        out_specs=[pairs] * 7,
        out_shape=[out] * 7,
        scratch_shapes=[pltpu.VMEM((8, C), F32)],
        compiler_params=pltpu.CompilerParams(
            dimension_semantics=("parallel", "arbitrary"), vmem_limit_bytes=VMEM_LIMIT),
        name="rwkv_pre",
    )(*ins)


def _chunk_cumsum(x):
    pos = lax.broadcasted_iota(jnp.int32, x.shape, 0) & (CHUNK - 1)
    s = 1
    while s < CHUNK:
        x = x + jnp.where(pos >= s, pltpu.roll(x, s, axis=0), 0.0)
        s *= 2
    return x


def _each(f, *lists):
    return [f(*xs) for xs in zip(*lists)]


_WKV_PRE_STAGES = 8
_T_AT, _T_RT, _T_KT, _T_BT, _T_V, _T_EC = range(6)


def _wkv_phase0(n_blocks, never, r_ref, lw_ref, k_ref, v_ref, av_ref, kk_ref, rk, tilde_ref, bv_ref):
    rows = r_ref.shape[2] // n_blocks
    for i in range(n_blocks):
        after = yield
        sl = slice(i * rows, (i + 1) * rows)
        r, lw, k, v, kk = r_ref[0, 0, sl], lw_ref[0, 0, sl], k_ref[0, 0, sl], v_ref[0, 0, sl], kk_ref[0, 0, sl]
        if after is not None:
            lw = lw + jnp.where(never, jnp.concatenate([after] * (rows // after.shape[0]), axis=0), 0.0)
        c = _chunk_cumsum(lw)
        ec = jnp.exp(c)
        enc = jnp.exp(-c)
        tilde_ref[_T_AT, sl] = -kk * jnp.exp(c - lw)
        tilde_ref[_T_RT, sl] = r * ec
        tilde_ref[_T_KT, sl] = k * enc
        tilde_ref[_T_BT, sl] = kk * av_ref[0, 0, sl] * enc
        tilde_ref[_T_V, sl] = v
        tilde_ref[_T_EC, sl] = ec
        bv_ref[sl] = _pair_group_sum_lanes(r * k * rk) * v
    yield


def _wkv_chunks_pre(first, n_chunks, tilde_ref):
    L = CHUNK
    bf = lambda t: t.astype(BF16)
    lane = lax.broadcasted_iota(jnp.int32, (L, PAIR), 1)
    head0 = lane < HEAD
    row = lax.broadcasted_iota(jnp.int32, (L, 2 * L), 0)
    col = lax.broadcasted_iota(jnp.int32, (L, 2 * L), 1)
    src = jnp.where(col >= L, col - L, col)
    strict = src < row
    incl = src <= row
    left = col < L
    eye = jnp.where(src == row, 1.0, 0.0)
    cis = list(range(n_chunks))
    mid = n_chunks // 2 - 1

    def plane(which):
        vals = [tilde_ref[which, (first + ci) * L:(first + ci + 1) * L, :] for ci in cis]
        return lambda ci: vals[ci]

    at, rt, kt, bt, v = plane(_T_AT), plane(_T_RT), plane(_T_KT), plane(_T_BT), plane(_T_V)
    p_last_vals = [tilde_ref[_T_EC, (first + ci + 1) * L - 1:(first + ci + 1) * L, :] for ci in cis]
    p_last = lambda ci: p_last_vals[ci]

    def bd_rows(x):
        return jnp.concatenate([jnp.where(head0, x, 0.0), jnp.where(head0, 0.0, x)], axis=0)

    def bd_cols(x):
        return jnp.concatenate([jnp.where(left, x, 0.0), jnp.where(left, 0.0, x)], axis=0)

    def bd_rows2(x):
        return jnp.concatenate([bd_rows(x[:, :PAIR]), bd_rows(x[:, PAIR:])], axis=1)

    sc = [_dot_nt(bf(jnp.concatenate([at(ci), rt(ci)], axis=0)),
                  bf(jnp.concatenate([bd_rows(bt(ci)), bd_rows(kt(ci))], axis=0))) for ci in cis]
    yield sc[mid][:L, :PAIR], sc[-1][:L, :PAIR]
    a_ab = [jnp.where(strict, s[:L, :2 * L], 0.0) for s in sc]
    akv = [_dot(bf(jnp.where(strict, sc[ci][:L, 2 * L:], 0.0)), bf(bd_rows(v(ci)))) for ci in cis]
    p = _each(lambda a: _dot(bf(a), bf(bd_cols(a))), a_ab)
    t = [eye + a for a in a_ab]
    yield p[mid], p[-1]
    s = 2
    while 2 * s < L:
        o = _each(lambda pp, tt: _dot(bf(pp), bf(jnp.concatenate([bd_cols(tt), bd_cols(pp)], axis=1))), p, t)
        t = _each(lambda tt, oo: tt + oo[:, :2 * L], t, o)
        p = [oo[:, 2 * L:] for oo in o]
        s *= 2
        yield p[mid], p[-1]
    t = _each(lambda pp, tt: tt + _dot(bf(pp), bf(bd_cols(tt))), p, t)
    yield t[mid], t[-1]
    z = [_dot(bf(t[ci]), bf(bd_rows2(jnp.concatenate([at(ci), akv[ci]], axis=1)))) for ci in cis]
    yield z[mid][:, :PAIR], z[-1][:, :PAIR]
    zeros2 = jnp.zeros((2 * L, PAIR), F32)
    o2 = [_dot(
        bf(jnp.concatenate([jnp.where(incl, sc[ci][L:, :2 * L], 0.0), jnp.where(incl, sc[ci][L:, 2 * L:], 0.0)], axis=1)),
        bf(jnp.concatenate([bd_rows2(z[ci]), jnp.concatenate([zeros2, bd_rows(v(ci))], axis=1)], axis=0)))
        for ci in cis]
    rh = [rt(ci) + o2[ci][:, :PAIR] for ci in cis]
    yh = [oo[:, PAIR:] for oo in o2]
    yield yh[mid], yh[-1]
    zeros1 = jnp.zeros((L, PAIR), F32)
    o3 = [_dot(
        bf(jnp.concatenate([bt(ci) * p_last(ci), kt(ci) * p_last(ci)], axis=0).T),
        bf(jnp.concatenate([z[ci], jnp.concatenate([zeros1, v(ci)], axis=1)], axis=0))) for ci in cis]
    r128 = lax.broadcasted_iota(jnp.int32, (PAIR, PAIR), 0)
    c128 = lax.broadcasted_iota(jnp.int32, (PAIR, PAIR), 1)
    same_head = (r128 < HEAD) == (c128 < HEAD)
    m = [jnp.where(same_head, o3[ci][:, :PAIR], 0.0) + jnp.where(r128 == c128, p_last(ci), 0.0) for ci in cis]
    g = [jnp.where(same_head, oo[:, PAIR:], 0.0) for oo in o3]
    return rh, yh, m, g


def _wkv_chain(n_chunks, first_tile, mrh_ref, g_ref, yh_ref, h_ref, ys):
    L = CHUNK
    h = jnp.where(first_tile, 0.0, h_ref[...])
    for ci in range(n_chunks):
        o = _dot(mrh_ref[ci], h.astype(BF16))
        h = o[:PAIR] + g_ref[ci]
        ys.append(o[PAIR:] + yh_ref[ci * L:(ci + 1) * L, :])
        yield
    h_ref[...] = h


def _wkv_kernel(n_chunks, tiles_per_seq,
                r_ref, lw_ref, k_ref, v_ref, av_ref, kk_ref, pvec_in_ref, pvec_out_ref,
                y_ref, h_ref, tilde_ref, bv_ref, mrh_ref, g_ref, yh_ref):
    s = pl.program_id(0)

    @pl.when(s == 0)
    def _():
        for ref in (h_ref, tilde_ref, bv_ref, mrh_ref, g_ref, yh_ref):
            ref[...] = jnp.zeros_like(ref)

    b_in = s % 3
    b_out = (s + 1) % 3

    def body(p0, p1):
        ys = []
        first_tile = (s - 2) % tiles_per_seq == 0
        chain = _wkv_chain(n_chunks, first_tile, mrh_ref.at[p0], g_ref.at[p0], yh_ref.at[p0], h_ref, ys)
        half = n_chunks // 2
        pre = [_wkv_chunks_pre(0, half, tilde_ref.at[p1]),
               _wkv_chunks_pre(half, n_chunks - half, tilde_ref.at[p1])]
        done = [None, None]
        phase0 = _wkv_phase0(2 * _WKV_PRE_STAGES, s < 0, r_ref, lw_ref, k_ref, v_ref, av_ref, kk_ref,
                             pvec_in_ref[0:1, :], tilde_ref.at[p0], bv_ref.at[b_in])
        next(phase0)
        blocks_left = [2 * _WKV_PRE_STAGES]

        def advance(i):
            try:
                deps = next(pre[i])
            except StopIteration as stop:
                done[i] = stop.value
                return
            next(chain, None)
            if blocks_left[0]:
                phase0.send(deps[-1])
                blocks_left[0] -= 1

        advance(0)
        while done[0] is None or done[1] is None:
            for i in (0, 1):
                if done[i] is None:
                    advance(i)
        for _ in chain:
            pass
        assert blocks_left[0] == 0
        rh, yh, m, g = (done[0][j] + done[1][j] for j in range(4))

        y = jnp.concatenate(ys, axis=0)
        inv_n = 1.0 / HEAD
        d = y - _pair_group_sum_lanes(y) * inv_n
        var = _pair_group_sum_lanes(d * d) * inv_n
        y_ref[0, 0] = d * lax.rsqrt(var + GN_EPS) * pvec_out_ref[1:2, :] + pvec_out_ref[2:3, :] + bv_ref[b_out]

        for ci in range(n_chunks):
            mrh_ref[p1, ci] = jnp.concatenate([m[ci], rh[ci]], axis=0).astype(BF16)
            g_ref[p1, ci] = g[ci]
        yh_ref[p1] = jnp.concatenate(yh, axis=0)

    for parity in (0, 1):
        pl.when(s % 2 == parity)(functools.partial(body, parity, 1 - parity))


def _wkv(r, lw, k, v, av, kk, pvec, tt):
    B, n_pairs, T, _ = r.shape
    n_tiles = T // tt
    n_chunks = tt // CHUNK
    steps = B * n_pairs * n_tiles

    def tile_index(i):
        return i // (n_pairs * n_tiles), (i // n_tiles) % n_pairs, i % n_tiles

    t_in = lambda s: jnp.minimum(s, steps - 1)
    t_out = lambda s: jnp.clip(s - 2, 0, steps - 1)
    act_in = pl.BlockSpec((1, 1, tt, PAIR), lambda s: tile_index(t_in(s)) + (0,))
    act_out = pl.BlockSpec((1, 1, tt, PAIR), lambda s: tile_index(t_out(s)) + (0,))
    return pl.pallas_call(
        functools.partial(_wkv_kernel, n_chunks, n_tiles),
        grid=(steps + 2,),
        in_specs=[act_in] * 6 + [pl.BlockSpec((8, PAIR), lambda s: (0, tile_index(t_in(s))[1])),
                                 pl.BlockSpec((8, PAIR), lambda s: (0, tile_index(t_out(s))[1]))],
        out_specs=act_out,
        out_shape=jax.ShapeDtypeStruct((B, n_pairs, T, PAIR), F32),
        scratch_shapes=[pltpu.VMEM((PAIR, PAIR), F32),
                        pltpu.VMEM((2, 6, tt, PAIR), F32),
                        pltpu.VMEM((3, tt, PAIR), F32),
                        pltpu.VMEM((2, n_chunks, PAIR + CHUNK, PAIR), BF16),
                        pltpu.VMEM((2, n_chunks, PAIR, PAIR), F32),
                        pltpu.VMEM((2, tt, PAIR), F32)],
        compiler_params=pltpu.CompilerParams(
            dimension_semantics=("arbitrary",), vmem_limit_bytes=VMEM_LIMIT),
        name="wkv",
    )(r, lw, k, v, av, kk, pvec, pvec)


def _post_mlp_kernel(has_gate, has_kv, has_q, *refs):
    refs = list(refs)
    x_ref, y_ref = refs[:2]
    del refs[:2]
    g_ref = refs.pop(0) if has_gate else None
    wo_ref, ln_ref, w1_ref, w2_ref = refs[:4]
    del refs[:4]
    if has_kv or has_q:
        ones_ref = refs.pop(0)
    if has_kv:
        kv_ln_ref, wkv_ref, kg_ref = refs[:3]
        del refs[:3]
    if has_q:
        q_ln_ref, wq_ref, qg_ref = refs[:3]
        del refs[:3]
    o_ref = refs.pop(0)
    if has_kv:
        k_ref, v_ref = refs[:2]
        del refs[:2]
    if has_q:
        q_ref = refs.pop(0)

    y = _load_pairs(y_ref)
    if has_gate:
        y = y * _load_pairs(g_ref)
    xn = x_ref[...] + _dot(y.astype(BF16), wo_ref[...])
    hid = _dot(_rms(xn, ln_ref[...]).astype(BF16), w1_ref[...])
    hid = jnp.square(jnp.maximum(hid, 0.0)).astype(BF16)
    out = xn + _dot(hid, w2_ref[...])
    o_ref[...] = out

    if has_kv:
        kv = _dot(_rms(out, kv_ln_ref[...]).astype(BF16), wkv_ref[...])
        half = kv.shape[1] // 2
        k = kv[:, :half]
        ms = _group_sum(k * k, ones_ref[...]) * (1.0 / HEAD)
        k_ref[...] = (k * lax.rsqrt(ms + RMS_EPS) * kg_ref[...]).astype(BF16)
        v_ref[...] = kv[:, half:].astype(BF16)
    if has_q:
        q = _dot(_rms(out, q_ln_ref[...]).astype(BF16), wq_ref[...])
        ms = _group_sum(q * q, ones_ref[...]) * (1.0 / HEAD)
        q_ref[...] = (q * lax.rsqrt(ms + RMS_EPS) * qg_ref[...] * (HEAD ** -0.5 * LOG2E)).astype(BF16)


def _post_mlp(x, y, g, wo, ln, w1, w2, tm, ones=None, kv=None, q=None):
    M, C = x.shape
    tiles_per_row = y.shape[2] // tm
    act = pl.BlockSpec((tm, C), lambda i: (i, 0))
    pairs = pl.BlockSpec((1, C // PAIR, tm, PAIR), lambda i: (i // tiles_per_row, 0, i % tiles_per_row, 0))
    has_gate, has_kv, has_q = g is not None, kv is not None, q is not None
    acts = [x, y] + ([g] if has_gate else [])
    consts = [wo, ln, w1, w2] + ([ones] if has_kv or has_q else []) + list(kv or ()) + list(q or ())
    out_shape = [jax.ShapeDtypeStruct((M, C), F32)]
    out_specs = [act]
    if has_kv:
        n = kv[1].shape[1] // 2
        out_shape += [jax.ShapeDtypeStruct((M, n), BF16)] * 2
        out_specs += [pl.BlockSpec((tm, n), lambda i: (i, 0))] * 2
    if has_q:
        n = q[1].shape[1]
        out_shape.append(jax.ShapeDtypeStruct((M, n), BF16))
        out_specs.append(pl.BlockSpec((tm, n), lambda i: (i, 0)))
    return pl.pallas_call(
        functools.partial(_post_mlp_kernel, has_gate, has_kv, has_q),
        grid=(M // tm,),
        in_specs=[act] + [pairs] * (len(acts) - 1) + [_weight_spec(a) for a in consts],
        out_specs=out_specs,
        out_shape=out_shape,
        compiler_params=pltpu.CompilerParams(
            dimension_semantics=("parallel",), vmem_limit_bytes=VMEM_LIMIT),
        name="post_mlp",
    )(*acts, *map(_weight_arg, consts))


def _swa_bias_init(n_q_heads, slopes_ref, bias_ref):
    W = WINDOW
    qi = lax.broadcasted_iota(jnp.int32, (W, 2 * W), 0)
    kj = lax.broadcasted_iota(jnp.int32, (W, 2 * W), 1)
    dist = qi + W - kj
    in_window = (dist >= 0) & (dist < WINDOW)
    in_window_cur = in_window & (kj >= W)
    distf = dist.astype(F32)
    for head in range(n_q_heads):
        alibi = -(slopes_ref[head] * LOG2E) * distf
        bias_ref[0, head] = jnp.where(in_window, alibi, NEG_BIG)
        bias_ref[1, head] = jnp.where(in_window_cur, alibi, NEG_BIG)


def _swa_tile(n_q_heads, first_in_row, sinks_ref, q_ref, k_refs, v_refs, bias_ref, attn_ref):
    W = WINDOW
    group = n_q_heads // N_KV
    pairs = group // 2
    lane = lax.broadcasted_iota(jnp.int32, (W, PAIR), 1)
    head0 = lane < HEAD
    lane2 = lax.broadcasted_iota(jnp.int32, (2 * W, PAIR), 1)
    head0_kv = lane2 < HEAD
    zero = jnp.zeros((), BF16)

    for qb in range(q_ref.shape[0] // W):
        rows = slice(qb * W, (qb + 1) * W)
        plane = jnp.where(first_in_row, 1, 0) if qb == 0 else 0
        for h in range(N_KV):
            ks = slice(h * PAIR, (h + 1) * PAIR)
            k2 = jnp.concatenate([k_refs[qb][:, ks], k_refs[qb + 1][:, ks]], axis=0)
            v2 = jnp.concatenate([v_refs[qb][:, ks], v_refs[qb + 1][:, ks]], axis=0)
            q_tiles = [q_ref[rows, (h * pairs + j) * PAIR:(h * pairs + j + 1) * PAIR] for j in range(pairs)]
            lhs = jnp.concatenate([jnp.where(head0, t, zero) for t in q_tiles]
                                  + [jnp.where(head0, zero, t) for t in q_tiles], axis=0)
            s_all = _dot_nt(lhs, k2)
            yield
            probs, denoms = [], []
            for i in range(group):
                head = h * group + 2 * (i % pairs) + i // pairs
                s = s_all[i * W:(i + 1) * W, :] + bias_ref[plane, head]
                sink = sinks_ref[head] * LOG2E
                mx = jnp.maximum(jnp.max(s, axis=-1, keepdims=True), sink)
                p = jnp.exp2(s - mx)
                denoms.append(jnp.sum(p, axis=-1, keepdims=True) + jnp.exp2(sink - mx))
                probs.append(p.astype(BF16))
            p_first = jnp.concatenate(probs[:pairs], axis=0)
            p_second = jnp.concatenate(probs[pairs:], axis=0)
            v_stack = jnp.concatenate([jnp.where(head0_kv, v2, zero), jnp.where(head0_kv, zero, v2)], axis=0)
            o = _dot(jnp.concatenate([p_first, p_second], axis=1), v_stack)
            for j in range(pairs):
                den = jnp.where(head0, denoms[j], denoms[pairs + j])
                col = (h * pairs + j) * PAIR
                attn_ref[rows, col:col + PAIR] = (o[j * W:(j + 1) * W, :] / den).astype(attn_ref.dtype)


_MLP_SPLIT = 4


def _mlp_stages(x_ref, attn_ref, wo_ref, ln_ref, w1_ref, w2_ref, result):
    xn = x_ref[...] + _dot(attn_ref[...], wo_ref[...])
    h = _rms(xn, ln_ref[...]).astype(BF16)
    yield
    step = w1_ref.shape[1] // _MLP_SPLIT
    hids = []
    for c in range(_MLP_SPLIT):
        hid = _dot(h, w1_ref[:, c * step:(c + 1) * step])
        hids.append(jnp.square(jnp.maximum(hid, 0.0)).astype(BF16))
        yield
    acc = xn
    for c in range(_MLP_SPLIT):
        acc = acc + _dot(hids[c], w2_ref[c * step:(c + 1) * step, :])
        if c + 1 < _MLP_SPLIT:
            yield
    result.append(acc)


_MLP_STAGES = 2 * _MLP_SPLIT


def _swa_mlp_kernel(has_q, n_q_heads, blocks, tiles_per_row, slopes_ref, sinks_ref, *refs):
    refs = list(refs)
    x_ref, q_ref = refs[:2]
    k_refs = refs[2:3 + blocks]
    v_refs = refs[3 + blocks:4 + 2 * blocks]
    del refs[:4 + 2 * blocks]
    wo_ref, ln_ref, w1_ref, w2_ref = refs[:4]
    del refs[:4]
    if has_q:
        ones_ref, q_ln_ref, wq_ref, qg_ref = refs[:4]
        del refs[:4]
    o_ref = refs.pop(0)
    qn_ref = refs.pop(0) if has_q else None
    attn_ref, bias_ref = refs
    s = pl.program_id(0)

    @pl.when(s == 0)
    def _():
        attn_ref[...] = jnp.zeros_like(attn_ref)
        _swa_bias_init(n_q_heads, slopes_ref, bias_ref)

    cur = s % 2
    first_in_row = s % tiles_per_row == 0
    result = []
    swa = _swa_tile(n_q_heads, first_in_row, sinks_ref, q_ref, k_refs, v_refs, bias_ref, attn_ref.at[cur])
    mlp = _mlp_stages(x_ref, attn_ref.at[1 - cur], wo_ref, ln_ref, w1_ref, w2_ref, result)
    mlp_stages_per_piece = -(-_MLP_STAGES // (blocks * N_KV))
    for _ in swa:
        for _ in range(mlp_stages_per_piece):
            next(mlp, None)
    for _ in mlp:
        pass
    out = result[0]
    o_ref[...] = out
    if has_q:
        q = _dot(_rms(out, q_ln_ref[...]).astype(BF16), wq_ref[...])
        ms = _group_sum(q * q, ones_ref[...]) * (1.0 / HEAD)
        qn_ref[...] = (q * lax.rsqrt(ms + RMS_EPS) * qg_ref[...] * (HEAD ** -0.5 * LOG2E)).astype(BF16)


def _swa_mlp(x, q, k2, v2, slopes, sinks, wo, ln, w1, w2, tm, tokens_per_row, ones=None, q_next=None):
    M, C = x.shape
    W = WINDOW
    steps = M // tm
    blocks = tm // W
    assert tokens_per_row % tm == 0
    has_q = q_next is not None
    t_att = lambda s: jnp.minimum(s, steps - 1)
    t_mlp = lambda s: jnp.maximum(s - 1, 0)
    kv_spec = lambda off: pl.BlockSpec(
        (W, k2.shape[1]), lambda s, *_: (jnp.maximum(t_att(s) * blocks + off, 0), 0))
    act_mlp = pl.BlockSpec((tm, C), lambda s, *_: (t_mlp(s), 0))
    consts = [wo, ln, w1, w2] + ([ones] + list(q_next) if has_q else [])
    out_shape = [jax.ShapeDtypeStruct((M, C), F32)]
    out_specs = [act_mlp]
    if has_q:
        out_shape.append(jax.ShapeDtypeStruct((M, q_next[1].shape[1]), BF16))
        out_specs.append(pl.BlockSpec((tm, q_next[1].shape[1]), lambda s, *_: (t_mlp(s), 0)))
    return pl.pallas_call(
        functools.partial(_swa_mlp_kernel, has_q, C // HEAD, blocks, tokens_per_row // tm),
        grid_spec=pltpu.PrefetchScalarGridSpec(
            num_scalar_prefetch=2,
            grid=(steps + 1,),
            in_specs=[act_mlp, pl.BlockSpec((tm, C), lambda s, *_: (t_att(s), 0)),
                      *[kv_spec(off) for off in range(-1, blocks)] * 2]
                     + [_weight_spec(a) for a in consts],
            out_specs=out_specs,
            scratch_shapes=[pltpu.VMEM((2, tm, C), BF16),
                            pltpu.VMEM((2, C // HEAD, W, 2 * W), F32)],
        ),
        out_shape=out_shape,
        compiler_params=pltpu.CompilerParams(
            dimension_semantics=("arbitrary",), vmem_limit_bytes=VMEM_LIMIT),
        name="swa_mlp",
    )(slopes, sinks, x, q, *[k2] * (blocks + 1), *[v2] * (blocks + 1), *map(_weight_arg, consts))


def _pad_cols(w, n):
    return jnp.pad(w, ((0, 0), (0, n - w.shape[1])))


def _pad_rows(w, n):
    return jnp.pad(w, ((0, n - w.shape[0]), (0, 0)))


def _round_up(n, m):
    return (n + m - 1) // m * m


def _lora(w_in, w_out):
    n = _round_up(w_in.shape[1], 128)
    return _pad_cols(w_in, n).astype(BF16), _pad_rows(w_out, n).astype(BF16)


def _dup_heads(w):
    c, n = w.shape
    w = w.reshape(c, n // HEAD, 1, HEAD)
    return jnp.broadcast_to(w, (c, n // HEAD, 2, HEAD)).reshape(c, 2 * n)


def _block_ones(n):
    i = jnp.arange(n) // HEAD
    return (i[:, None] == i[None, :]).astype(BF16)


def kernel(x, ln_mix, ln_mlp, mlp_w1, mlp_w2, a_mu, a_w_rkv, a_w0, a_w1, a_w2, a_a0, a_a1, a_a2, a_g1, a_g2, a_k_k, a_k_a, a_r_k, a_gn_g, a_gn_b, a_wo, a_v0, a_v1, a_v2, kv_norm, w_kv, k_gain, b_wq, b_q_gain, b_sinks, b_wo):
    B, T, C = x.shape
    M = B * T
    n_a = a_mu.shape[0]
    n_b = b_wq.shape[0]
    n_heads = C // HEAD
    tm_mlp = 512
    tm_pre = 512
    tt = 1024
    ones256 = _block_ones(256)
    row = lambda v: v.reshape(1, -1).astype(F32)
    slopes = jnp.exp2(-8.0 * jnp.arange(1, n_heads + 1, dtype=F32) / n_heads)
    w1_all = mlp_w1.astype(BF16)
    w2_all = mlp_w2.astype(BF16)

    def q_params(i):
        j = i - n_a
        return row(ln_mix[i]), b_wq[j].astype(BF16), jnp.tile(row(b_q_gain[j]), (1, n_heads))

    def mlp(i, x, y, g, wo):
        kv = q = None
        if i == n_a - 1:
            kv = (row(kv_norm), _dup_heads(w_kv).astype(BF16), jnp.tile(row(k_gain), (1, 2 * N_KV)))
        if n_a - 1 <= i < n_a + n_b - 1:
            q = q_params(i + 1)
        return _post_mlp(x.reshape(M, C), y, g, wo.astype(BF16), row(ln_mlp[i]), _Layer(w1_all, i), _Layer(w2_all, i),
                         tm_mlp, ones256, kv, q)

    v_first = None
    k2 = v2 = q = None
    for i in range(n_a + n_b):
        if i < n_a:
            j = i
            zeros = jnp.zeros((1, C), F32)
            vecs = jnp.concatenate(
                [a_mu[j], row(ln_mix[i]), row(a_w0[j]), row(a_a0[j]),
                 row(a_v0[j - 1]) if j > 0 else zeros, row(a_k_k[j]), row(a_k_a[j])]
                + [zeros] * 4, axis=0)
            loras = list(_lora(a_w1[j], a_w2[j]) + _lora(a_a1[j], a_a2[j]) + _lora(a_g1[j], a_g2[j]))
            if j > 0:
                loras += list(_lora(a_v1[j - 1], a_v2[j - 1]))
            r, lw, k, v, av, kk, g = _rwkv_pre(
                x, v_first if j > 0 else None, vecs, a_w_rkv[j].astype(BF16), loras, ones256, tm_pre)
            if j == 0:
                v_first = v
            pvec = jnp.concatenate([row(a_r_k[j]), row(a_gn_g[j]), row(a_gn_b[j])]
                                   + [jnp.zeros((1, C), F32)] * 5, axis=0)
            y = _wkv(r, lw, k, v, av, kk, pvec, tt)
            outs = mlp(i, x, y, g, a_wo[j])
        else:
            outs = _swa_mlp(x.reshape(M, C), q, k2, v2, slopes, b_sinks[i - n_a].astype(F32),
                            b_wo[i - n_a].astype(BF16), row(ln_mlp[i]), _Layer(w1_all, i), _Layer(w2_all, i),
                            tm_mlp, T, ones256, q_params(i + 1) if i + 1 < n_a + n_b else None)
        outs = list(outs)
        x = outs.pop(0).reshape(B, T, C)
        if i == n_a - 1:
            k2 = outs.pop(0)
            v2 = outs.pop(0)
        if outs:
            q = outs.pop(0)
    return x
```

```python
import functools
import math

import jax
import jax.numpy as jnp
from jax import lax
from jax.experimental import pallas as pl
from jax.experimental.pallas import tpu as pltpu

F32 = jnp.float32
BF16 = jnp.bfloat16

HEAD = 64
PAIR = 2 * HEAD
N_KV = 2
WINDOW = 128
CHUNK = 64
GN_EPS = 64e-5
RMS_EPS = 1e-6
NEG_BIG = -1e30
LOG2E = math.log2(math.e)
VMEM_LIMIT = 60 * 1024 * 1024

NT_DIMS = (((1,), (1,)), ((), ()))


def _dot(a, b):
    return jnp.dot(a, b, preferred_element_type=F32)


def _dot_nt(a, b):
    return lax.dot_general(a, b, NT_DIMS, preferred_element_type=F32)


def _rms(x, g):
    return x * lax.rsqrt(jnp.mean(x * x, axis=-1, keepdims=True) + RMS_EPS) * g


def _group_sum(x, ones):
    w = ones.shape[0]
    parts = [_dot(x[:, j:j + w].astype(BF16), ones) for j in range(0, x.shape[1], w)]
    return parts[0] if len(parts) == 1 else jnp.concatenate(parts, axis=1)


def _pair_group_sum_lanes(x):
    head0 = lax.broadcasted_iota(jnp.int32, x.shape, 1) < HEAD
    s0 = jnp.sum(jnp.where(head0, x, 0.0), axis=-1, keepdims=True)
    s1 = jnp.sum(jnp.where(head0, 0.0, x), axis=-1, keepdims=True)
    return jnp.where(head0, s0, s1)


def _sigmoid(x):
    return 0.5 * jnp.tanh(0.5 * x) + 0.5


def _store_pairs(ref, x):
    for p in range(x.shape[1] // PAIR):
        ref[0, p] = x[:, p * PAIR:(p + 1) * PAIR]


def _load_pairs(ref):
    return jnp.concatenate([ref[0, p] for p in range(ref.shape[1])], axis=1)


def _const_spec(shape):
    nd = len(shape)
    return pl.BlockSpec(shape, lambda *_: (0,) * nd, pipeline_mode=pl.Buffered(1))


class _Layer:
    def __init__(self, stacked, layer):
        self.stacked, self.layer, self.shape = stacked, layer, stacked.shape[1:]


def _weight_spec(w):
    if isinstance(w, _Layer):
        nd = len(w.shape)
        return pl.BlockSpec((None,) + w.shape, lambda *_: (w.layer,) + (0,) * nd, pipeline_mode=pl.Buffered(1))
    return _const_spec(w.shape)


def _weight_arg(w):
    return w.stacked if isinstance(w, _Layer) else w


_V_MU, _V_LN, _V_W0, _V_A0, _V_V0, _V_KK, _V_KA = 0, 6, 7, 8, 9, 10, 11


def _rwkv_pre_kernel(has_vres, tm, *refs):
    if has_vres:
        (x_ref, vf_ref, vec_ref, wrkv_ref, w1_ref, w2_ref, a1_ref, a2_ref, g1_ref, g2_ref,
         v1_ref, v2_ref, ones_ref,
         r_ref, lw_ref, k_ref, v_ref, av_ref, kk_ref, g_ref, carry_ref) = refs
    else:
        (x_ref, vec_ref, wrkv_ref, w1_ref, w2_ref, a1_ref, a2_ref, g1_ref, g2_ref, ones_ref,
         r_ref, lw_ref, k_ref, v_ref, av_ref, kk_ref, g_ref, carry_ref) = refs

    @pl.when(pl.program_id(1) == 0)
    def _():
        carry_ref[...] = jnp.zeros_like(carry_ref)

    def vec(i):
        return vec_ref[i:i + 1, :]

    h = _rms(x_ref[0], vec(_V_LN))
    row = lax.broadcasted_iota(jnp.int32, h.shape, 0)
    prev = jnp.where(row == 0, carry_ref[0:1, :], pltpu.roll(h, 1, axis=0))
    carry_ref[0:1, :] = h[tm - 1:tm, :]
    xx = prev - h

    def mix(i):
        return (h + xx * vec(_V_MU + i)).astype(BF16)

    r = _dot(mix(0), wrkv_ref[0])
    k = _dot(mix(1), wrkv_ref[1])
    xv = mix(2)
    v = _dot(xv, wrkv_ref[2])

    hw = jnp.tanh(_dot(mix(3), w1_ref[...])).astype(BF16)
    hv = _dot(xv, v1_ref[...]).astype(BF16) if has_vres else None
    ha = _dot(mix(4), a1_ref[...]).astype(BF16)
    hg = _sigmoid(_dot(mix(5), g1_ref[...])).astype(BF16)
    _store_pairs(r_ref, r)
    n_pairs = r.shape[1] // PAIR
    halves = 2
    width = r.shape[1] // halves

    def store_half(ref, x, c):
        for p in range(n_pairs // halves):
            ref[0, c * (n_pairs // halves) + p] = x[:, p * PAIR:(p + 1) * PAIR]

    for c in range(halves):
        cs = slice(c * width, (c + 1) * width)
        vc = lambda i: vec_ref[i:i + 1, cs]
        wl = vc(_V_W0) + _dot(hw, w2_ref[:, cs])
        store_half(lw_ref, -math.exp(-0.5) * _sigmoid(wl), c)
        v_c = v[:, cs]
        if has_vres:
            gate = _sigmoid(vc(_V_V0) + _dot(hv, v2_ref[:, cs]))
            vf = jnp.concatenate([vf_ref[0, c * (n_pairs // halves) + p] for p in range(n_pairs // halves)], axis=1)
            v_c = v_c + (vf - v_c) * gate
        store_half(v_ref, v_c, c)
        a = _sigmoid(vc(_V_A0) + _dot(ha, a2_ref[:, cs]))
        store_half(av_ref, a, c)
        store_half(g_ref, _dot(hg, g2_ref[:, cs]), c)
        k_c = k[:, cs]
        kk = k_c * vc(_V_KK)
        store_half(kk_ref, kk * jnp.minimum(lax.rsqrt(_group_sum(kk * kk, ones_ref[...])), 1e12), c)
        store_half(k_ref, k_c * (1.0 + (a - 1.0) * vc(_V_KA)), c)


def _rwkv_pre(x, v_first, vecs, wrkv, loras, ones, tm):
    B, T, C = x.shape
    has_vres = v_first is not None
    act = pl.BlockSpec((1, tm, C), lambda b, t: (b, t, 0))
    pairs = pl.BlockSpec((1, C // PAIR, tm, PAIR), lambda b, t: (b, 0, t, 0))
    ins = [x] + ([v_first] if has_vres else []) + [vecs, wrkv] + list(loras) + [ones]
    in_specs = [act] + ([pairs] if has_vres else []) + [_const_spec(a.shape) for a in ins[(2 if has_vres else 1):]]
    out = jax.ShapeDtypeStruct((B, C // PAIR, T, PAIR), F32)
    return pl.pallas_call(
        functools.partial(_rwkv_pre_kernel, has_vres, tm),
        grid=(B, T // tm),
        in_specs=in_specs,
        out_specs=[pairs] * 7,
        out_shape=[out] * 7,
        scratch_shapes=[pltpu.VMEM((8, C), F32)],
        compiler_params=pltpu.CompilerParams(
            dimension_semantics=("parallel", "arbitrary"), vmem_limit_bytes=VMEM_LIMIT),
        name="rwkv_pre",
    )(*ins)


def _chunk_cumsum(x):
    pos = lax.broadcasted_iota(jnp.int32, x.shape, 0) & (CHUNK - 1)
    s = 1
    while s < CHUNK:
        x = x + jnp.where(pos >= s, pltpu.roll(x, s, axis=0), 0.0)
        s *= 2
    return x


def _each(f, *lists):
    return [f(*xs) for xs in zip(*lists)]


_WKV_PRE_STAGES = 8
_T_AT, _T_RT, _T_KT, _T_BT, _T_V, _T_EC = range(6)


def _wkv_phase0(n_blocks, never, r_ref, lw_ref, k_ref, v_ref, av_ref, kk_ref, rk, tilde_ref, bv_ref):
    rows = r_ref.shape[2] // n_blocks
    for i in range(n_blocks):
        after = yield
        sl = slice(i * rows, (i + 1) * rows)
        r, lw, k, v, kk = r_ref[0, 0, sl], lw_ref[0, 0, sl], k_ref[0, 0, sl], v_ref[0, 0, sl], kk_ref[0, 0, sl]
        if after is not None:
            lw = lw + jnp.where(never, jnp.concatenate([after] * (rows // after.shape[0]), axis=0), 0.0)
        c = _chunk_cumsum(lw)
        ec = jnp.exp(c)
        enc = jnp.exp(-c)
        tilde_ref[_T_AT, sl] = -kk * jnp.exp(c - lw)
        tilde_ref[_T_RT, sl] = r * ec
        tilde_ref[_T_KT, sl] = k * enc
        tilde_ref[_T_BT, sl] = kk * av_ref[0, 0, sl] * enc
        tilde_ref[_T_V, sl] = v
        tilde_ref[_T_EC, sl] = ec
        bv_ref[sl] = _pair_group_sum_lanes(r * k * rk) * v
    yield


def _wkv_chunks_pre(first, n_chunks, tilde_ref):
    L = CHUNK
    bf = lambda t: t.astype(BF16)
    lane = lax.broadcasted_iota(jnp.int32, (L, PAIR), 1)
    head0 = lane < HEAD
    row = lax.broadcasted_iota(jnp.int32, (L, 2 * L), 0)
    col = lax.broadcasted_iota(jnp.int32, (L, 2 * L), 1)
    src = jnp.where(col >= L, col - L, col)
    strict = src < row
    incl = src <= row
    left = col < L
    eye = jnp.where(src == row, 1.0, 0.0)
    cis = list(range(n_chunks))
    mid = n_chunks // 2 - 1

    def plane(which):
        vals = [tilde_ref[which, (first + ci) * L:(first + ci + 1) * L, :] for ci in cis]
        return lambda ci: vals[ci]

    at, rt, kt, bt, v = plane(_T_AT), plane(_T_RT), plane(_T_KT), plane(_T_BT), plane(_T_V)
    p_last_vals = [tilde_ref[_T_EC, (first + ci + 1) * L - 1:(first + ci + 1) * L, :] for ci in cis]
    p_last = lambda ci: p_last_vals[ci]

    def bd_rows(x):
        return jnp.concatenate([jnp.where(head0, x, 0.0), jnp.where(head0, 0.0, x)], axis=0)

    def bd_cols(x):
        return jnp.concatenate([jnp.where(left, x, 0.0), jnp.where(left, 0.0, x)], axis=0)

    def bd_rows2(x):
        return jnp.concatenate([bd_rows(x[:, :PAIR]), bd_rows(x[:, PAIR:])], axis=1)

    sc = [_dot_nt(bf(jnp.concatenate([at(ci), rt(ci)], axis=0)),
                  bf(jnp.concatenate([bd_rows(bt(ci)), bd_rows(kt(ci))], axis=0))) for ci in cis]
    yield sc[mid][:L, :PAIR], sc[-1][:L, :PAIR]
    a_ab = [jnp.where(strict, s[:L, :2 * L], 0.0) for s in sc]
    akv = [_dot(bf(jnp.where(strict, sc[ci][:L, 2 * L:], 0.0)), bf(bd_rows(v(ci)))) for ci in cis]
    p = _each(lambda a: _dot(bf(a), bf(bd_cols(a))), a_ab)
    t = [eye + a for a in a_ab]
    yield p[mid], p[-1]
    s = 2
    while 2 * s < L:
        o = _each(lambda pp, tt: _dot(bf(pp), bf(jnp.concatenate([bd_cols(tt), bd_cols(pp)], axis=1))), p, t)
        t = _each(lambda tt, oo: tt + oo[:, :2 * L], t, o)
        p = [oo[:, 2 * L:] for oo in o]
        s *= 2
        yield p[mid], p[-1]
    t = _each(lambda pp, tt: tt + _dot(bf(pp), bf(bd_cols(tt))), p, t)
    yield t[mid], t[-1]
    z = [_dot(bf(t[ci]), bf(bd_rows2(jnp.concatenate([at(ci), akv[ci]], axis=1)))) for ci in cis]
    yield z[mid][:, :PAIR], z[-1][:, :PAIR]
    zeros2 = jnp.zeros((2 * L, PAIR), F32)
    o2 = [_dot(
        bf(jnp.concatenate([jnp.where(incl, sc[ci][L:, :2 * L], 0.0), jnp.where(incl, sc[ci][L:, 2 * L:], 0.0)], axis=1)),
        bf(jnp.concatenate([bd_rows2(z[ci]), jnp.concatenate([zeros2, bd_rows(v(ci))], axis=1)], axis=0)))
        for ci in cis]
    rh = [rt(ci) + o2[ci][:, :PAIR] for ci in cis]
    yh = [oo[:, PAIR:] for oo in o2]
    yield yh[mid], yh[-1]
    zeros1 = jnp.zeros((L, PAIR), F32)
    o3 = [_dot(
        bf(jnp.concatenate([bt(ci) * p_last(ci), kt(ci) * p_last(ci)], axis=0).T),
        bf(jnp.concatenate([z[ci], jnp.concatenate([zeros1, v(ci)], axis=1)], axis=0))) for ci in cis]
    r128 = lax.broadcasted_iota(jnp.int32, (PAIR, PAIR), 0)
    c128 = lax.broadcasted_iota(jnp.int32, (PAIR, PAIR), 1)
    same_head = (r128 < HEAD) == (c128 < HEAD)
    m = [jnp.where(same_head, o3[ci][:, :PAIR], 0.0) + jnp.where(r128 == c128, p_last(ci), 0.0) for ci in cis]
    g = [jnp.where(same_head, oo[:, PAIR:], 0.0) for oo in o3]
    return rh, yh, m, g


def _wkv_chain(n_chunks, first_tile, mrh_ref, g_ref, yh_ref, h_ref, ys):
    L = CHUNK
    h = jnp.where(first_tile, 0.0, h_ref[...])
    for ci in range(n_chunks):
        o = _dot(mrh_ref[ci], h.astype(BF16))
        h = o[:PAIR] + g_ref[ci]
        ys.append(o[PAIR:] + yh_ref[ci * L:(ci + 1) * L, :])
        yield
    h_ref[...] = h


def _wkv_kernel(n_chunks, tiles_per_seq,
                r_ref, lw_ref, k_ref, v_ref, av_ref, kk_ref, pvec_in_ref, pvec_out_ref,
                y_ref, h_ref, tilde_ref, bv_ref, mrh_ref, g_ref, yh_ref):
    s = pl.program_id(0)

    @pl.when(s == 0)
    def _():
        for ref in (h_ref, tilde_ref, bv_ref, mrh_ref, g_ref, yh_ref):
            ref[...] = jnp.zeros_like(ref)

    b_in = s % 3
    b_out = (s + 1) % 3

    def body(p0, p1):
        ys = []
        first_tile = (s - 2) % tiles_per_seq == 0
        chain = _wkv_chain(n_chunks, first_tile, mrh_ref.at[p0], g_ref.at[p0], yh_ref.at[p0], h_ref, ys)
        half = n_chunks // 2
        pre = [_wkv_chunks_pre(0, half, tilde_ref.at[p1]),
               _wkv_chunks_pre(half, n_chunks - half, tilde_ref.at[p1])]
        done = [None, None]
        phase0 = _wkv_phase0(2 * _WKV_PRE_STAGES, s < 0, r_ref, lw_ref, k_ref, v_ref, av_ref, kk_ref,
                             pvec_in_ref[0:1, :], tilde_ref.at[p0], bv_ref.at[b_in])
        next(phase0)
        blocks_left = [2 * _WKV_PRE_STAGES]

        def advance(i):
            try:
                deps = next(pre[i])
            except StopIteration as stop:
                done[i] = stop.value
                return
            next(chain, None)
            if blocks_left[0]:
                phase0.send(deps[-1])
                blocks_left[0] -= 1

        advance(0)
        while done[0] is None or done[1] is None:
            for i in (0, 1):
                if done[i] is None:
                    advance(i)
        for _ in chain:
            pass
        assert blocks_left[0] == 0
        rh, yh, m, g = (done[0][j] + done[1][j] for j in range(4))

        y = jnp.concatenate(ys, axis=0)
        inv_n = 1.0 / HEAD
        d = y - _pair_group_sum_lanes(y) * inv_n
        var = _pair_group_sum_lanes(d * d) * inv_n
        y_ref[0, 0] = d * lax.rsqrt(var + GN_EPS) * pvec_out_ref[1:2, :] + pvec_out_ref[2:3, :] + bv_ref[b_out]

        for ci in range(n_chunks):
            mrh_ref[p1, ci] = jnp.concatenate([m[ci], rh[ci]], axis=0).astype(BF16)
            g_ref[p1, ci] = g[ci]
        yh_ref[p1] = jnp.concatenate(yh, axis=0)

    for parity in (0, 1):
        pl.when(s % 2 == parity)(functools.partial(body, parity, 1 - parity))


def _wkv(r, lw, k, v, av, kk, pvec, tt):
    B, n_pairs, T, _ = r.shape
    n_tiles = T // tt
    n_chunks = tt // CHUNK
    steps = B * n_pairs * n_tiles

    def tile_index(i):
        return i // (n_pairs * n_tiles), (i // n_tiles) % n_pairs, i % n_tiles

    t_in = lambda s: jnp.minimum(s, steps - 1)
    t_out = lambda s: jnp.clip(s - 2, 0, steps - 1)
    act_in = pl.BlockSpec((1, 1, tt, PAIR), lambda s: tile_index(t_in(s)) + (0,))
    act_out = pl.BlockSpec((1, 1, tt, PAIR), lambda s: tile_index(t_out(s)) + (0,))
    return pl.pallas_call(
        functools.partial(_wkv_kernel, n_chunks, n_tiles),
        grid=(steps + 2,),
        in_specs=[act_in] * 6 + [pl.BlockSpec((8, PAIR), lambda s: (0, tile_index(t_in(s))[1])),
                                 pl.BlockSpec((8, PAIR), lambda s: (0, tile_index(t_out(s))[1]))],
        out_specs=act_out,
        out_shape=jax.ShapeDtypeStruct((B, n_pairs, T, PAIR), F32),
        scratch_shapes=[pltpu.VMEM((PAIR, PAIR), F32),
                        pltpu.VMEM((2, 6, tt, PAIR), F32),
                        pltpu.VMEM((3, tt, PAIR), F32),
                        pltpu.VMEM((2, n_chunks, PAIR + CHUNK, PAIR), BF16),
                        pltpu.VMEM((2, n_chunks, PAIR, PAIR), F32),
                        pltpu.VMEM((2, tt, PAIR), F32)],
        compiler_params=pltpu.CompilerParams(
            dimension_semantics=("arbitrary",), vmem_limit_bytes=VMEM_LIMIT),
        name="wkv",
    )(r, lw, k, v, av, kk, pvec, pvec)


def _post_mlp_kernel(has_gate, has_kv, has_q, *refs):
    refs = list(refs)
    x_ref, y_ref = refs[:2]
    del refs[:2]
    g_ref = refs.pop(0) if has_gate else None
    wo_ref, ln_ref, w1_ref, w2_ref = refs[:4]
    del refs[:4]
    if has_kv or has_q:
        ones_ref = refs.pop(0)
    if has_kv:
        kv_ln_ref, wkv_ref, kg_ref = refs[:3]
        del refs[:3]
    if has_q:
        q_ln_ref, wq_ref, qg_ref = refs[:3]
        del refs[:3]
    o_ref = refs.pop(0)
    if has_kv:
        k_ref, v_ref = refs[:2]
        del refs[:2]
    if has_q:
        q_ref = refs.pop(0)

    y = _load_pairs(y_ref)
    if has_gate:
        y = y * _load_pairs(g_ref)
    xn = x_ref[...] + _dot(y.astype(BF16), wo_ref[...])
    hid = _dot(_rms(xn, ln_ref[...]).astype(BF16), w1_ref[...])
    hid = jnp.square(jnp.maximum(hid, 0.0)).astype(BF16)
    out = xn + _dot(hid, w2_ref[...])
    o_ref[...] = out

    if has_kv:
        kv = _dot(_rms(out, kv_ln_ref[...]).astype(BF16), wkv_ref[...])
        half = kv.shape[1] // 2
        k = kv[:, :half]
        ms = _group_sum(k * k, ones_ref[...]) * (1.0 / HEAD)
        k_ref[...] = (k * lax.rsqrt(ms + RMS_EPS) * kg_ref[...]).astype(BF16)
        v_ref[...] = kv[:, half:].astype(BF16)
    if has_q:
        q = _dot(_rms(out, q_ln_ref[...]).astype(BF16), wq_ref[...])
        ms = _group_sum(q * q, ones_ref[...]) * (1.0 / HEAD)
        q_ref[...] = (q * lax.rsqrt(ms + RMS_EPS) * qg_ref[...] * (HEAD ** -0.5 * LOG2E)).astype(BF16)


def _post_mlp(x, y, g, wo, ln, w1, w2, tm, ones=None, kv=None, q=None):
    M, C = x.shape
    tiles_per_row = y.shape[2] // tm
    act = pl.BlockSpec((tm, C), lambda i: (i, 0))
    pairs = pl.BlockSpec((1, C // PAIR, tm, PAIR), lambda i: (i // tiles_per_row, 0, i % tiles_per_row, 0))
    has_gate, has_kv, has_q = g is not None, kv is not None, q is not None
    acts = [x, y] + ([g] if has_gate else [])
    consts = [wo, ln, w1, w2] + ([ones] if has_kv or has_q else []) + list(kv or ()) + list(q or ())
    out_shape = [jax.ShapeDtypeStruct((M, C), F32)]
    out_specs = [act]
    if has_kv:
        n = kv[1].shape[1] // 2
        out_shape += [jax.ShapeDtypeStruct((M, n), BF16)] * 2
        out_specs += [pl.BlockSpec((tm, n), lambda i: (i, 0))] * 2
    if has_q:
        n = q[1].shape[1]
        out_shape.append(jax.ShapeDtypeStruct((M, n), BF16))
        out_specs.append(pl.BlockSpec((tm, n), lambda i: (i, 0)))
    return pl.pallas_call(
        functools.partial(_post_mlp_kernel, has_gate, has_kv, has_q),
        grid=(M // tm,),
        in_specs=[act] + [pairs] * (len(acts) - 1) + [_weight_spec(a) for a in consts],
        out_specs=out_specs,
        out_shape=out_shape,
        compiler_params=pltpu.CompilerParams(
            dimension_semantics=("parallel",), vmem_limit_bytes=VMEM_LIMIT),
        name="post_mlp",
    )(*acts, *map(_weight_arg, consts))


def _swa_bias_init(n_q_heads, slopes_ref, bias_ref):
    W = WINDOW
    qi = lax.broadcasted_iota(jnp.int32, (W, 2 * W), 0)
    kj = lax.broadcasted_iota(jnp.int32, (W, 2 * W), 1)
    dist = qi + W - kj
    in_window = (dist >= 0) & (dist < WINDOW)
    in_window_cur = in_window & (kj >= W)
    distf = dist.astype(F32)
    for head in range(n_q_heads):
        alibi = -(slopes_ref[head] * LOG2E) * distf
        bias_ref[0, head] = jnp.where(in_window, alibi, NEG_BIG)
        bias_ref[1, head] = jnp.where(in_window_cur, alibi, NEG_BIG)


def _swa_tile(n_q_heads, first_in_row, sinks_ref, q_ref, k_refs, v_refs, bias_ref, attn_ref):
    W = WINDOW
    group = n_q_heads // N_KV
    pairs = group // 2
    lane = lax.broadcasted_iota(jnp.int32, (W, PAIR), 1)
    head0 = lane < HEAD
    lane2 = lax.broadcasted_iota(jnp.int32, (2 * W, PAIR), 1)
    head0_kv = lane2 < HEAD
    zero = jnp.zeros((), BF16)

    for qb in range(q_ref.shape[0] // W):
        rows = slice(qb * W, (qb + 1) * W)
        plane = jnp.where(first_in_row, 1, 0) if qb == 0 else 0
        for h in range(N_KV):
            ks = slice(h * PAIR, (h + 1) * PAIR)
            k2 = jnp.concatenate([k_refs[qb][:, ks], k_refs[qb + 1][:, ks]], axis=0)
            v2 = jnp.concatenate([v_refs[qb][:, ks], v_refs[qb + 1][:, ks]], axis=0)
            q_tiles = [q_ref[rows, (h * pairs + j) * PAIR:(h * pairs + j + 1) * PAIR] for j in range(pairs)]
            lhs = jnp.concatenate([jnp.where(head0, t, zero) for t in q_tiles]
                                  + [jnp.where(head0, zero, t) for t in q_tiles], axis=0)
            s_all = _dot_nt(lhs, k2)
            yield
            probs, denoms = [], []
            for i in range(group):
                head = h * group + 2 * (i % pairs) + i // pairs
                s = s_all[i * W:(i + 1) * W, :] + bias_ref[plane, head]
                sink = sinks_ref[head] * LOG2E
                mx = jnp.maximum(jnp.max(s, axis=-1, keepdims=True), sink)
                p = jnp.exp2(s - mx)
                denoms.append(jnp.sum(p, axis=-1, keepdims=True) + jnp.exp2(sink - mx))
                probs.append(p.astype(BF16))
            p_first = jnp.concatenate(probs[:pairs], axis=0)
            p_second = jnp.concatenate(probs[pairs:], axis=0)
            v_stack = jnp.concatenate([jnp.where(head0_kv, v2, zero), jnp.where(head0_kv, zero, v2)], axis=0)
            o = _dot(jnp.concatenate([p_first, p_second], axis=1), v_stack)
            for j in range(pairs):
                den = jnp.where(head0, denoms[j], denoms[pairs + j])
                col = (h * pairs + j) * PAIR
                attn_ref[rows, col:col + PAIR] = (o[j * W:(j + 1) * W, :] / den).astype(attn_ref.dtype)


_MLP_SPLIT = 4


def _mlp_stages(x_ref, attn_ref, wo_ref, ln_ref, w1_ref, w2_ref, result):
    xn = x_ref[...] + _dot(attn_ref[...], wo_ref[...])
    h = _rms(xn, ln_ref[...]).astype(BF16)
    yield
    step = w1_ref.shape[1] // _MLP_SPLIT
    hids = []
    for c in range(_MLP_SPLIT):
        hid = _dot(h, w1_ref[:, c * step:(c + 1) * step])
        hids.append(jnp.square(jnp.maximum(hid, 0.0)).astype(BF16))
        yield
    acc = xn
    for c in range(_MLP_SPLIT):
        acc = acc + _dot(hids[c], w2_ref[c * step:(c + 1) * step, :])
        if c + 1 < _MLP_SPLIT:
            yield
    result.append(acc)


_MLP_STAGES = 2 * _MLP_SPLIT


def _swa_mlp_kernel(has_q, n_q_heads, blocks, tiles_per_row, slopes_ref, sinks_ref, *refs):
    refs = list(refs)
    x_ref, q_ref = refs[:2]
    k_refs = refs[2:3 + blocks]
    v_refs = refs[3 + blocks:4 + 2 * blocks]
    del refs[:4 + 2 * blocks]
    wo_ref, ln_ref, w1_ref, w2_ref = refs[:4]
    del refs[:4]
    if has_q:
        ones_ref, q_ln_ref, wq_ref, qg_ref = refs[:4]
        del refs[:4]
    o_ref = refs.pop(0)
    qn_ref = refs.pop(0) if has_q else None
    attn_ref, bias_ref = refs
    s = pl.program_id(0)

    @pl.when(s == 0)
    def _():
        attn_ref[...] = jnp.zeros_like(attn_ref)
        _swa_bias_init(n_q_heads, slopes_ref, bias_ref)

    cur = s % 2
    first_in_row = s % tiles_per_row == 0
    result = []
    swa = _swa_tile(n_q_heads, first_in_row, sinks_ref, q_ref, k_refs, v_refs, bias_ref, attn_ref.at[cur])
    mlp = _mlp_stages(x_ref, attn_ref.at[1 - cur], wo_ref, ln_ref, w1_ref, w2_ref, result)
    mlp_stages_per_piece = -(-_MLP_STAGES // (blocks * N_KV))
    for _ in swa:
        for _ in range(mlp_stages_per_piece):
            next(mlp, None)
    for _ in mlp:
        pass
    out = result[0]
    o_ref[...] = out
    if has_q:
        q = _dot(_rms(out, q_ln_ref[...]).astype(BF16), wq_ref[...])
        ms = _group_sum(q * q, ones_ref[...]) * (1.0 / HEAD)
        qn_ref[...] = (q * lax.rsqrt(ms + RMS_EPS) * qg_ref[...] * (HEAD ** -0.5 * LOG2E)).astype(BF16)


def _swa_mlp(x, q, k2, v2, slopes, sinks, wo, ln, w1, w2, tm, tokens_per_row, ones=None, q_next=None):
    M, C = x.shape
    W = WINDOW
    steps = M // tm
    blocks = tm // W
    assert tokens_per_row % tm == 0
    has_q = q_next is not None
    t_att = lambda s: jnp.minimum(s, steps - 1)
    t_mlp = lambda s: jnp.maximum(s - 1, 0)
    kv_spec = lambda off: pl.BlockSpec(
        (W, k2.shape[1]), lambda s, *_: (jnp.maximum(t_att(s) * blocks + off, 0), 0))
    act_mlp = pl.BlockSpec((tm, C), lambda s, *_: (t_mlp(s), 0))
    consts = [wo, ln, w1, w2] + ([ones] + list(q_next) if has_q else [])
    out_shape = [jax.ShapeDtypeStruct((M, C), F32)]
    out_specs = [act_mlp]
    if has_q:
        out_shape.append(jax.ShapeDtypeStruct((M, q_next[1].shape[1]), BF16))
        out_specs.append(pl.BlockSpec((tm, q_next[1].shape[1]), lambda s, *_: (t_mlp(s), 0)))
    return pl.pallas_call(
        functools.partial(_swa_mlp_kernel, has_q, C // HEAD, blocks, tokens_per_row // tm),
        grid_spec=pltpu.PrefetchScalarGridSpec(
            num_scalar_prefetch=2,
            grid=(steps + 1,),
            in_specs=[act_mlp, pl.BlockSpec((tm, C), lambda s, *_: (t_att(s), 0)),
                      *[kv_spec(off) for off in range(-1, blocks)] * 2]
                     + [_weight_spec(a) for a in consts],
            out_specs=out_specs,
            scratch_shapes=[pltpu.VMEM((2, tm, C), BF16),
                            pltpu.VMEM((2, C // HEAD, W, 2 * W), F32)],
        ),
        out_shape=out_shape,
        compiler_params=pltpu.CompilerParams(
            dimension_semantics=("arbitrary",), vmem_limit_bytes=VMEM_LIMIT),
        name="swa_mlp",
    )(slopes, sinks, x, q, *[k2] * (blocks + 1), *[v2] * (blocks + 1), *map(_weight_arg, consts))


def _pad_cols(w, n):
    return jnp.pad(w, ((0, 0), (0, n - w.shape[1])))


def _pad_rows(w, n):
    return jnp.pad(w, ((0, n - w.shape[0]), (0, 0)))


def _round_up(n, m):
    return (n + m - 1) // m * m


def _lora(w_in, w_out):
    n = _round_up(w_in.shape[1], 128)
    return _pad_cols(w_in, n).astype(BF16), _pad_rows(w_out, n).astype(BF16)


def _dup_heads(w):
    c, n = w.shape
    w = w.reshape(c, n // HEAD, 1, HEAD)
    return jnp.broadcast_to(w, (c, n // HEAD, 2, HEAD)).reshape(c, 2 * n)


def _block_ones(n):
    i = jnp.arange(n) // HEAD
    return (i[:, None] == i[None, :]).astype(BF16)


def kernel(x, ln_mix, ln_mlp, mlp_w1, mlp_w2, a_mu, a_w_rkv, a_w0, a_w1, a_w2, a_a0, a_a1, a_a2, a_g1, a_g2, a_k_k, a_k_a, a_r_k, a_gn_g, a_gn_b, a_wo, a_v0, a_v1, a_v2, kv_norm, w_kv, k_gain, b_wq, b_q_gain, b_sinks, b_wo):
    B, T, C = x.shape
    M = B * T
    n_a = a_mu.shape[0]
    n_b = b_wq.shape[0]
    n_heads = C // HEAD
    tm_mlp = 512
    tm_pre = 512
    tt = 1024
    ones256 = _block_ones(256)
    row = lambda v: v.reshape(1, -1).astype(F32)
    slopes = jnp.exp2(-8.0 * jnp.arange(1, n_heads + 1, dtype=F32) / n_heads)
    w1_all = mlp_w1.astype(BF16)
    w2_all = mlp_w2.astype(BF16)

    def q_params(i):
        j = i - n_a
        return row(ln_mix[i]), b_wq[j].astype(BF16), jnp.tile(row(b_q_gain[j]), (1, n_heads))

    def mlp(i, x, y, g, wo):
        kv = q = None
        if i == n_a - 1:
            kv = (row(kv_norm), _dup_heads(w_kv).astype(BF16), jnp.tile(row(k_gain), (1, 2 * N_KV)))
        if n_a - 1 <= i < n_a + n_b - 1:
            q = q_params(i + 1)
        return _post_mlp(x.reshape(M, C), y, g, wo.astype(BF16), row(ln_mlp[i]), _Layer(w1_all, i), _Layer(w2_all, i),
                         tm_mlp, ones256, kv, q)

    v_first = None
    k2 = v2 = q = None
    for i in range(n_a + n_b):
        if i < n_a:
            j = i
            zeros = jnp.zeros((1, C), F32)
            vecs = jnp.concatenate(
                [a_mu[j], row(ln_mix[i]), row(a_w0[j]), row(a_a0[j]),
                 row(a_v0[j - 1]) if j > 0 else zeros, row(a_k_k[j]), row(a_k_a[j])]
                + [zeros] * 4, axis=0)
            loras = list(_lora(a_w1[j], a_w2[j]) + _lora(a_a1[j], a_a2[j]) + _lora(a_g1[j], a_g2[j]))
            if j > 0:
                loras += list(_lora(a_v1[j - 1], a_v2[j - 1]))
            r, lw, k, v, av, kk, g = _rwkv_pre(
                x, v_first if j > 0 else None, vecs, a_w_rkv[j].astype(BF16), loras, ones256, tm_pre)
            if j == 0:
                v_first = v
            pvec = jnp.concatenate([row(a_r_k[j]), row(a_gn_g[j]), row(a_gn_b[j])]
                                   + [jnp.zeros((1, C), F32)] * 5, axis=0)
            y = _wkv(r, lw, k, v, av, kk, pvec, tt)
            outs = mlp(i, x, y, g, a_wo[j])
        else:
            outs = _swa_mlp(x.reshape(M, C), q, k2, v2, slopes, b_sinks[i - n_a].astype(F32),
                            b_wo[i - n_a].astype(BF16), row(ln_mlp[i]), _Layer(w1_all, i), _Layer(w2_all, i),
                            tm_mlp, T, ones256, q_params(i + 1) if i + 1 < n_a + n_b else None)
        outs = list(outs)
        x = outs.pop(0).reshape(B, T, C)
        if i == n_a - 1:
            k2 = outs.pop(0)
            v2 = outs.pop(0)
        if outs:
            q = outs.pop(0)
    return x
```
